```python
import jax
import jax.numpy as jnp
from jax import lax
import numpy as np

D_MODEL = 2048
BATCH = 8
SEQ = 2048
DEPTH = 2

N_MIXERS = 4
BRANCH_DIM = 512
NORM_EPS = 1e-6
NEG = -1e30
BIG = 1e9

RWKV_HEADS = 8
RWKV_HEAD_DIM = 64
RWKV_DIM = RWKV_HEADS * RWKV_HEAD_DIM
RWKV_DECAY_LORA = 96
RWKV_ICLR_LORA = 96
RWKV_VALUE_LORA = 64
RWKV_GATE_LORA = 256
RWKV_GN_EPS = 64e-5
RWKV_IN_DIM = 3 * RWKV_DIM + RWKV_DECAY_LORA + RWKV_ICLR_LORA + RWKV_GATE_LORA

MLSTM_HEADS = 4
MLSTM_QK_DIM = 128
MLSTM_V_DIM = 128
MLSTM_QK_WIDTH = MLSTM_HEADS * MLSTM_QK_DIM
MLSTM_DIM = MLSTM_HEADS * MLSTM_V_DIM
MLSTM_CONV = 4
MLSTM_CHUNK = 64
MLSTM_IN_DIM = 2 * MLSTM_QK_WIDTH + 2 * MLSTM_DIM + 2 * MLSTM_HEADS

ATTN_HEAD_DIM = 128
ROPE_DIM = ATTN_HEAD_DIM // 4
ROPE_THETA = 500000.0

MOBA_HEADS = 4
MOBA_DIM = MOBA_HEADS * ATTN_HEAD_DIM
MOBA_BLOCK = 256
MOBA_TOPK = 3
MOBA_Q_CHUNK = 16
MOBA_IN_DIM = 3 * MOBA_DIM

NSA_HEADS = 4
NSA_DIM = NSA_HEADS * ATTN_HEAD_DIM
NSA_CMP_BLOCK = 32
NSA_CMP_STRIDE = 16
NSA_CMP_HIDDEN = 256
NSA_SLC_BLOCK = 64
NSA_SLC_TOPK = 16
NSA_WINDOW = 512
NSA_WIN_QBLOCK = 128
NSA_Q_CHUNK = 32
NSA_IN_DIM = NSA_DIM + 6 * ATTN_HEAD_DIM + 3 * NSA_HEADS

MERGE_IN_DIM = N_MIXERS * D_MODEL
IN_WIDTHS = (RWKV_IN_DIM, MLSTM_IN_DIM, MOBA_IN_DIM, NSA_IN_DIM, MERGE_IN_DIM)
IN_DIM = RWKV_IN_DIM + MLSTM_IN_DIM + MOBA_IN_DIM + NSA_IN_DIM + MERGE_IN_DIM

N_EXPERTS = 64
TOP_K = 8
N_GROUPS = 8
TOPK_GROUPS = 4
EXPERT_DIM = 512
SHARED_DIM = 512
ROUTED_SCALE = 2.5
MOE_BLOCK = 256

kernel_name = 'hybrid_rwkv7_mlstm_moba_nsa_moe_block'


def _split(z, widths):
    cuts = [int(i) for i in np.cumsum(widths)[:-1]]
    return jnp.split(z, cuts, axis=-1)


def _rms(z, g, eps=NORM_EPS):
    zf = z.astype(jnp.float32)
    y = zf * lax.rsqrt(jnp.mean(zf * zf, axis=-1, keepdims=True) + eps)
    return (y * g).astype(z.dtype)


def _heads(z, n_heads):
    B, S, C = z.shape
    return z.reshape(B, S, n_heads, C // n_heads).transpose(0, 2, 1, 3)


def _merge_heads(z):
    B, H, S, d = z.shape
    return z.transpose(0, 2, 1, 3).reshape(B, S, H * d)


def _rope(z, pos):
    half = ROPE_DIM // 2
    inv_freq = ROPE_THETA ** (-jnp.arange(half, dtype=jnp.float32) / half)
    ang = pos.astype(jnp.float32)[:, None] * inv_freq[None, :]
    cos = jnp.cos(ang).astype(z.dtype)
    sin = jnp.sin(ang).astype(z.dtype)
    z1, z2, zr = z[..., :half], z[..., half:ROPE_DIM], z[..., ROPE_DIM:]
    return jnp.concatenate([z1 * cos - z2 * sin, z2 * cos + z1 * sin, zr], axis=-1)


def _token_shift(z):
    return jnp.pad(z, ((0, 0), (1, 0), (0, 0)))[:, :-1]


def _causal_conv(z, w, b):
    K, C = w.shape
    zp = jnp.pad(z, ((0, 0), (K - 1, 0), (0, 0)))
    y = lax.conv_general_dilated(zp, w[:, None, :].astype(z.dtype), window_strides=(1,), padding='VALID',
                                 dimension_numbers=('NWC', 'WIO', 'NWC'), feature_group_count=C)
    return y + b


def _swiglu(z, wg, wu, wd):
    return (jax.nn.silu(z @ wg) * (z @ wu)) @ wd


def _rwkv7(p, v_first, v_mix, mu, w0, w_up, a0, a_up, g_up, k_k, k_a, r_k, ln_w, ln_b):
    B, S, _ = p.shape
    H, N = RWKV_HEADS, RWKV_HEAD_DIM
    xs = p + (_token_shift(p) - p) * mu
    r, k, v, xw, xa, xg = _split(xs, (RWKV_DIM, RWKV_DIM, RWKV_DIM, RWKV_DECAY_LORA, RWKV_ICLR_LORA, RWKV_GATE_LORA))
    w = -jax.nn.softplus(-(w0 + jnp.tanh(xw) @ w_up)) - 0.5
    decay = jnp.exp(-jnp.exp(w.astype(jnp.float32)))
    a = jax.nn.sigmoid(a0 + xa @ a_up)
    g = jax.nn.sigmoid(xg) @ g_up
    kk = (k * k_k).reshape(B, S, H, N).astype(jnp.float32)
    kk = kk / jnp.maximum(jnp.sqrt(jnp.sum(kk * kk, axis=-1, keepdims=True)), 1e-12)
    k = k * (1.0 + (a - 1.0) * k_a)
    if v_mix is None:
        v_first = v
    else:
        v0, v_down, v_up = v_mix
        v = v + (v_first - v) * jax.nn.sigmoid(v0 + (v @ v_down) @ v_up)

    def to_time(z):
        return jnp.moveaxis(z.reshape(B, S, H, N).astype(jnp.float32), 1, 0)

    def step(state, inp):
        r_t, w_t, k_t, v_t, kk_t, a_t = inp
        sa = jnp.einsum('bhij,bhj->bhi', state, kk_t)
        state = (state * w_t[:, :, None, :] - sa[..., None] * (kk_t * a_t)[:, :, None, :]
                 + v_t[..., None] * k_t[:, :, None, :])
        return state, jnp.einsum('bhij,bhj->bhi', state, r_t)

    xs_t = (to_time(r), to_time(decay), to_time(k), to_time(v), jnp.moveaxis(kk, 1, 0), to_time(a))
    _, y = lax.scan(step, jnp.zeros((B, H, N, N), jnp.float32), xs_t)
    y = jnp.moveaxis(y, 0, 1)
    mean = jnp.mean(y, axis=-1, keepdims=True)
    var = jnp.mean(jnp.square(y - mean), axis=-1, keepdims=True)
    y = ((y - mean) * lax.rsqrt(var + RWKV_GN_EPS)).reshape(B, S, RWKV_DIM) * ln_w + ln_b
    bonus = jnp.sum((r * k * r_k).reshape(B, S, H, N), axis=-1, keepdims=True) * v.reshape(B, S, H, N)
    y = (y + bonus.reshape(B, S, RWKV_DIM)) * g
    return y.astype(p.dtype), v_first


def _mlstm_chunkwise(q, k, v, i_pre, logf):
    B, H, S, dk = q.shape
    dv = v.shape[-1]
    L = MLSTM_CHUNK
    NC = S // L
    q = q.reshape(B, H, NC, L, dk)
    k = k.reshape(B, H, NC, L, dk)
    v = v.reshape(B, H, NC, L, dv)
    ic = i_pre.reshape(B, H, NC, L)
    bcum = jnp.cumsum(logf.reshape(B, H, NC, L), axis=-1)
    gtot = bcum[..., -1]
    a_log = gtot[..., None] - bcum + ic
    a_max = jnp.max(a_log, axis=-1)

    def step(carry, inp):
        C, n, m = carry
        g_c, a_c, amax_c, k_c, v_c = inp
        m_new = jnp.maximum(g_c + m, amax_c)
        wgt = jnp.exp(a_c - m_new[..., None])
        dec = jnp.exp(g_c + m - m_new)
        C_new = dec[..., None, None] * C + jnp.einsum('bhl,bhlk,bhlv->bhkv', wgt, k_c, v_c)
        n_new = dec[..., None] * n + jnp.einsum('bhl,bhlk->bhk', wgt, k_c)
        return (C_new, n_new, m_new), (C, n, m)

    init = (jnp.zeros((B, H, dk, dv), jnp.float32), jnp.zeros((B, H, dk), jnp.float32), jnp.zeros((B, H), jnp.float32))
    xs = (jnp.moveaxis(gtot, 2, 0), jnp.moveaxis(a_log, 2, 0), jnp.moveaxis(a_max, 2, 0),
          jnp.moveaxis(k, 2, 0), jnp.moveaxis(v, 2, 0))
    _, (C_prev, n_prev, m_prev) = lax.scan(step, init, xs)
    C_prev = jnp.moveaxis(C_prev, 0, 2)
    n_prev = jnp.moveaxis(n_prev, 0, 2)
    m_prev = jnp.moveaxis(m_prev, 0, 2)

    causal = jnp.tril(jnp.ones((L, L), dtype=bool))
    logD = jnp.where(causal, bcum[..., :, None] - bcum[..., None, :] + ic[..., None, :], -jnp.inf)
    m_inter = bcum + m_prev[..., None]
    m_t = jnp.maximum(m_inter, jnp.max(logD, axis=-1))
    Dm = jnp.exp(logD - m_t[..., None])
    sqk = jnp.einsum('bhctd,bhcsd->bhcts', q, k) * Dm
    inter = jnp.exp(m_inter - m_t)
    num = jnp.einsum('bhcts,bhcsv->bhctv', sqk, v) + inter[..., None] * jnp.einsum('bhctk,bhckv->bhctv', q, C_prev)
    den = jnp.sum(sqk, axis=-1) + inter * jnp.einsum('bhctk,bhck->bhct', q, n_prev)
    h = num / jnp.maximum(jnp.abs(den), jnp.exp(-m_t))[..., None]
    return h.reshape(B, H, S, dv)


def _mlstm(p, conv_w, conv_b, i_b, f_b, norm_g):
    H = MLSTM_HEADS
    qk, v, o, ig, fg = _split(p, (2 * MLSTM_QK_WIDTH, MLSTM_DIM, MLSTM_DIM, H, H))
    qk = jax.nn.silu(_causal_conv(qk, conv_w, conv_b))
    q, k = jnp.split(qk, 2, axis=-1)
    q = _heads(q, H).astype(jnp.float32) * (MLSTM_QK_DIM ** -0.5)
    k = _heads(k, H).astype(jnp.float32)
    v = _heads(v, H).astype(jnp.float32)
    i_pre = jnp.moveaxis((ig + i_b).astype(jnp.float32), -1, 1)
    logf = jax.nn.log_sigmoid(jnp.moveaxis((fg + f_b).astype(jnp.float32), -1, 1))
    h = _mlstm_chunkwise(q, k, v, i_pre, logf)
    h = _rms(h, norm_g.reshape(H, 1, MLSTM_V_DIM))
    return (jax.nn.sigmoid(o) * _merge_heads(h).astype(p.dtype)).astype(p.dtype)


def _moba(q, k, v):
    B, H, S, d = q.shape
    BLK, QC = MOBA_BLOCK, MOBA_Q_CHUNK
    nb = -(-S // BLK)
    pad = nb * BLK - S
    kp = jnp.pad(k, ((0, 0), (0, 0), (0, pad), (0, 0)))
    vp = jnp.pad(v, ((0, 0), (0, 0), (0, pad), (0, 0)))
    kb = kp.reshape(B, H, nb, BLK, d)
    vb = vp.reshape(B, H, nb, BLK, d)
    qblk = jnp.arange(S) // BLK
    gate = jnp.einsum('bhsd,bhnd->bhsn', q, jnp.mean(kb, axis=3)).astype(jnp.float32)
    gate = jnp.where(jnp.arange(nb)[None, :] < qblk[:, None], gate, NEG)
    n_top = min(MOBA_TOPK, nb)
    _, sel = lax.top_k(gate, n_top)
    nq = S // QC
    bi = jnp.arange(B)[:, None, None, None]
    hi = jnp.arange(H)[None, :, None, None]
    scale = d ** -0.5

    def chunk(args):
        ci, q_c, sel_c = args
        t = ci * QC + jnp.arange(QC)
        j = (ci * QC) // BLK
        kg = kb[bi, hi, sel_c].reshape(B, H, QC, n_top * BLK, d)
        vg = vb[bi, hi, sel_c].reshape(B, H, QC, n_top * BLK, d)
        s_sel = jnp.einsum('bhqd,bhqkd->bhqk', q_c, kg).astype(jnp.float32) * scale
        slot_ok = jnp.repeat(jnp.arange(n_top) < j, BLK)
        s_sel = jnp.where(slot_ok, s_sel, NEG)
        k_own = lax.dynamic_slice_in_dim(kp, j * BLK, BLK, axis=2)
        v_own = lax.dynamic_slice_in_dim(vp, j * BLK, BLK, axis=2)
        s_own = jnp.einsum('bhqd,bhkd->bhqk', q_c, k_own).astype(jnp.float32) * scale
        s_own = jnp.where((j * BLK + jnp.arange(BLK))[None, :] <= t[:, None], s_own, NEG)
        pr = jax.nn.softmax(jnp.concatenate([s_sel, s_own], axis=-1), axis=-1).astype(v.dtype)
        return (jnp.einsum('bhqk,bhqkd->bhqd', pr[..., :n_top * BLK], vg)
                + jnp.einsum('bhqk,bhkd->bhqd', pr[..., n_top * BLK:], v_own))

    q_chunks = jnp.moveaxis(q.reshape(B, H, nq, QC, d), 2, 0)
    sel_chunks = jnp.moveaxis(sel.reshape(B, H, nq, QC, n_top), 2, 0)
    out = lax.map(chunk, (jnp.arange(nq), q_chunks, sel_chunks))
    return jnp.moveaxis(out, 0, 2).reshape(B, H, S, d)


def _nsa(p, pos, q_norm, k_norm, cmp_pe, cmp_w1, cmp_b1, cmp_w2):
    B, S, _ = p.shape
    H, d = NSA_HEADS, ATTN_HEAD_DIM
    scale = d ** -0.5
    q, kc_in, vc_in, ks_in, vs_in, kw_in, vw_in, gl = _split(p, (NSA_DIM, d, d, d, d, d, d, 3 * H))
    q = _rope(_rms(_heads(q, H), q_norm), pos)

    ncmp = (S - NSA_CMP_BLOCK) // NSA_CMP_STRIDE + 1
    cstart = NSA_CMP_STRIDE * jnp.arange(ncmp)
    cend = cstart + NSA_CMP_BLOCK - 1
    cidx = cstart[:, None] + jnp.arange(NSA_CMP_BLOCK)[None, :]

    def compress(z, i):
        blocks = (z[:, cidx] + cmp_pe[i]).reshape(B, ncmp, NSA_CMP_BLOCK * d)
        return jax.nn.gelu(blocks @ cmp_w1[i] + cmp_b1[i]) @ cmp_w2[i]

    k_cmp = _rope(_rms(compress(kc_in, 0), k_norm[0]), cend)
    v_cmp = compress(vc_in, 1)
    s_c = jnp.einsum('bhsd,bnd->bhsn', q, k_cmp).astype(jnp.float32) * scale
    cvalid = cend[None, :] <= pos[:, None]
    p_cmp = jax.nn.softmax(jnp.where(cvalid, s_c, NEG), axis=-1)
    p_cmp = jnp.where(jnp.any(cvalid, axis=-1)[:, None], p_cmp, 0.0)
    o_cmp = jnp.einsum('bhsn,bnd->bhsd', p_cmp.astype(v_cmp.dtype), v_cmp)

    SB, QC = NSA_SLC_BLOCK, NSA_Q_CHUNK
    nsb = S // SB
    sstart = SB * jnp.arange(nsb)
    overlap = ((cstart[:, None] <= sstart[None, :] + SB - 1) & (cend[:, None] >= sstart[None, :])).astype(jnp.float32)
    imp = jnp.einsum('bhsn,nj->bsj', p_cmp, overlap)
    cur = pos // SB
    blk = jnp.arange(nsb)[None, :]
    forced = (blk == 0) | (blk == cur[:, None]) | (blk == cur[:, None] - 1)
    imp = jnp.where(forced, BIG, jnp.where(blk <= cur[:, None], imp, NEG))
    n_sel = min(NSA_SLC_TOPK, nsb)
    _, sel = lax.top_k(imp, n_sel)
    k_slc = _rope(_rms(ks_in, k_norm[1]), pos).reshape(B, nsb, SB, d)
    v_slc = vs_in.reshape(B, nsb, SB, d)
    bi = jnp.arange(B)[:, None, None]

    def slc_chunk(args):
        ci, q_c, sel_c = args
        t = ci * QC + jnp.arange(QC)
        kg = k_slc[bi, sel_c].reshape(B, QC, n_sel * SB, d)
        vg = v_slc[bi, sel_c].reshape(B, QC, n_sel * SB, d)
        kpos = (sel_c[..., None] * SB + jnp.arange(SB)).reshape(B, QC, n_sel * SB)
        sc = jnp.einsum('bhqd,bqkd->bhqk', q_c, kg).astype(jnp.float32) * scale
        sc = jnp.where((kpos <= t[None, :, None])[:, None], sc, NEG)
        pr = jax.nn.softmax(sc, axis=-1).astype(vg.dtype)
        return jnp.einsum('bhqk,bqkd->bhqd', pr, vg)

    nq = S // QC
    q_chunks = jnp.moveaxis(q.reshape(B, H, nq, QC, d), 2, 0)
    sel_chunks = jnp.moveaxis(sel.reshape(B, nq, QC, n_sel), 1, 0)
    o_slc = lax.map(slc_chunk, (jnp.arange(nq), q_chunks, sel_chunks))
    o_slc = jnp.moveaxis(o_slc, 0, 2).reshape(B, H, S, d)

    WQ, W = NSA_WIN_QBLOCK, NSA_WINDOW
    nqb = S // WQ
    nband = W // WQ + 1
    k_win = _rope(_rms(kw_in, k_norm[2]), pos)
    kwp = jnp.pad(k_win, ((0, 0), (W, 0), (0, 0))).reshape(B, nqb + W // WQ, WQ, d)
    vwp = jnp.pad(vw_in, ((0, 0), (W, 0), (0, 0))).reshape(B, nqb + W // WQ, WQ, d)
    band_k = jnp.concatenate([kwp[:, i:i + nqb] for i in range(nband)], axis=2)
    band_v = jnp.concatenate([vwp[:, i:i + nqb] for i in range(nband)], axis=2)
    sw = jnp.einsum('bhnqd,bnkd->bhnqk', q.reshape(B, H, nqb, WQ, d), band_k).astype(jnp.float32) * scale
    tq = jnp.arange(nqb)[:, None] * WQ + jnp.arange(WQ)[None, :]
    tk = jnp.arange(nqb)[:, None] * WQ - W + jnp.arange(nband * WQ)[None, :]
    win_ok = ((tk[:, None, :] <= tq[:, :, None]) & (tk[:, None, :] > tq[:, :, None] - W) & (tk[:, None, :] >= 0))
    pw = jax.nn.softmax(jnp.where(win_ok, sw, NEG), axis=-1).astype(band_v.dtype)
    o_win = jnp.einsum('bhnqk,bnkd->bhnqd', pw, band_v).reshape(B, H, S, d)

    gates = jax.nn.sigmoid(gl).reshape(B, S, H, 3).transpose(0, 2, 1, 3)
    o = gates[..., 0:1] * o_cmp + gates[..., 1:2] * o_slc + gates[..., 2:3] * o_win
    return _merge_heads(o).astype(p.dtype)


def _moe(h, router, bias, w_gate, w_up, w_down, s_gate, s_up, s_down):
    B, S, D = h.shape
    T = B * S
    A = T * TOP_K
    xt = h.reshape(T, D)
    scores = jax.nn.sigmoid((xt @ router).astype(jnp.float32))
    biased = scores + bias
    grp_score = lax.top_k(biased.reshape(T, N_GROUPS, N_EXPERTS // N_GROUPS), 2)[0].sum(-1)
    _, gidx = lax.top_k(grp_score, TOPK_GROUPS)
    gmask = jnp.any(gidx[:, :, None] == jnp.arange(N_GROUPS)[None, None, :], axis=1)
    emask = jnp.repeat(gmask, N_EXPERTS // N_GROUPS, axis=1)
    _, eidx = lax.top_k(jnp.where(emask, biased, NEG), TOP_K)
    wts = jnp.take_along_axis(scores, eidx, axis=1)
    wts = wts / jnp.sum(wts, axis=-1, keepdims=True) * ROUTED_SCALE

    e_flat = eidx.reshape(A)
    order = jnp.argsort(e_flat)
    e_s = e_flat[order]
    tok_s = order // TOP_K
    w_s = wts.reshape(A)[order]
    counts = jnp.bincount(e_flat, length=N_EXPERTS)
    padded = (counts + MOE_BLOCK - 1) // MOE_BLOCK * MOE_BLOCK
    start = jnp.cumsum(counts) - counts
    pstart = jnp.cumsum(padded) - padded
    dest = pstart[e_s] + jnp.arange(A) - start[e_s]
    n_blk = -(-(A + N_EXPERTS * (MOE_BLOCK - 1)) // MOE_BLOCK)
    tok_buf = jnp.zeros((n_blk * MOE_BLOCK,), jnp.int32).at[dest].set(tok_s.astype(jnp.int32))
    w_buf = jnp.zeros((n_blk * MOE_BLOCK,), jnp.float32).at[dest].set(w_s)
    blk_e = jnp.minimum(jnp.sum((pstart + padded)[None, :] <= (jnp.arange(n_blk) * MOE_BLOCK)[:, None], axis=1),
                        N_EXPERTS - 1)

    def expert_block(acc, args):
        e, tok, wt = args
        yb = _swiglu(xt[tok], w_gate[e], w_up[e], w_down[e]) * wt[:, None].astype(xt.dtype)
        return acc.at[tok].add(yb.astype(acc.dtype)), None

    routed, _ = lax.scan(expert_block, jnp.zeros_like(xt),
                         (blk_e, tok_buf.reshape(n_blk, MOE_BLOCK), w_buf.reshape(n_blk, MOE_BLOCK)))
    return (routed + _swiglu(xt, s_gate, s_up, s_down)).reshape(B, S, D)


def setup_inputs(seed: int = 0) -> dict:
    key = jax.random.key(seed)
    keys = iter(jax.random.split(key, 64))
    L, D = DEPTH, D_MODEL

    def nrm(shape, scale):
        return jax.random.normal(next(keys), shape, jnp.float32) * scale

    def gain(shape):
        return 1.0 + nrm(shape, 0.05)

    return {
        'x': nrm((BATCH, SEQ, D), 1.0),
        'c': nrm((BATCH, D), 1.0),
        'ada_w': nrm((L, D, 6 * D), 0.5 * D ** -0.5),
        'ada_b': nrm((L, 6 * D), 0.02),
        'norm_mix_g': gain((L, D)),
        'norm_ffn_g': gain((L, D)),
        'w_in': nrm((L, D, IN_DIM), D ** -0.5),
        'rwkv_mu': jax.random.uniform(next(keys), (L, RWKV_IN_DIM), jnp.float32),
        'rwkv_w0': -1.0 + nrm((L, RWKV_DIM), 0.5),
        'rwkv_w_up': nrm((L, RWKV_DECAY_LORA, RWKV_DIM), 0.5 * RWKV_DECAY_LORA ** -0.5),
        'rwkv_a0': nrm((L, RWKV_DIM), 0.1),
        'rwkv_a_up': nrm((L, RWKV_ICLR_LORA, RWKV_DIM), 0.5 * RWKV_ICLR_LORA ** -0.5),
        'rwkv_g_up': nrm((L, RWKV_GATE_LORA, RWKV_DIM), RWKV_GATE_LORA ** -0.5),
        'rwkv_k_k': 0.85 + nrm((L, RWKV_DIM), 0.05),
        'rwkv_k_a': gain((L, RWKV_DIM)),
        'rwkv_r_k': nrm((L, RWKV_DIM), 0.1),
        'rwkv_ln_w': gain((L, RWKV_DIM)),
        'rwkv_ln_b': nrm((L, RWKV_DIM), 0.02),
        'rwkv_v0': 1.0 + nrm((L - 1, RWKV_DIM), 0.1),
        'rwkv_v_down': nrm((L - 1, RWKV_DIM, RWKV_VALUE_LORA), RWKV_DIM ** -0.5),
        'rwkv_v_up': nrm((L - 1, RWKV_VALUE_LORA, RWKV_DIM), 0.5 * RWKV_VALUE_LORA ** -0.5),
        'mlstm_conv_w': nrm((L, MLSTM_CONV, 2 * MLSTM_QK_WIDTH), 0.5),
        'mlstm_conv_b': nrm((L, 2 * MLSTM_QK_WIDTH), 0.02),
        'mlstm_i_b': nrm((L, MLSTM_HEADS), 0.5),
        'mlstm_f_b': 3.0 + 3.0 * jax.random.uniform(next(keys), (L, MLSTM_HEADS), jnp.float32),
        'mlstm_norm_g': gain((L, MLSTM_DIM)),
        'moba_q_norm': gain((L, ATTN_HEAD_DIM)),
        'moba_k_norm': gain((L, ATTN_HEAD_DIM)),
        'nsa_q_norm': gain((L, ATTN_HEAD_DIM)),
        'nsa_k_norm': gain((L, 3, ATTN_HEAD_DIM)),
        'nsa_cmp_pe': nrm((L, 2, NSA_CMP_BLOCK, ATTN_HEAD_DIM), 0.02),
        'nsa_cmp_w1': nrm((L, 2, NSA_CMP_BLOCK * ATTN_HEAD_DIM, NSA_CMP_HIDDEN), (NSA_CMP_BLOCK * ATTN_HEAD_DIM) ** -0.5),
        'nsa_cmp_b1': nrm((L, 2, NSA_CMP_HIDDEN), 0.02),
        'nsa_cmp_w2': nrm((L, 2, NSA_CMP_HIDDEN, ATTN_HEAD_DIM), NSA_CMP_HIDDEN ** -0.5),
        'w_branch': nrm((L, N_MIXERS, BRANCH_DIM, D), BRANCH_DIM ** -0.5),
        'w_out': nrm((L, D, D), D ** -0.5),
        'moe_router': nrm((L, D, N_EXPERTS), D ** -0.5),
        'moe_bias': nrm((L, N_EXPERTS), 0.01),
        'moe_w_gate': nrm((L, N_EXPERTS, D, EXPERT_DIM), D ** -0.5),
        'moe_w_up': nrm((L, N_EXPERTS, D, EXPERT_DIM), D ** -0.5),
        'moe_w_down': nrm((L, N_EXPERTS, EXPERT_DIM, D), EXPERT_DIM ** -0.5),
        'shared_w_gate': nrm((L, D, SHARED_DIM), D ** -0.5),
        'shared_w_up': nrm((L, D, SHARED_DIM), D ** -0.5),
        'shared_w_down': nrm((L, SHARED_DIM, D), SHARED_DIM ** -0.5),
    }


def reference(x, c, ada_w, ada_b, norm_mix_g, norm_ffn_g, w_in,
              rwkv_mu, rwkv_w0, rwkv_w_up, rwkv_a0, rwkv_a_up, rwkv_g_up, rwkv_k_k, rwkv_k_a, rwkv_r_k,
              rwkv_ln_w, rwkv_ln_b, rwkv_v0, rwkv_v_down, rwkv_v_up,
              mlstm_conv_w, mlstm_conv_b, mlstm_i_b, mlstm_f_b, mlstm_norm_g,
              moba_q_norm, moba_k_norm,
              nsa_q_norm, nsa_k_norm, nsa_cmp_pe, nsa_cmp_w1, nsa_cmp_b1, nsa_cmp_w2,
              w_branch, w_out,
              moe_router, moe_bias, moe_w_gate, moe_w_up, moe_w_down,
              shared_w_gate, shared_w_up, shared_w_down):
    B, S, D = x.shape
    pos = jnp.arange(S)
    cond = jax.nn.silu(c)
    v_first = None
    for l in range(DEPTH):
        mod = cond @ ada_w[l] + ada_b[l]
        sh_mix, sc_mix, gate_mix, sh_ffn, sc_ffn, gate_ffn = [z[:, None, :] for z in jnp.split(mod, 6, axis=-1)]

        h = _rms(x, norm_mix_g[l]) * (1.0 + sc_mix) + sh_mix
        p_a, p_b, p_c, p_d, p_g = _split(h @ w_in[l], IN_WIDTHS)
        v_mix = None if l == 0 else (rwkv_v0[l - 1], rwkv_v_down[l - 1], rwkv_v_up[l - 1])
        y_a, v_first = _rwkv7(p_a, v_first, v_mix, rwkv_mu[l], rwkv_w0[l], rwkv_w_up[l], rwkv_a0[l], rwkv_a_up[l],
                              rwkv_g_up[l], rwkv_k_k[l], rwkv_k_a[l], rwkv_r_k[l], rwkv_ln_w[l], rwkv_ln_b[l])
        y_b = _mlstm(p_b, mlstm_conv_w[l], mlstm_conv_b[l], mlstm_i_b[l], mlstm_f_b[l], mlstm_norm_g[l])
        q_c, k_c, v_c = _split(p_c, (MOBA_DIM, MOBA_DIM, MOBA_DIM))
        q_c = _rope(_rms(_heads(q_c, MOBA_HEADS), moba_q_norm[l]), pos)
        k_c = _rope(_rms(_heads(k_c, MOBA_HEADS), moba_k_norm[l]), pos)
        y_c = _merge_heads(_moba(q_c, k_c, _heads(v_c, MOBA_HEADS)))
        y_d = _nsa(p_d, pos, nsa_q_norm[l], nsa_k_norm[l], nsa_cmp_pe[l], nsa_cmp_w1[l], nsa_cmp_b1[l], nsa_cmp_w2[l])
        gates = jax.nn.sigmoid(p_g.reshape(B, S, N_MIXERS, D))
        merged = None
        for m, y_m in enumerate((y_a, y_b, y_c, y_d)):
            term = gates[:, :, m] * (y_m @ w_branch[l, m])
            merged = term if merged is None else merged + term
        x = x + gate_mix * (merged @ w_out[l])

        h = _rms(x, norm_ffn_g[l]) * (1.0 + sc_ffn) + sh_ffn
        x = x + gate_ffn * _moe(h, moe_router[l], moe_bias[l], moe_w_gate[l], moe_w_up[l], moe_w_down[l],
                                shared_w_gate[l], shared_w_up[l], shared_w_down[l])
    return x
```

```python
import functools

import numpy as np
import jax
import jax.numpy as jnp
from jax import lax
from jax.experimental import pallas as pl
from jax.experimental.pallas import tpu as pltpu

F32 = jnp.float32
BF16 = jnp.bfloat16
HIGHEST = lax.Precision.HIGHEST

D_MODEL = 2048
N_MIXERS = 4
BRANCH_DIM = 512
NORM_EPS = 1e-6
NEG = -1e30
BIG = 1e9

RWKV_HEADS = 8
RWKV_HEAD_DIM = 64
RWKV_DIM = RWKV_HEADS * RWKV_HEAD_DIM
RWKV_DECAY_LORA = 96
RWKV_ICLR_LORA = 96
RWKV_GATE_LORA = 256
RWKV_GN_EPS = 64e-5
RWKV_IN_DIM = 3 * RWKV_DIM + RWKV_DECAY_LORA + RWKV_ICLR_LORA + RWKV_GATE_LORA

MLSTM_HEADS = 4
MLSTM_QK_DIM = 128
MLSTM_V_DIM = 128
MLSTM_QK_WIDTH = MLSTM_HEADS * MLSTM_QK_DIM
MLSTM_DIM = MLSTM_HEADS * MLSTM_V_DIM
MLSTM_CONV = 4
MLSTM_CHUNK = 64
MLSTM_IN_DIM = 2 * MLSTM_QK_WIDTH + 2 * MLSTM_DIM + 2 * MLSTM_HEADS

ATTN_HEAD_DIM = 128
ROPE_DIM = ATTN_HEAD_DIM // 4
ROPE_THETA = 500000.0

MOBA_HEADS = 4
MOBA_DIM = MOBA_HEADS * ATTN_HEAD_DIM
MOBA_BLOCK = 256
MOBA_TOPK = 3
MOBA_IN_DIM = 3 * MOBA_DIM

NSA_HEADS = 4
NSA_DIM = NSA_HEADS * ATTN_HEAD_DIM
NSA_CMP_BLOCK = 32
NSA_CMP_STRIDE = 16
NSA_SLC_BLOCK = 64
NSA_SLC_TOPK = 16
NSA_WINDOW = 512
NSA_IN_DIM = NSA_DIM + 6 * ATTN_HEAD_DIM + 3 * NSA_HEADS

MERGE_IN_DIM = N_MIXERS * D_MODEL
IN_WIDTHS = (RWKV_IN_DIM, MLSTM_IN_DIM, MOBA_IN_DIM, NSA_IN_DIM, MERGE_IN_DIM)

N_EXPERTS = 64
TOP_K = 8
N_GROUPS = 8
TOPK_GROUPS = 4
EXPERT_DIM = 512
ROUTED_SCALE = 2.5
MOE_BLOCK = 256

V7X_VMEM_LIMIT_BYTES = 48 * 1024 * 1024
LANES = 128
SUBLANES = 8


def _mm_kernel(a_ref, b_ref, o_ref):
    o_ref[...] = jnp.dot(a_ref[...], b_ref[...], preferred_element_type=F32).astype(o_ref.dtype)


def _pick_tile(n, pref):
    t = min(pref, n)
    while n % t:
        t //= 2
    return t


def _matmul(a, b, out_dtype=F32, tm=1024, tn=512):
    M, K = a.shape
    _, N = b.shape
    if N % tn:
        n_pad = -(-N // tn) * tn
        return _matmul(a, jnp.pad(b, ((0, 0), (0, n_pad - N))), out_dtype, tm, tn)[:, :N]
    tm = _pick_tile(M, tm)
    return pl.pallas_call(
        _mm_kernel,
        grid=(N // tn, M // tm),
        in_specs=[pl.BlockSpec((tm, K), lambda j, i: (i, 0)),
                  pl.BlockSpec((K, tn), lambda j, i: (0, j))],
        out_specs=pl.BlockSpec((tm, tn), lambda j, i: (i, j)),
        out_shape=jax.ShapeDtypeStruct((M, N), out_dtype),
        compiler_params=pltpu.CompilerParams(dimension_semantics=("parallel", "parallel"),
                                             vmem_limit_bytes=V7X_VMEM_LIMIT_BYTES),
        name="dense_matmul",
    )(a, b)


RWKV_TIME_CHUNK = 32
RWKV_ROWS = RWKV_HEAD_DIM // 2
RWKV_ROW_GROUPS = RWKV_ROWS // SUBLANES
RWKV_KEY_TILES = RWKV_HEAD_DIM // SUBLANES


def _rwkv_scan_kernel(r_ref, w_ref, k_ref, kk_ref, b_ref, v_ref, y_ref, st_ref):
    @pl.when(pl.program_id(0) == 0)
    def _():
        st_ref[...] = jnp.zeros_like(st_ref)

    n_t = r_ref.shape[0]
    sub_iota = lax.broadcasted_iota(jnp.int32, (SUBLANES, st_ref.shape[-1]), 0)

    def step(t, carry):
        r = r_ref[t]
        w = w_ref[t]
        k = k_ref[t]
        kk = kk_ref[t]
        bb = b_ref[t]

        def group(g, c2):
            y_tile = jnp.zeros(sub_iota.shape, F32)
            for ii in range(SUBLANES):
                i = g * SUBLANES + ii
                s = st_ref[i]
                sa = jnp.sum(jnp.sum(s * kk, axis=0), axis=0, keepdims=True)
                vrow = v_ref[t, g, pl.ds(ii, 1), :]
                s = s * w - sa[None] * bb + vrow[None] * k
                st_ref[i] = s
                yrow = jnp.sum(jnp.sum(s * r, axis=0), axis=0, keepdims=True)
                y_tile = jnp.where(sub_iota == ii, jnp.broadcast_to(yrow, sub_iota.shape), y_tile)
            y_ref[t, g] = y_tile
            return c2

        lax.fori_loop(0, RWKV_ROW_GROUPS, group, 0)
        return carry

    lax.fori_loop(0, n_t, step, 0)


def _rwkv_scan(r, w, k, v, kk, bvec):
    B, S, H, N = r.shape
    chains = B * H
    lanes = 2 * chains

    def key_layout(z):
        zt = jnp.transpose(z, (1, 3, 0, 2)).reshape(S, N, chains)
        zt = jnp.concatenate([zt, zt], axis=-1)
        return zt.reshape(S, RWKV_KEY_TILES, SUBLANES, lanes)

    def val_layout(z):
        zt = jnp.transpose(z.reshape(B, S, H, 2, RWKV_ROWS), (1, 4, 3, 0, 2))
        return zt.reshape(S, RWKV_ROW_GROUPS, SUBLANES, lanes)

    tc = _pick_tile(S, RWKV_TIME_CHUNK)
    kspec = pl.BlockSpec((tc, RWKV_KEY_TILES, SUBLANES, lanes), lambda c: (c, 0, 0, 0))
    vspec = pl.BlockSpec((tc, RWKV_ROW_GROUPS, SUBLANES, lanes), lambda c: (c, 0, 0, 0))
    y = pl.pallas_call(
        _rwkv_scan_kernel,
        grid=(S // tc,),
        in_specs=[kspec, kspec, kspec, kspec, kspec, vspec],
        out_specs=vspec,
        out_shape=jax.ShapeDtypeStruct((S, RWKV_ROW_GROUPS, SUBLANES, lanes), F32),
        scratch_shapes=[pltpu.VMEM((RWKV_ROWS, RWKV_KEY_TILES, SUBLANES, lanes), F32)],
        compiler_params=pltpu.CompilerParams(dimension_semantics=("arbitrary",),
                                             vmem_limit_bytes=V7X_VMEM_LIMIT_BYTES),
        name="rwkv7_scan",
    )(key_layout(r), key_layout(w), key_layout(k), key_layout(kk), key_layout(bvec), val_layout(v))
    y = y.reshape(S, RWKV_ROWS, 2, B, H)
    return jnp.transpose(y, (3, 0, 4, 2, 1)).reshape(B, S, H, N)


def _moe_kernel(be_ref, nused_ref, x_ref, wg_ref, wu_ref, wd_ref, wt_ref, o_ref):
    i = pl.program_id(0)

    @pl.when(i < nused_ref[0])
    def _():
        x = x_ref[...]
        g = jnp.dot(x, wg_ref[0], preferred_element_type=F32)
        u = jnp.dot(x, wu_ref[0], preferred_element_type=F32)
        hmid = (g * jax.nn.sigmoid(g) * u).astype(BF16)
        y = jnp.dot(hmid, wd_ref[0], preferred_element_type=F32)
        o_ref[...] = y * wt_ref[...]

    @pl.when(i >= nused_ref[0])
    def _():
        o_ref[...] = jnp.zeros_like(o_ref)


def _moe_grouped(xg, wt, blk_e, n_used, w_gate, w_up, w_down):
    R, D = xg.shape
    n_blk = R // MOE_BLOCK
    E, _, F = w_gate.shape

    def xmap(i, be, nu):
        return (jnp.minimum(i, jnp.maximum(nu[0] - 1, 0)), 0)

    grid_spec = pltpu.PrefetchScalarGridSpec(
        num_scalar_prefetch=2,
        grid=(n_blk,),
        in_specs=[pl.BlockSpec((MOE_BLOCK, D), xmap),
                  pl.BlockSpec((1, D, F), lambda i, be, nu: (be[i], 0, 0)),
                  pl.BlockSpec((1, D, F), lambda i, be, nu: (be[i], 0, 0)),
                  pl.BlockSpec((1, F, D), lambda i, be, nu: (be[i], 0, 0)),
                  pl.BlockSpec((MOE_BLOCK, 1), xmap)],
        out_specs=pl.BlockSpec((MOE_BLOCK, D), lambda i, be, nu: (i, 0)),
    )
    return pl.pallas_call(
        _moe_kernel,
        grid_spec=grid_spec,
        out_shape=jax.ShapeDtypeStruct((R, D), F32),
        compiler_params=pltpu.CompilerParams(dimension_semantics=("arbitrary",),
                                             vmem_limit_bytes=V7X_VMEM_LIMIT_BYTES),
        name="moe_grouped_swiglu",
    )(blk_e, n_used, xg, w_gate, w_up, w_down, wt)


def _split(z, widths):
    cuts = [int(i) for i in np.cumsum(widths)[:-1]]
    return jnp.split(z, cuts, axis=-1)


def _rms(z, g, eps=NORM_EPS):
    return z * lax.rsqrt(jnp.mean(z * z, axis=-1, keepdims=True) + eps) * g


def _heads(z, n_heads):
    B, S, C = z.shape
    return z.reshape(B, S, n_heads, C // n_heads).transpose(0, 2, 1, 3)


def _merge_heads(z):
    B, H, S, d = z.shape
    return z.transpose(0, 2, 1, 3).reshape(B, S, H * d)


def _rope(z, pos):
    half = ROPE_DIM // 2
    inv_freq = ROPE_THETA ** (-jnp.arange(half, dtype=F32) / half)
    ang = pos.astype(F32)[:, None] * inv_freq[None, :]
    cos = jnp.cos(ang)
    sin = jnp.sin(ang)
    z1, z2, zr = z[..., :half], z[..., half:ROPE_DIM], z[..., ROPE_DIM:]
    return jnp.concatenate([z1 * cos - z2 * sin, z2 * cos + z1 * sin, zr], axis=-1)


def _token_shift(z):
    return jnp.pad(z, ((0, 0), (1, 0), (0, 0)))[:, :-1]


def _masked_attention(q, k, v, mask, scale):
    s = jnp.einsum('bhqd,bhkd->bhqk', q, jnp.broadcast_to(k, q.shape[:2] + k.shape[2:])) * scale
    s = jnp.where(mask, s, NEG)
    p = jax.nn.softmax(s, axis=-1)
    return jnp.einsum('bhqk,bhkd->bhqd', p, jnp.broadcast_to(v, q.shape[:2] + v.shape[2:]))


def _rwkv7(p, v_first, v_mix, mu, w0, w_up, a0, a_up, g_up, k_k, k_a, r_k, ln_w, ln_b):
    B, S, _ = p.shape
    H, N = RWKV_HEADS, RWKV_HEAD_DIM
    xs = p + (_token_shift(p) - p) * mu
    r, k, v, xw, xa, xg = _split(xs, (RWKV_DIM, RWKV_DIM, RWKV_DIM, RWKV_DECAY_LORA, RWKV_ICLR_LORA, RWKV_GATE_LORA))
    w = -jax.nn.softplus(-(w0 + jnp.tanh(xw) @ w_up)) - 0.5
    decay = jnp.exp(-jnp.exp(w))
    a = jax.nn.sigmoid(a0 + xa @ a_up)
    g = jax.nn.sigmoid(xg) @ g_up
    kk = (k * k_k).reshape(B, S, H, N)
    kk = kk / jnp.maximum(jnp.sqrt(jnp.sum(kk * kk, axis=-1, keepdims=True)), 1e-12)
    k = k * (1.0 + (a - 1.0) * k_a)
    if v_mix is None:
        v_first = v
    else:
        v0, v_down, v_up = v_mix
        v = v + (v_first - v) * jax.nn.sigmoid(v0 + (v @ v_down) @ v_up)

    def hd(z):
        return z.reshape(B, S, H, N)

    y = _rwkv_scan(hd(r), hd(decay), hd(k), hd(v), kk, kk * hd(a))
    mean = jnp.mean(y, axis=-1, keepdims=True)
    var = jnp.mean(jnp.square(y - mean), axis=-1, keepdims=True)
    y = ((y - mean) * lax.rsqrt(var + RWKV_GN_EPS)).reshape(B, S, RWKV_DIM) * ln_w + ln_b
    bonus = jnp.sum((r * k * r_k).reshape(B, S, H, N), axis=-1, keepdims=True) * v.reshape(B, S, H, N)
    y = (y + bonus.reshape(B, S, RWKV_DIM)) * g
    return y, v_first


def _mlstm_chunkwise(q, k, v, i_pre, logf):
    B, H, S, dk = q.shape
    dv = v.shape[-1]
    L = MLSTM_CHUNK
    NC = S // L
    q = q.reshape(B, H, NC, L, dk)
    k = k.reshape(B, H, NC, L, dk)
    v = v.reshape(B, H, NC, L, dv)
    ic = i_pre.reshape(B, H, NC, L)
    bcum = jnp.cumsum(logf.reshape(B, H, NC, L), axis=-1)
    gtot = bcum[..., -1]
    a_log = gtot[..., None] - bcum + ic
    a_max = jnp.max(a_log, axis=-1)

    def step(carry, inp):
        C, n, m = carry
        g_c, a_c, amax_c, k_c, v_c = inp
        m_new = jnp.maximum(g_c + m, amax_c)
        wgt = jnp.exp(a_c - m_new[..., None])
        dec = jnp.exp(g_c + m - m_new)
        C_new = dec[..., None, None] * C + jnp.einsum('bhl,bhlk,bhlv->bhkv', wgt, k_c, v_c)
        n_new = dec[..., None] * n + jnp.einsum('bhl,bhlk->bhk', wgt, k_c)
        return (C_new, n_new, m_new), (C, n, m)

    init = (jnp.zeros((B, H, dk, dv), F32), jnp.zeros((B, H, dk), F32), jnp.zeros((B, H), F32))
    xs = (jnp.moveaxis(gtot, 2, 0), jnp.moveaxis(a_log, 2, 0), jnp.moveaxis(a_max, 2, 0),
          jnp.moveaxis(k, 2, 0), jnp.moveaxis(v, 2, 0))
    _, (C_prev, n_prev, m_prev) = lax.scan(step, init, xs)
    C_prev = jnp.moveaxis(C_prev, 0, 2)
    n_prev = jnp.moveaxis(n_prev, 0, 2)
    m_prev = jnp.moveaxis(m_prev, 0, 2)

    causal = jnp.tril(jnp.ones((L, L), dtype=bool))
    logD = jnp.where(causal, bcum[..., :, None] - bcum[..., None, :] + ic[..., None, :], -jnp.inf)
    m_inter = bcum + m_prev[..., None]
    m_t = jnp.maximum(m_inter, jnp.max(logD, axis=-1))
    Dm = jnp.exp(logD - m_t[..., None])
    sqk = jnp.einsum('bhctd,bhcsd->bhcts', q, k) * Dm
    inter = jnp.exp(m_inter - m_t)
    num = jnp.einsum('bhcts,bhcsv->bhctv', sqk, v) + inter[..., None] * jnp.einsum('bhctk,bhckv->bhctv', q, C_prev)
    den = jnp.sum(sqk, axis=-1) + inter * jnp.einsum('bhctk,bhck->bhct', q, n_prev)
    h = num / jnp.maximum(jnp.abs(den), jnp.exp(-m_t))[..., None]
    return h.reshape(B, H, S, dv)


def _mlstm(p, conv_w, conv_b, i_b, f_b, norm_g):
    H = MLSTM_HEADS
    qk, v, o, ig, fg = _split(p, (2 * MLSTM_QK_WIDTH, MLSTM_DIM, MLSTM_DIM, H, H))
    S = qk.shape[1]
    qkp = jnp.pad(qk, ((0, 0), (MLSTM_CONV - 1, 0), (0, 0)))
    conv = conv_b
    for j in range(MLSTM_CONV):
        conv = conv + qkp[:, j:j + S] * conv_w[j]
    qk = jax.nn.silu(conv)
    q, k = jnp.split(qk, 2, axis=-1)
    q = _heads(q, H) * (MLSTM_QK_DIM ** -0.5)
    k = _heads(k, H)
    v = _heads(v, H)
    i_pre = jnp.moveaxis(ig + i_b, -1, 1)
    logf = jax.nn.log_sigmoid(jnp.moveaxis(fg + f_b, -1, 1))
    h = _mlstm_chunkwise(q, k, v, i_pre, logf)
    h = _rms(h, norm_g.reshape(H, 1, MLSTM_V_DIM))
    return jax.nn.sigmoid(o) * _merge_heads(h)


def _moba(q, k, v):
    B, H, S, d = q.shape
    BLK = MOBA_BLOCK
    nb = S // BLK
    kb = k.reshape(B, H, nb, BLK, d)
    pos = jnp.arange(S)
    qblk = pos // BLK
    gate = jnp.einsum('bhsd,bhnd->bhsn', q, jnp.mean(kb, axis=3), precision=HIGHEST)
    gate = jnp.where(jnp.arange(nb)[None, :] < qblk[:, None], gate, NEG)
    n_top = min(MOBA_TOPK, nb)
    _, sel = lax.top_k(gate, n_top)
    slot_ok = jnp.arange(n_top)[None, :] < qblk[:, None]
    hit = (sel[..., None] == jnp.arange(nb)) & slot_ok[:, :, None]
    blk_mask = jnp.any(hit, axis=3)
    kblk = pos // BLK
    key_mask = jnp.take(blk_mask, kblk, axis=-1)
    own = (kblk[None, :] == qblk[:, None]) & (pos[None, :] <= pos[:, None])
    return _masked_attention(q, k, v, key_mask | own, d ** -0.5)


def _nsa(p, pos, q_norm, k_norm, cmp_pe, cmp_w1, cmp_b1, cmp_w2):
    B, S, _ = p.shape
    H, d = NSA_HEADS, ATTN_HEAD_DIM
    scale = d ** -0.5
    q, kc_in, vc_in, ks_in, vs_in, kw_in, vw_in, gl = _split(p, (NSA_DIM, d, d, d, d, d, d, 3 * H))
    q = _rope(_rms(_heads(q, H), q_norm), pos)

    ncmp = (S - NSA_CMP_BLOCK) // NSA_CMP_STRIDE + 1
    cstart = NSA_CMP_STRIDE * jnp.arange(ncmp)
    cend = cstart + NSA_CMP_BLOCK - 1
    cidx = cstart[:, None] + jnp.arange(NSA_CMP_BLOCK)[None, :]

    def compress(z, i):
        blocks = (z[:, cidx] + cmp_pe[i]).reshape(B, ncmp, NSA_CMP_BLOCK * d)
        return jax.nn.gelu(blocks @ cmp_w1[i] + cmp_b1[i]) @ cmp_w2[i]

    k_cmp = _rope(_rms(compress(kc_in, 0), k_norm[0]), cend)
    v_cmp = compress(vc_in, 1)
    s_c = jnp.einsum('bhsd,bnd->bhsn', q, k_cmp, precision=HIGHEST) * scale
    cvalid = cend[None, :] <= pos[:, None]
    p_cmp = jax.nn.softmax(jnp.where(cvalid, s_c, NEG), axis=-1)
    p_cmp = jnp.where(jnp.any(cvalid, axis=-1)[:, None], p_cmp, 0.0)
    o_cmp = jnp.einsum('bhsn,bnd->bhsd', p_cmp, v_cmp)

    SB = NSA_SLC_BLOCK
    nsb = S // SB
    sstart = SB * jnp.arange(nsb)
    overlap = ((cstart[:, None] <= sstart[None, :] + SB - 1) & (cend[:, None] >= sstart[None, :])).astype(F32)
    imp = jnp.einsum('bhsn,nj->bsj', p_cmp, overlap, precision=HIGHEST)
    cur = pos // SB
    blk = jnp.arange(nsb)[None, :]
    forced = (blk == 0) | (blk == cur[:, None]) | (blk == cur[:, None] - 1)
    imp = jnp.where(forced, BIG, jnp.where(blk <= cur[:, None], imp, NEG))
    n_sel = min(NSA_SLC_TOPK, nsb)
    _, sel = lax.top_k(imp, n_sel)
    blk_mask = jnp.any(sel[..., None] == jnp.arange(nsb), axis=2)
    causal = pos[None, :] <= pos[:, None]
    slc_mask = jnp.take(blk_mask, pos // SB, axis=-1) & causal
    k_slc = _rope(_rms(ks_in, k_norm[1]), pos)
    o_slc = _masked_attention(q, k_slc[:, None], vs_in[:, None], slc_mask[:, None], scale)

    k_win = _rope(_rms(kw_in, k_norm[2]), pos)
    win_mask = causal & (pos[None, :] > pos[:, None] - NSA_WINDOW)
    o_win = _masked_attention(q, k_win[:, None], vw_in[:, None], win_mask[None, None], scale)

    gates = jax.nn.sigmoid(gl).reshape(B, S, H, 3).transpose(0, 2, 1, 3)
    o = gates[..., 0:1] * o_cmp + gates[..., 1:2] * o_slc + gates[..., 2:3] * o_win
    return _merge_heads(o)


def _moe(h, router, bias, w_gate, w_up, w_down, s_gate, s_up, s_down):
    B, S, D = h.shape
    T = B * S
    A = T * TOP_K
    xt = h.reshape(T, D)
    scores = jax.nn.sigmoid(jnp.dot(xt, router, precision=HIGHEST))
    biased = scores + bias
    grp_score = lax.top_k(biased.reshape(T, N_GROUPS, N_EXPERTS // N_GROUPS), 2)[0].sum(-1)
    _, gidx = lax.top_k(grp_score, TOPK_GROUPS)
    gmask = jnp.any(gidx[:, :, None] == jnp.arange(N_GROUPS)[None, None, :], axis=1)
    emask = jnp.repeat(gmask, N_EXPERTS // N_GROUPS, axis=1)
    _, eidx = lax.top_k(jnp.where(emask, biased, NEG), TOP_K)
    wts = jnp.take_along_axis(scores, eidx, axis=1)
    wts = wts / jnp.sum(wts, axis=-1, keepdims=True) * ROUTED_SCALE

    e_flat = eidx.reshape(A)
    order = jnp.argsort(e_flat)
    e_s = e_flat[order]
    tok_s = order // TOP_K
    w_s = wts.reshape(A)[order]
    counts = jnp.bincount(e_flat, length=N_EXPERTS)
    padded = (counts + MOE_BLOCK - 1) // MOE_BLOCK * MOE_BLOCK
    start = jnp.cumsum(counts) - counts
    pstart = jnp.cumsum(padded) - padded
    dest = (pstart[e_s] + jnp.arange(A) - start[e_s]).astype(jnp.int32)
    n_blk = -(-(A + N_EXPERTS * (MOE_BLOCK - 1)) // MOE_BLOCK)
    R = n_blk * MOE_BLOCK
    tok_buf = jnp.zeros((R,), jnp.int32).at[dest].set(tok_s.astype(jnp.int32))
    w_buf = jnp.zeros((R,), F32).at[dest].set(w_s)
    blk_e = jnp.minimum(jnp.sum((pstart + padded)[None, :] <= (jnp.arange(n_blk) * MOE_BLOCK)[:, None], axis=1),
                        N_EXPERTS - 1).astype(jnp.int32)
    n_used = (jnp.sum(padded) // MOE_BLOCK).astype(jnp.int32).reshape(1)

    xb = xt.astype(BF16)
    yg = _moe_grouped(xb[tok_buf], w_buf[:, None], blk_e, n_used,
                      w_gate.astype(BF16), w_up.astype(BF16), w_down.astype(BF16))
    slot = jnp.zeros((A,), jnp.int32).at[order].set(dest)
    routed = jnp.sum(yg[slot].reshape(T, TOP_K, D), axis=1)

    gu = _matmul(xb, jnp.concatenate([s_gate, s_up], axis=1).astype(BF16))
    F = s_gate.shape[1]
    hmid = (jax.nn.silu(gu[:, :F]) * gu[:, F:]).astype(BF16)
    shared = _matmul(hmid, s_down.astype(BF16))
    return (routed + shared).reshape(B, S, D)


def kernel(x, c, ada_w, ada_b, norm_mix_g, norm_ffn_g, w_in, rwkv_mu, rwkv_w0, rwkv_w_up, rwkv_a0, rwkv_a_up, rwkv_g_up, rwkv_k_k, rwkv_k_a, rwkv_r_k, rwkv_ln_w, rwkv_ln_b, rwkv_v0, rwkv_v_down, rwkv_v_up, mlstm_conv_w, mlstm_conv_b, mlstm_i_b, mlstm_f_b, mlstm_norm_g, moba_q_norm, moba_k_norm, nsa_q_norm, nsa_k_norm, nsa_cmp_pe, nsa_cmp_w1, nsa_cmp_b1, nsa_cmp_w2, w_branch, w_out, moe_router, moe_bias, moe_w_gate, moe_w_up, moe_w_down, shared_w_gate, shared_w_up, shared_w_down):
    B, S, D = x.shape
    T = B * S
    depth = ada_w.shape[0]
    pos = jnp.arange(S)
    cond = jax.nn.silu(c)
    v_first = None
    for l in range(depth):
        mod = jnp.dot(cond, ada_w[l], precision=HIGHEST) + ada_b[l]
        sh_mix, sc_mix, gate_mix, sh_ffn, sc_ffn, gate_ffn = [z[:, None, :] for z in jnp.split(mod, 6, axis=-1)]

        h = _rms(x, norm_mix_g[l]) * (1.0 + sc_mix) + sh_mix
        proj = _matmul(h.reshape(T, D).astype(BF16), w_in[l].astype(BF16)).reshape(B, S, -1)
        p_a, p_b, p_c, p_d, p_g = _split(proj, IN_WIDTHS)
        v_mix = None if l == 0 else (rwkv_v0[l - 1], rwkv_v_down[l - 1], rwkv_v_up[l - 1])
        y_a, v_first = _rwkv7(p_a, v_first, v_mix, rwkv_mu[l], rwkv_w0[l], rwkv_w_up[l], rwkv_a0[l], rwkv_a_up[l],
                              rwkv_g_up[l], rwkv_k_k[l], rwkv_k_a[l], rwkv_r_k[l], rwkv_ln_w[l], rwkv_ln_b[l])
        y_b = _mlstm(p_b, mlstm_conv_w[l], mlstm_conv_b[l], mlstm_i_b[l], mlstm_f_b[l], mlstm_norm_g[l])
        q_c, k_c, v_c = _split(p_c, (MOBA_DIM, MOBA_DIM, MOBA_DIM))
        q_c = _rope(_rms(_heads(q_c, MOBA_HEADS), moba_q_norm[l]), pos)
        k_c = _rope(_rms(_heads(k_c, MOBA_HEADS), moba_k_norm[l]), pos)
        y_c = _merge_heads(_moba(q_c, k_c, _heads(v_c, MOBA_HEADS)))
        y_d = _nsa(p_d, pos, nsa_q_norm[l], nsa_k_norm[l], nsa_cmp_pe[l], nsa_cmp_w1[l], nsa_cmp_b1[l], nsa_cmp_w2[l])
        gates = jax.nn.sigmoid(p_g.reshape(B, S, N_MIXERS, D))
        merged = None
        for m, y_m in enumerate((y_a, y_b, y_c, y_d)):
            term = gates[:, :, m] * _matmul(y_m.reshape(T, -1).astype(BF16), w_branch[l, m].astype(BF16)).reshape(B, S, D)
            merged = term if merged is None else merged + term
        x = x + gate_mix * _matmul(merged.reshape(T, D).astype(BF16), w_out[l].astype(BF16)).reshape(B, S, D)

        h = _rms(x, norm_ffn_g[l]) * (1.0 + sc_ffn) + sh_ffn
        x = x + gate_ffn * _moe(h, moe_router[l], moe_bias[l], moe_w_gate[l], moe_w_up[l], moe_w_down[l],
                                shared_w_gate[l], shared_w_up[l], shared_w_down[l])
    return x
```

```python
import functools

import numpy as np
import jax
import jax.numpy as jnp
from jax import lax
from jax.experimental import pallas as pl
from jax.experimental.pallas import tpu as pltpu

F32 = jnp.float32
BF16 = jnp.bfloat16
HIGHEST = lax.Precision.HIGHEST

D_MODEL = 2048
N_MIXERS = 4
BRANCH_DIM = 512
NORM_EPS = 1e-6
NEG = -1e30
BIG = 1e9

RWKV_HEADS = 8
RWKV_HEAD_DIM = 64
RWKV_DIM = RWKV_HEADS * RWKV_HEAD_DIM
RWKV_DECAY_LORA = 96
RWKV_ICLR_LORA = 96
RWKV_GATE_LORA = 256
RWKV_GN_EPS = 64e-5
RWKV_IN_DIM = 3 * RWKV_DIM + RWKV_DECAY_LORA + RWKV_ICLR_LORA + RWKV_GATE_LORA

MLSTM_HEADS = 4
MLSTM_QK_DIM = 128
MLSTM_V_DIM = 128
MLSTM_QK_WIDTH = MLSTM_HEADS * MLSTM_QK_DIM
MLSTM_DIM = MLSTM_HEADS * MLSTM_V_DIM
MLSTM_CONV = 4
MLSTM_CHUNK = 64
MLSTM_IN_DIM = 2 * MLSTM_QK_WIDTH + 2 * MLSTM_DIM + 2 * MLSTM_HEADS

ATTN_HEAD_DIM = 128
ROPE_DIM = ATTN_HEAD_DIM // 4
ROPE_THETA = 500000.0

MOBA_HEADS = 4
MOBA_DIM = MOBA_HEADS * ATTN_HEAD_DIM
MOBA_BLOCK = 256
MOBA_TOPK = 3
MOBA_IN_DIM = 3 * MOBA_DIM

NSA_HEADS = 4
NSA_DIM = NSA_HEADS * ATTN_HEAD_DIM
NSA_CMP_BLOCK = 32
NSA_CMP_STRIDE = 16
NSA_SLC_BLOCK = 64
NSA_SLC_TOPK = 16
NSA_WINDOW = 512
NSA_IN_DIM = NSA_DIM + 6 * ATTN_HEAD_DIM + 3 * NSA_HEADS

MERGE_IN_DIM = N_MIXERS * D_MODEL
IN_WIDTHS = (RWKV_IN_DIM, MLSTM_IN_DIM, MOBA_IN_DIM, NSA_IN_DIM, MERGE_IN_DIM)

N_EXPERTS = 64
TOP_K = 8
N_GROUPS = 8
TOPK_GROUPS = 4
EXPERT_DIM = 512
ROUTED_SCALE = 2.5
MOE_BLOCK = 256

V7X_VMEM_LIMIT_BYTES = 48 * 1024 * 1024
LANES = 128
SUBLANES = 8


def _mm_kernel(a_ref, b_ref, o_ref):
    o_ref[...] = jnp.dot(a_ref[...], b_ref[...], preferred_element_type=F32).astype(o_ref.dtype)


def _pick_tile(n, pref):
    t = min(pref, n)
    while n % t:
        t //= 2
    return t


def _pick_lane_tile(n, pref):
    assert n % LANES == 0
    units = n // LANES
    best = 1
    for u in range(1, units + 1):
        if units % u == 0 and u * LANES <= pref:
            best = u
    return best * LANES


def _matmul(a, b, out_dtype=F32, tm=1024, tn=1024):
    M, K = a.shape
    _, N = b.shape
    tm = _pick_tile(M, tm)
    tn = _pick_lane_tile(N, tn)
    return pl.pallas_call(
        _mm_kernel,
        grid=(N // tn, M // tm),
        in_specs=[pl.BlockSpec((tm, K), lambda j, i: (i, 0)),
                  pl.BlockSpec((K, tn), lambda j, i: (0, j))],
        out_specs=pl.BlockSpec((tm, tn), lambda j, i: (i, j)),
        out_shape=jax.ShapeDtypeStruct((M, N), out_dtype),
        compiler_params=pltpu.CompilerParams(dimension_semantics=("parallel", "parallel"),
                                             vmem_limit_bytes=V7X_VMEM_LIMIT_BYTES),
        name="dense_matmul",
    )(a, b)


def _mm_res_kernel(*refs, has_extra):
    if has_extra:
        a_ref, b_ref, res_ref, gate_ref, extra_ref, o_ref = refs
    else:
        a_ref, b_ref, res_ref, gate_ref, o_ref = refs
    y = jnp.dot(a_ref[...], b_ref[...], preferred_element_type=F32)
    if has_extra:
        y = y + extra_ref[...]
    o_ref[...] = res_ref[...] + gate_ref[0] * y


def _matmul_residual(a, b, res, gate, seq_len, extra=None, tm=1024, tn=1024):
    M, K = a.shape
    _, N = b.shape
    tm = _pick_tile(seq_len, tm)
    tn = _pick_lane_tile(N, tn)
    row_tile = pl.BlockSpec((tm, tn), lambda j, i: (i, j))
    in_specs = [pl.BlockSpec((tm, K), lambda j, i: (i, 0)),
                pl.BlockSpec((K, tn), lambda j, i: (0, j)),
                row_tile,
                pl.BlockSpec((1, 1, tn), lambda j, i: ((i * tm) // seq_len, 0, j))]
    args = [a, b, res, gate]
    if extra is not None:
        in_specs.append(row_tile)
        args.append(extra)
    return pl.pallas_call(
        functools.partial(_mm_res_kernel, has_extra=extra is not None),
        grid=(N // tn, M // tm),
        in_specs=in_specs,
        out_specs=row_tile,
        out_shape=jax.ShapeDtypeStruct((M, N), F32),
        compiler_params=pltpu.CompilerParams(dimension_semantics=("parallel", "parallel"),
                                             vmem_limit_bytes=V7X_VMEM_LIMIT_BYTES),
        name="matmul_gated_residual",
    )(*args)


def _normmod_kernel(*refs, with_router):
    if with_router:
        x_ref, g_ref, sc_ref, sh_ref, rt_ref, o_ref, lg_ref = refs
    else:
        x_ref, g_ref, sc_ref, sh_ref, o_ref = refs
    x = x_ref[...]
    y = x * lax.rsqrt(jnp.mean(x * x, axis=-1, keepdims=True) + NORM_EPS) * g_ref[...]
    h = y * (1.0 + sc_ref[0]) + sh_ref[0]
    o_ref[...] = h.astype(o_ref.dtype)
    if with_router:
        lg_ref[...] = jnp.dot(h, rt_ref[...], preferred_element_type=F32, precision=HIGHEST)


def _normmod(x2d, g, scale, shift, seq_len, router=None, tm=512):
    M, D = x2d.shape
    tm = _pick_tile(seq_len, tm)
    mod_spec = pl.BlockSpec((1, 1, D), lambda i: ((i * tm) // seq_len, 0, 0))
    row_spec = pl.BlockSpec((tm, D), lambda i: (i, 0))
    in_specs = [row_spec, pl.BlockSpec((1, D), lambda i: (0, 0)), mod_spec, mod_spec]
    args = [x2d, g.reshape(1, D), scale, shift]
    out_shape = [jax.ShapeDtypeStruct((M, D), BF16)]
    out_specs = [row_spec]
    if router is not None:
        n_exp = router.shape[1]
        E = -(-n_exp // LANES) * LANES
        in_specs.append(pl.BlockSpec((D, E), lambda i: (0, 0)))
        args.append(jnp.pad(router, ((0, 0), (0, E - n_exp))))
        out_shape.append(jax.ShapeDtypeStruct((M, E), F32))
        out_specs.append(pl.BlockSpec((tm, E), lambda i: (i, 0)))
    out = pl.pallas_call(
        functools.partial(_normmod_kernel, with_router=router is not None),
        grid=(M // tm,),
        in_specs=in_specs,
        out_specs=out_specs,
        out_shape=out_shape,
        compiler_params=pltpu.CompilerParams(dimension_semantics=("parallel",),
                                             vmem_limit_bytes=V7X_VMEM_LIMIT_BYTES),
        name="rmsnorm_modulate",
    )(*args)
    return (out[0], out[1][:, :router.shape[1]]) if router is not None else out[0]


def _merge_kernel(g0, g1, g2, g3, y0, y1, y2, y3, wb_ref, o_ref):
    acc = None
    for m, (g_ref, y_ref) in enumerate(((g0, y0), (g1, y1), (g2, y2), (g3, y3))):
        t = jnp.dot(y_ref[...], wb_ref[m], preferred_element_type=F32) * jax.nn.sigmoid(g_ref[...].astype(F32))
        acc = t if acc is None else acc + t
    o_ref[...] = acc.astype(o_ref.dtype)


def _merge_branches(p_gate, ys, w_branch, tm=512, tn=512):
    M = p_gate.shape[0]
    n_mix, kb, D = w_branch.shape
    tm = _pick_tile(M, tm)
    tn = _pick_lane_tile(D, tn)
    nj = D // tn
    gate_specs = [pl.BlockSpec((tm, tn), functools.partial(lambda j, i, m: (i, m * nj + j), m=m)) for m in range(n_mix)]
    y_specs = [pl.BlockSpec((tm, kb), lambda j, i: (i, 0)) for _ in range(n_mix)]
    return pl.pallas_call(
        _merge_kernel,
        grid=(nj, M // tm),
        in_specs=gate_specs + y_specs + [pl.BlockSpec((n_mix, kb, tn), lambda j, i: (0, 0, j))],
        out_specs=pl.BlockSpec((tm, tn), lambda j, i: (i, j)),
        out_shape=jax.ShapeDtypeStruct((M, D), BF16),
        compiler_params=pltpu.CompilerParams(dimension_semantics=("parallel", "parallel"),
                                             vmem_limit_bytes=V7X_VMEM_LIMIT_BYTES),
        name="merge_gated_branches",
    )(*([p_gate] * n_mix), *ys, w_branch)


ATTN_KEY_CHUNK = 512
NSA_Q_TILE = 128
MASKED_SCORE = -1e30
RUNNING_MAX_FLOOR = -1e20


def _attn_kernel(*refs, tqt, n_rep, kc, window, use_bm, scale):
    if use_bm:
        q_ref, k_ref, v_ref, bm_ref, e_ref, o_ref = refs
    else:
        q_ref, k_ref, v_ref, o_ref = refs
    t0 = pl.program_id(2) * tqt
    d = q_ref.shape[-1]
    rows = n_rep * tqt
    q = q_ref[0].reshape(rows, d)
    dmat = lax.broadcasted_iota(jnp.int32, (tqt, kc), 1) - lax.broadcasted_iota(jnp.int32, (tqt, kc), 0)

    def body(c, carry):
        m, l, acc = carry
        k0 = pl.multiple_of(c * kc, kc)
        kch = k_ref[0, 0, pl.ds(k0, kc), :]
        vch = v_ref[0, 0, pl.ds(k0, kc), :]
        s = lax.dot_general(q, kch, (((1,), (1,)), ((), ())), preferred_element_type=F32) * scale
        off = t0 - k0
        ok = dmat <= off
        if window is not None:
            ok = ok & (dmat > off - window)
        if use_bm:
            ok = ok & (jnp.dot(bm_ref[0, 0], e_ref[c], preferred_element_type=F32) > 0.5)
        bias = jnp.where(ok, 0.0, MASKED_SCORE)
        s = (s.reshape(n_rep, tqt, kc) + bias[None]).reshape(rows, kc)
        m_new = jnp.maximum(m, jnp.max(s, axis=-1, keepdims=True))
        alpha = jnp.exp(m - m_new)
        p = jnp.exp(s - m_new)
        l = alpha * l + jnp.sum(p, axis=-1, keepdims=True)
        acc = alpha * acc + jnp.dot(p.astype(BF16), vch, preferred_element_type=F32)
        return m_new, l, acc

    lo = 0 if window is None else jnp.maximum(t0 - window + 1, 0) // kc
    hi = (t0 + tqt + kc - 1) // kc
    init = (jnp.full((rows, 1), RUNNING_MAX_FLOOR, F32), jnp.zeros((rows, 1), F32), jnp.zeros((rows, d), F32))
    _, l, acc = lax.fori_loop(lo, hi, body, init)
    o_ref[0] = (acc / l).reshape(n_rep, tqt, d).astype(o_ref.dtype)


def _block_expander(seq_len, kc, block, n_lanes=LANES):
    key_blk = (np.arange(seq_len) // block).reshape(seq_len // kc, 1, kc)
    return jnp.asarray(key_blk == np.arange(n_lanes).reshape(1, n_lanes, 1), dtype=BF16)


def _attention(q, k, v, tqt, scale, block_mask=None, block=None, window=None):
    B, HQ, S, d = q.shape
    G = k.shape[1]
    n_rep = HQ // G
    kc = _pick_tile(S, ATTN_KEY_CHUNK)
    tqt = _pick_tile(S, tqt)
    use_bm = block_mask is not None
    qspec = pl.BlockSpec((1, n_rep, tqt, d), lambda b, g, i: (b, g, i, 0))
    kvspec = pl.BlockSpec((1, 1, S, d), lambda b, g, i: (b, g, 0, 0))
    in_specs = [qspec, kvspec, kvspec]
    args = [q, k, v]
    if use_bm:
        in_specs += [pl.BlockSpec((1, 1, tqt, LANES), lambda b, g, i: (b, g, i, 0)),
                     pl.BlockSpec((S // kc, LANES, kc), lambda b, g, i: (0, 0, 0))]
        args += [block_mask, _block_expander(S, kc, block)]
    return pl.pallas_call(
        functools.partial(_attn_kernel, tqt=tqt, n_rep=n_rep, kc=kc, window=window, use_bm=use_bm, scale=scale),
        grid=(B, G, S // tqt),
        in_specs=in_specs,
        out_specs=qspec,
        out_shape=jax.ShapeDtypeStruct((B, HQ, S, d), F32),
        compiler_params=pltpu.CompilerParams(dimension_semantics=("parallel", "parallel", "parallel"),
                                             vmem_limit_bytes=V7X_VMEM_LIMIT_BYTES),
        name="block_masked_attention",
    )(*args)


def _nsa_cmp_kernel(q_ref, kc_ref, vc_ref, ov_ref, o_ref, imp_ref, *, tqt, n_rep, ncmp, scale):
    t0 = pl.program_id(1) * tqt
    d = q_ref.shape[-1]
    ncp = kc_ref.shape[1]
    rows = n_rep * tqt
    q = q_ref[0].reshape(rows, d)
    s = lax.dot_general(q, kc_ref[0], (((1,), (1,)), ((), ())), preferred_element_type=F32) * scale
    n_io = lax.broadcasted_iota(jnp.int32, (tqt, ncp), 1)
    t_io = t0 + lax.broadcasted_iota(jnp.int32, (tqt, ncp), 0)
    valid = (n_io * NSA_CMP_STRIDE + (NSA_CMP_BLOCK - 1) <= t_io) & (n_io < ncmp)
    bias = jnp.where(valid, 0.0, MASKED_SCORE)
    s = s.reshape(n_rep, tqt, ncp) + bias[None]
    m = jnp.maximum(jnp.max(s, axis=-1, keepdims=True), RUNNING_MAX_FLOOR)
    e = jnp.exp(s - m)
    l = jnp.sum(e, axis=-1, keepdims=True)
    p = e / jnp.where(l > 0.0, l, 1.0)
    o = jnp.dot(p.reshape(rows, ncp).astype(BF16), vc_ref[0], preferred_element_type=F32)
    o_ref[0] = o.reshape(n_rep, tqt, d)
    imp_ref[0] = jnp.dot(jnp.sum(p, axis=0), ov_ref[...], preferred_element_type=F32, precision=HIGHEST)


def _nsa_cmp_attention(q, k_cmp, v_cmp, overlap, ncmp, scale, tqt=128):
    B, H, S, d = q.shape
    ncp = k_cmp.shape[1]
    tqt = _pick_tile(S, tqt)
    qspec = pl.BlockSpec((1, H, tqt, d), lambda b, i: (b, 0, i, 0))
    cspec = pl.BlockSpec((1, ncp, d), lambda b, i: (b, 0, 0))
    return pl.pallas_call(
        functools.partial(_nsa_cmp_kernel, tqt=tqt, n_rep=H, ncmp=ncmp, scale=scale),
        grid=(B, S // tqt),
        in_specs=[qspec, cspec, cspec, pl.BlockSpec((ncp, LANES), lambda b, i: (0, 0))],
        out_specs=[qspec, pl.BlockSpec((1, tqt, LANES), lambda b, i: (b, i, 0))],
        out_shape=[jax.ShapeDtypeStruct((B, H, S, d), F32), jax.ShapeDtypeStruct((B, S, LANES), F32)],
        compiler_params=pltpu.CompilerParams(dimension_semantics=("parallel", "parallel"),
                                             vmem_limit_bytes=V7X_VMEM_LIMIT_BYTES),
        name="nsa_compressed_attention",
    )(q, k_cmp, v_cmp, overlap)


RWKV_TIME_CHUNK = 32
RWKV_ROWS = RWKV_HEAD_DIM // 2
RWKV_ROW_GROUPS = RWKV_ROWS // SUBLANES
RWKV_KEY_TILES = RWKV_HEAD_DIM // SUBLANES


def _rwkv_scan_kernel(r_ref, w_ref, k_ref, kk_ref, b_ref, v_ref, y_ref, st_ref):
    @pl.when(pl.program_id(0) == 0)
    def _():
        st_ref[...] = jnp.zeros_like(st_ref)

    n_t = r_ref.shape[0]
    sub_iota = lax.broadcasted_iota(jnp.int32, (SUBLANES, st_ref.shape[-1]), 0)

    def step(t, carry):
        r = r_ref[t]
        w = w_ref[t]
        k = k_ref[t]
        kk = kk_ref[t]
        bb = b_ref[t]

        def group(g, c2):
            y_tile = jnp.zeros(sub_iota.shape, F32)
            for ii in range(SUBLANES):
                i = g * SUBLANES + ii
                s = st_ref[i]
                sa = jnp.sum(jnp.sum(s * kk, axis=0), axis=0, keepdims=True)
                vrow = v_ref[t, g, pl.ds(ii, 1), :]
                s = s * w - sa[None] * bb + vrow[None] * k
                st_ref[i] = s
                yrow = jnp.sum(jnp.sum(s * r, axis=0), axis=0, keepdims=True)
                y_tile = jnp.where(sub_iota == ii, jnp.broadcast_to(yrow, sub_iota.shape), y_tile)
            y_ref[t, g] = y_tile
            return c2

        lax.fori_loop(0, RWKV_ROW_GROUPS, group, 0)
        return carry

    lax.fori_loop(0, n_t, step, 0)


def _rwkv_scan(r, w, k, v, kk, bvec):
    B, S, H, N = r.shape
    chains = B * H
    lanes = 2 * chains

    def key_layout(z):
        zt = jnp.transpose(z, (1, 3, 0, 2)).reshape(S, N, chains)
        zt = jnp.concatenate([zt, zt], axis=-1)
        return zt.reshape(S, RWKV_KEY_TILES, SUBLANES, lanes)

    def val_layout(z):
        zt = jnp.transpose(z.reshape(B, S, H, 2, RWKV_ROWS), (1, 4, 3, 0, 2))
        return zt.reshape(S, RWKV_ROW_GROUPS, SUBLANES, lanes)

    tc = _pick_tile(S, RWKV_TIME_CHUNK)
    kspec = pl.BlockSpec((tc, RWKV_KEY_TILES, SUBLANES, lanes), lambda c: (c, 0, 0, 0))
    vspec = pl.BlockSpec((tc, RWKV_ROW_GROUPS, SUBLANES, lanes), lambda c: (c, 0, 0, 0))
    y = pl.pallas_call(
        _rwkv_scan_kernel,
        grid=(S // tc,),
        in_specs=[kspec, kspec, kspec, kspec, kspec, vspec],
        out_specs=vspec,
        out_shape=jax.ShapeDtypeStruct((S, RWKV_ROW_GROUPS, SUBLANES, lanes), F32),
        scratch_shapes=[pltpu.VMEM((RWKV_ROWS, RWKV_KEY_TILES, SUBLANES, lanes), F32)],
        compiler_params=pltpu.CompilerParams(dimension_semantics=("arbitrary",),
                                             vmem_limit_bytes=V7X_VMEM_LIMIT_BYTES),
        name="rwkv7_scan",
    )(key_layout(r), key_layout(w), key_layout(k), key_layout(kk), key_layout(bvec), val_layout(v))
    y = y.reshape(S, RWKV_ROWS, 2, B, H)
    return jnp.transpose(y, (3, 0, 4, 2, 1)).reshape(B, S, H, N)


def _moe_kernel(be_ref, nused_ref, x_ref, wg_ref, wu_ref, wd_ref, wt_ref, o_ref, wg_s, wu_s, wd_s):
    i = pl.program_id(0)
    active = i < nused_ref[0]

    @pl.when(active & ((i == 0) | (be_ref[i] != be_ref[jnp.maximum(i - 1, 0)])))
    def _():
        wg_s[...] = wg_ref[0].astype(BF16)
        wu_s[...] = wu_ref[0].astype(BF16)
        wd_s[...] = wd_ref[0].astype(BF16)

    @pl.when(active)
    def _():
        x = x_ref[...]
        g = jnp.dot(x, wg_s[...], preferred_element_type=F32)
        u = jnp.dot(x, wu_s[...], preferred_element_type=F32)
        hmid = (g * jax.nn.sigmoid(g) * u).astype(BF16)
        y = jnp.dot(hmid, wd_s[...], preferred_element_type=F32)
        o_ref[...] = (y * wt_ref[...]).astype(o_ref.dtype)

    @pl.when(jnp.logical_not(active))
    def _():
        o_ref[...] = jnp.zeros_like(o_ref)


def _moe_grouped(xg, wt, blk_e, n_used, w_gate, w_up, w_down):
    R, D = xg.shape
    n_blk = R // MOE_BLOCK
    E, _, F = w_gate.shape

    def xmap(i, be, nu):
        return (jnp.minimum(i, jnp.maximum(nu[0] - 1, 0)), 0)

    grid_spec = pltpu.PrefetchScalarGridSpec(
        num_scalar_prefetch=2,
        grid=(n_blk,),
        in_specs=[pl.BlockSpec((MOE_BLOCK, D), xmap),
                  pl.BlockSpec((1, D, F), lambda i, be, nu: (be[i], 0, 0)),
                  pl.BlockSpec((1, D, F), lambda i, be, nu: (be[i], 0, 0)),
                  pl.BlockSpec((1, F, D), lambda i, be, nu: (be[i], 0, 0)),
                  pl.BlockSpec((MOE_BLOCK, 1), xmap)],
        out_specs=pl.BlockSpec((MOE_BLOCK, D), lambda i, be, nu: (i, 0)),
        scratch_shapes=[pltpu.VMEM((D, F), BF16), pltpu.VMEM((D, F), BF16), pltpu.VMEM((F, D), BF16)],
    )
    return pl.pallas_call(
        _moe_kernel,
        grid_spec=grid_spec,
        out_shape=jax.ShapeDtypeStruct((R, D), BF16),
        compiler_params=pltpu.CompilerParams(dimension_semantics=("arbitrary",),
                                             vmem_limit_bytes=V7X_VMEM_LIMIT_BYTES),
        name="moe_grouped_swiglu",
    )(blk_e, n_used, xg, w_gate, w_up, w_down, wt)


def _split(z, widths):
    cuts = [int(i) for i in np.cumsum(widths)[:-1]]
    return jnp.split(z, cuts, axis=-1)


def _rms(z, g, eps=NORM_EPS):
    return z * lax.rsqrt(jnp.mean(z * z, axis=-1, keepdims=True) + eps) * g


def _heads(z, n_heads):
    B, S, C = z.shape
    return z.reshape(B, S, n_heads, C // n_heads).transpose(0, 2, 1, 3)


def _merge_heads(z):
    B, H, S, d = z.shape
    return z.transpose(0, 2, 1, 3).reshape(B, S, H * d)


def _rope(z, pos):
    half = ROPE_DIM // 2
    inv_freq = ROPE_THETA ** (-jnp.arange(half, dtype=F32) / half)
    ang = pos.astype(F32)[:, None] * inv_freq[None, :]
    cos = jnp.cos(ang)
    sin = jnp.sin(ang)
    z1, z2, zr = z[..., :half], z[..., half:ROPE_DIM], z[..., ROPE_DIM:]
    return jnp.concatenate([z1 * cos - z2 * sin, z2 * cos + z1 * sin, zr], axis=-1)


def _token_shift(z):
    return jnp.pad(z, ((0, 0), (1, 0), (0, 0)))[:, :-1]


def _rwkv7(p, v_first, v_mix, mu, w0, w_up, a0, a_up, g_up, k_k, k_a, r_k, ln_w, ln_b):
    B, S, _ = p.shape
    H, N = RWKV_HEADS, RWKV_HEAD_DIM
    xs = p + (_token_shift(p) - p) * mu
    r, k, v, xw, xa, xg = _split(xs, (RWKV_DIM, RWKV_DIM, RWKV_DIM, RWKV_DECAY_LORA, RWKV_ICLR_LORA, RWKV_GATE_LORA))
    w = -jax.nn.softplus(-(w0 + jnp.tanh(xw) @ w_up)) - 0.5
    decay = jnp.exp(-jnp.exp(w))
    a = jax.nn.sigmoid(a0 + xa @ a_up)
    g = jax.nn.sigmoid(xg) @ g_up
    kk = (k * k_k).reshape(B, S, H, N)
    kk = kk / jnp.maximum(jnp.sqrt(jnp.sum(kk * kk, axis=-1, keepdims=True)), 1e-12)
    k = k * (1.0 + (a - 1.0) * k_a)
    if v_mix is None:
        v_first = v
    else:
        v0, v_down, v_up = v_mix
        v = v + (v_first - v) * jax.nn.sigmoid(v0 + (v @ v_down) @ v_up)

    def hd(z):
        return z.reshape(B, S, H, N)

    y = _rwkv_scan(hd(r), hd(decay), hd(k), hd(v), kk, kk * hd(a))
    mean = jnp.mean(y, axis=-1, keepdims=True)
    var = jnp.mean(jnp.square(y - mean), axis=-1, keepdims=True)
    y = ((y - mean) * lax.rsqrt(var + RWKV_GN_EPS)).reshape(B, S, RWKV_DIM) * ln_w + ln_b
    bonus = jnp.sum((r * k * r_k).reshape(B, S, H, N), axis=-1, keepdims=True) * v.reshape(B, S, H, N)
    y = (y + bonus.reshape(B, S, RWKV_DIM)) * g
    return y, v_first


def _mlstm_chunkwise(q, k, v, i_pre, logf):
    B, H, S, dk = q.shape
    dv = v.shape[-1]
    L = MLSTM_CHUNK
    NC = S // L
    q = q.reshape(B, H, NC, L, dk)
    k = k.reshape(B, H, NC, L, dk)
    v = v.reshape(B, H, NC, L, dv)
    ic = i_pre.reshape(B, H, NC, L)
    bcum = jnp.cumsum(logf.reshape(B, H, NC, L), axis=-1)
    gtot = bcum[..., -1]
    a_log = gtot[..., None] - bcum + ic
    a_max = jnp.max(a_log, axis=-1)

    def step(carry, inp):
        C, n, m = carry
        g_c, a_c, amax_c, k_c, v_c = inp
        m_new = jnp.maximum(g_c + m, amax_c)
        wgt = jnp.exp(a_c - m_new[..., None])
        dec = jnp.exp(g_c + m - m_new)
        C_new = dec[..., None, None] * C + jnp.einsum('bhl,bhlk,bhlv->bhkv', wgt, k_c, v_c)
        n_new = dec[..., None] * n + jnp.einsum('bhl,bhlk->bhk', wgt, k_c)
        return (C_new, n_new, m_new), (C, n, m)

    init = (jnp.zeros((B, H, dk, dv), F32), jnp.zeros((B, H, dk), F32), jnp.zeros((B, H), F32))
    xs = (jnp.moveaxis(gtot, 2, 0), jnp.moveaxis(a_log, 2, 0), jnp.moveaxis(a_max, 2, 0),
          jnp.moveaxis(k, 2, 0), jnp.moveaxis(v, 2, 0))
    _, (C_prev, n_prev, m_prev) = lax.scan(step, init, xs)
    C_prev = jnp.moveaxis(C_prev, 0, 2)
    n_prev = jnp.moveaxis(n_prev, 0, 2)
    m_prev = jnp.moveaxis(m_prev, 0, 2)

    causal = jnp.tril(jnp.ones((L, L), dtype=bool))
    logD = jnp.where(causal, bcum[..., :, None] - bcum[..., None, :] + ic[..., None, :], -jnp.inf)
    m_inter = bcum + m_prev[..., None]
    m_t = jnp.maximum(m_inter, jnp.max(logD, axis=-1))
    Dm = jnp.exp(logD - m_t[..., None])
    sqk = jnp.einsum('bhctd,bhcsd->bhcts', q, k) * Dm
    inter = jnp.exp(m_inter - m_t)
    num = jnp.einsum('bhcts,bhcsv->bhctv', sqk, v) + inter[..., None] * jnp.einsum('bhctk,bhckv->bhctv', q, C_prev)
    den = jnp.sum(sqk, axis=-1) + inter * jnp.einsum('bhctk,bhck->bhct', q, n_prev)
    h = num / jnp.maximum(jnp.abs(den), jnp.exp(-m_t))[..., None]
    return h.reshape(B, H, S, dv)


def _mlstm(p, p_gates, conv_w, conv_b, i_b, f_b, norm_g):
    H = MLSTM_HEADS
    qk, v, o = _split(p, (2 * MLSTM_QK_WIDTH, MLSTM_DIM, MLSTM_DIM))
    ig, fg = _split(p_gates, (H, H))
    S = qk.shape[1]
    qkp = jnp.pad(qk, ((0, 0), (MLSTM_CONV - 1, 0), (0, 0)))
    conv = conv_b
    for j in range(MLSTM_CONV):
        conv = conv + qkp[:, j:j + S] * conv_w[j]
    qk = jax.nn.silu(conv)
    q, k = jnp.split(qk, 2, axis=-1)
    q = _heads(q, H) * (MLSTM_QK_DIM ** -0.5)
    k = _heads(k, H)
    v = _heads(v, H)
    i_pre = jnp.moveaxis(ig + i_b, -1, 1)
    logf = jax.nn.log_sigmoid(jnp.moveaxis(fg + f_b, -1, 1))
    h = _mlstm_chunkwise(q, k, v, i_pre, logf)
    h = _rms(h, norm_g.reshape(H, 1, MLSTM_V_DIM))
    return jax.nn.sigmoid(o) * _merge_heads(h)


def _moba(q, k, v):
    B, H, S, d = q.shape
    BLK = MOBA_BLOCK
    nb = S // BLK
    kb = k.reshape(B, H, nb, BLK, d)
    pos = jnp.arange(S)
    qblk = pos // BLK
    gate = jnp.einsum('bhsd,bhnd->bhsn', q, jnp.mean(kb, axis=3), precision=HIGHEST)
    gate = jnp.where(jnp.arange(nb)[None, :] < qblk[:, None], gate, NEG)
    n_top = min(MOBA_TOPK, nb)
    _, sel = lax.top_k(gate, n_top)
    slot_ok = jnp.arange(n_top)[None, :] < qblk[:, None]
    lanes = jnp.arange(LANES)
    hit = (sel[..., None] == lanes) & slot_ok[:, :, None]
    blk_mask = jnp.any(hit, axis=3) | (lanes[None, :] == qblk[:, None])
    return _attention(q.astype(BF16), k.astype(BF16), v.astype(BF16), MOBA_BLOCK, d ** -0.5,
                      block_mask=blk_mask.astype(BF16), block=BLK)


def _nsa(p, gl, pos, q_norm, k_norm, cmp_pe, cmp_w1, cmp_b1, cmp_w2):
    B, S, _ = p.shape
    H, d = NSA_HEADS, ATTN_HEAD_DIM
    scale = d ** -0.5
    q, kc_in, vc_in, ks_in, vs_in, kw_in, vw_in = _split(p, (NSA_DIM, d, d, d, d, d, d))
    q = _rope(_rms(_heads(q, H), q_norm), pos)

    ncmp = (S - NSA_CMP_BLOCK) // NSA_CMP_STRIDE + 1
    cstart = NSA_CMP_STRIDE * jnp.arange(ncmp)
    cend = cstart + NSA_CMP_BLOCK - 1
    cidx = cstart[:, None] + jnp.arange(NSA_CMP_BLOCK)[None, :]

    def compress(z, i):
        blocks = (z[:, cidx] + cmp_pe[i]).reshape(B, ncmp, NSA_CMP_BLOCK * d)
        return jax.nn.gelu(blocks @ cmp_w1[i] + cmp_b1[i]) @ cmp_w2[i]

    k_cmp = _rope(_rms(compress(kc_in, 0), k_norm[0]), cend)
    v_cmp = compress(vc_in, 1)
    SB = NSA_SLC_BLOCK
    nsb = S // SB
    assert nsb <= LANES
    ncp = -(-ncmp // LANES) * LANES
    cs_np = NSA_CMP_STRIDE * np.arange(ncp)
    ss_np = SB * np.arange(LANES)
    overlap = ((cs_np[:, None] <= ss_np[None, :] + SB - 1) & (cs_np[:, None] + NSA_CMP_BLOCK - 1 >= ss_np[None, :])
               & (np.arange(ncp)[:, None] < ncmp) & (np.arange(LANES)[None, :] < nsb))
    qb = q.astype(BF16)
    cpad = ((0, 0), (0, ncp - ncmp), (0, 0))
    o_cmp, imp = _nsa_cmp_attention(qb, jnp.pad(k_cmp, cpad).astype(BF16), jnp.pad(v_cmp, cpad).astype(BF16),
                                    jnp.asarray(overlap, F32), ncmp, scale)
    imp = imp[:, :, :nsb]
    cur = pos // SB
    blk = jnp.arange(nsb)[None, :]
    forced = (blk == 0) | (blk == cur[:, None]) | (blk == cur[:, None] - 1)
    imp = jnp.where(forced, BIG, jnp.where(blk <= cur[:, None], imp, NEG))
    n_sel = min(NSA_SLC_TOPK, nsb)
    _, sel = lax.top_k(imp, n_sel)
    blk_mask = jnp.any(sel[..., None] == jnp.arange(LANES), axis=2)
    k_slc = _rope(_rms(ks_in, k_norm[1]), pos)
    o_slc = _attention(qb, k_slc[:, None].astype(BF16), vs_in[:, None].astype(BF16), NSA_Q_TILE, scale,
                       block_mask=blk_mask[:, None].astype(BF16), block=SB)

    k_win = _rope(_rms(kw_in, k_norm[2]), pos)
    o_win = _attention(qb, k_win[:, None].astype(BF16), vw_in[:, None].astype(BF16), NSA_Q_TILE, scale,
                       window=NSA_WINDOW)

    gates = jax.nn.sigmoid(gl).reshape(B, S, H, 3).transpose(0, 2, 1, 3)
    o = gates[..., 0:1] * o_cmp + gates[..., 1:2] * o_slc + gates[..., 2:3] * o_win
    return _merge_heads(o)


def _moe(xb, logits, x_res, gate, seq_len, bias, w_gate, w_up, w_down, s_gate, s_up, s_down):
    T, D = xb.shape
    A = T * TOP_K
    scores = jax.nn.sigmoid(logits)
    biased = scores + bias
    grp_score = lax.top_k(biased.reshape(T, N_GROUPS, N_EXPERTS // N_GROUPS), 2)[0].sum(-1)
    _, gidx = lax.top_k(grp_score, TOPK_GROUPS)
    gmask = jnp.any(gidx[:, :, None] == jnp.arange(N_GROUPS)[None, None, :], axis=1)
    emask = jnp.repeat(gmask, N_EXPERTS // N_GROUPS, axis=1)
    _, eidx = lax.top_k(jnp.where(emask, biased, NEG), TOP_K)
    wts = jnp.take_along_axis(scores, eidx, axis=1)
    wts = wts / jnp.sum(wts, axis=-1, keepdims=True) * ROUTED_SCALE

    e_flat = eidx.reshape(A).astype(jnp.int32)
    assign = jnp.arange(A, dtype=jnp.int32)
    e_s, order = lax.sort((e_flat, assign), num_keys=1)
    counts = jnp.sum(e_flat[:, None] == jnp.arange(N_EXPERTS, dtype=jnp.int32)[None, :], axis=0, dtype=jnp.int32)
    padded = (counts + MOE_BLOCK - 1) // MOE_BLOCK * MOE_BLOCK
    start = jnp.cumsum(counts) - counts
    pstart = jnp.cumsum(padded) - padded
    dest_sorted = pstart[e_s] + assign - start[e_s]
    _, slot = lax.sort((order, dest_sorted), num_keys=1)
    n_blk = -(-(A + N_EXPERTS * (MOE_BLOCK - 1)) // MOE_BLOCK)
    R = n_blk * MOE_BLOCK
    blk_e = jnp.minimum(jnp.sum((pstart + padded)[None, :] <= (jnp.arange(n_blk) * MOE_BLOCK)[:, None], axis=1),
                        N_EXPERTS - 1).astype(jnp.int32)
    n_used = (jnp.sum(padded) // MOE_BLOCK).astype(jnp.int32).reshape(1)
    row_e = jnp.repeat(blk_e, MOE_BLOCK)
    idx = jnp.arange(R, dtype=jnp.int32) - pstart[row_e]
    valid = idx < counts[row_e]
    a_src = order[jnp.where(valid, start[row_e] + idx, 0)]
    tok_buf = jnp.where(valid, a_src // TOP_K, 0)
    w_buf = jnp.where(valid, wts.reshape(A)[a_src], 0.0)

    yg = _moe_grouped(xb[tok_buf], w_buf[:, None], blk_e, n_used, w_gate, w_up, w_down)
    routed = jnp.sum(yg[slot].reshape(T, TOP_K, D).astype(F32), axis=1)

    gu = _matmul(xb, jnp.concatenate([s_gate, s_up], axis=1).astype(BF16))
    F = s_gate.shape[1]
    hmid = (jax.nn.silu(gu[:, :F]) * gu[:, F:]).astype(BF16)
    return _matmul_residual(hmid, s_down.astype(BF16), x_res, gate, seq_len, extra=routed)


def _projection_weights(w):
    o_b = RWKV_IN_DIM
    o_c = o_b + MLSTM_IN_DIM
    o_d = o_c + MOBA_IN_DIM
    o_g = o_d + NSA_IN_DIM
    mlstm_main = MLSTM_IN_DIM - 2 * MLSTM_HEADS
    nsa_main = NSA_IN_DIM - 3 * NSA_HEADS
    pieces = (('rwkv', 0, o_b), ('mlstm', o_b, o_b + mlstm_main), ('moba', o_c, o_d), ('nsa', o_d, o_d + nsa_main),
              ('mlstm_gates', o_b + mlstm_main, o_c), ('nsa_gates', o_d + nsa_main, o_g))
    cols, offs, cur = [], {}, 0
    for name, lo, hi in pieces:
        offs[name] = cur
        cols.append(w[:, lo:hi].astype(BF16))
        pad = -(hi - lo) % LANES
        if pad:
            cols.append(jnp.zeros((w.shape[0], pad), BF16))
        cur += hi - lo + pad
    return jnp.concatenate(cols, axis=1), w[:, o_g:].astype(BF16), offs


def kernel(x, c, ada_w, ada_b, norm_mix_g, norm_ffn_g, w_in, rwkv_mu, rwkv_w0, rwkv_w_up, rwkv_a0, rwkv_a_up, rwkv_g_up, rwkv_k_k, rwkv_k_a, rwkv_r_k, rwkv_ln_w, rwkv_ln_b, rwkv_v0, rwkv_v_down, rwkv_v_up, mlstm_conv_w, mlstm_conv_b, mlstm_i_b, mlstm_f_b, mlstm_norm_g, moba_q_norm, moba_k_norm, nsa_q_norm, nsa_k_norm, nsa_cmp_pe, nsa_cmp_w1, nsa_cmp_b1, nsa_cmp_w2, w_branch, w_out, moe_router, moe_bias, moe_w_gate, moe_w_up, moe_w_down, shared_w_gate, shared_w_up, shared_w_down):
    B, S, D = x.shape
    T = B * S
    depth = ada_w.shape[0]
    pos = jnp.arange(S)
    cond = jax.nn.silu(c)
    v_first = None
    x2d = x.reshape(T, D)
    for l in range(depth):
        mod = jnp.dot(cond, ada_w[l], precision=HIGHEST) + ada_b[l]
        sh_mix, sc_mix, gate_mix, sh_ffn, sc_ffn, gate_ffn = [z[:, None, :] for z in jnp.split(mod, 6, axis=-1)]

        hb = _normmod(x2d, norm_mix_g[l], sc_mix, sh_mix, S)
        w_mix, w_gates, offs = _projection_weights(w_in[l])
        proj = _matmul(hb, w_mix).reshape(B, S, -1)
        p_gate = _matmul(hb, w_gates, out_dtype=BF16)

        def seg(name, width):
            return proj[:, :, offs[name]:offs[name] + width]

        v_mix = None if l == 0 else (rwkv_v0[l - 1], rwkv_v_down[l - 1], rwkv_v_up[l - 1])
        y_a, v_first = _rwkv7(seg('rwkv', RWKV_IN_DIM), v_first, v_mix, rwkv_mu[l], rwkv_w0[l], rwkv_w_up[l],
                              rwkv_a0[l], rwkv_a_up[l], rwkv_g_up[l], rwkv_k_k[l], rwkv_k_a[l], rwkv_r_k[l],
                              rwkv_ln_w[l], rwkv_ln_b[l])
        y_b = _mlstm(seg('mlstm', MLSTM_IN_DIM - 2 * MLSTM_HEADS), seg('mlstm_gates', 2 * MLSTM_HEADS),
                     mlstm_conv_w[l], mlstm_conv_b[l], mlstm_i_b[l], mlstm_f_b[l], mlstm_norm_g[l])
        q_c, k_c, v_c = _split(seg('moba', MOBA_IN_DIM), (MOBA_DIM, MOBA_DIM, MOBA_DIM))
        q_c = _rope(_rms(_heads(q_c, MOBA_HEADS), moba_q_norm[l]), pos)
        k_c = _rope(_rms(_heads(k_c, MOBA_HEADS), moba_k_norm[l]), pos)
        y_c = _merge_heads(_moba(q_c, k_c, _heads(v_c, MOBA_HEADS)))
        y_d = _nsa(seg('nsa', NSA_IN_DIM - 3 * NSA_HEADS), seg('nsa_gates', 3 * NSA_HEADS), pos, nsa_q_norm[l],
                   nsa_k_norm[l], nsa_cmp_pe[l], nsa_cmp_w1[l], nsa_cmp_b1[l], nsa_cmp_w2[l])
        ys = [y.reshape(T, BRANCH_DIM).astype(BF16) for y in (y_a, y_b, y_c, y_d)]
        merged = _merge_branches(p_gate, ys, w_branch[l].astype(BF16))
        x2d = _matmul_residual(merged, w_out[l].astype(BF16), x2d, gate_mix, S)

        hb, logits = _normmod(x2d, norm_ffn_g[l], sc_ffn, sh_ffn, S, router=moe_router[l])
        x2d = _moe(hb, logits, x2d, gate_ffn, S, moe_bias[l], moe_w_gate[l], moe_w_up[l], moe_w_down[l],
                   shared_w_gate[l], shared_w_up[l], shared_w_down[l])
    return x2d.reshape(B, S, D)
```

```python
import functools

import numpy as np
import jax
import jax.numpy as jnp
from jax import lax
from jax.experimental import pallas as pl
from jax.experimental.pallas import tpu as pltpu

F32 = jnp.float32
BF16 = jnp.bfloat16
HIGHEST = lax.Precision.HIGHEST

D_MODEL = 2048
N_MIXERS = 4
BRANCH_DIM = 512
NORM_EPS = 1e-6
NEG = -1e30
BIG = 1e9

RWKV_HEADS = 8
RWKV_HEAD_DIM = 64
RWKV_DIM = RWKV_HEADS * RWKV_HEAD_DIM
RWKV_DECAY_LORA = 96
RWKV_ICLR_LORA = 96
RWKV_GATE_LORA = 256
RWKV_GN_EPS = 64e-5
RWKV_IN_DIM = 3 * RWKV_DIM + RWKV_DECAY_LORA + RWKV_ICLR_LORA + RWKV_GATE_LORA

MLSTM_HEADS = 4
MLSTM_QK_DIM = 128
MLSTM_V_DIM = 128
MLSTM_QK_WIDTH = MLSTM_HEADS * MLSTM_QK_DIM
MLSTM_DIM = MLSTM_HEADS * MLSTM_V_DIM
MLSTM_CONV = 4
MLSTM_CHUNK = 64
MLSTM_IN_DIM = 2 * MLSTM_QK_WIDTH + 2 * MLSTM_DIM + 2 * MLSTM_HEADS

ATTN_HEAD_DIM = 128
ROPE_DIM = ATTN_HEAD_DIM // 4
ROPE_THETA = 500000.0

MOBA_HEADS = 4
MOBA_DIM = MOBA_HEADS * ATTN_HEAD_DIM
MOBA_BLOCK = 256
MOBA_TOPK = 3
MOBA_IN_DIM = 3 * MOBA_DIM

NSA_HEADS = 4
NSA_DIM = NSA_HEADS * ATTN_HEAD_DIM
NSA_CMP_BLOCK = 32
NSA_CMP_STRIDE = 16
NSA_SLC_BLOCK = 64
NSA_SLC_TOPK = 16
NSA_WINDOW = 512
NSA_IN_DIM = NSA_DIM + 6 * ATTN_HEAD_DIM + 3 * NSA_HEADS

MERGE_IN_DIM = N_MIXERS * D_MODEL
IN_WIDTHS = (RWKV_IN_DIM, MLSTM_IN_DIM, MOBA_IN_DIM, NSA_IN_DIM, MERGE_IN_DIM)

N_EXPERTS = 64
TOP_K = 8
N_GROUPS = 8
TOPK_GROUPS = 4
EXPERT_DIM = 512
ROUTED_SCALE = 2.5
MOE_BLOCK = 256

V7X_VMEM_LIMIT_BYTES = 48 * 1024 * 1024
LANES = 128
SUBLANES = 8


def _mm_kernel(a_ref, b_ref, o_ref):
    o_ref[...] = jnp.dot(a_ref[...], b_ref[...], preferred_element_type=F32).astype(o_ref.dtype)


def _pick_tile(n, pref):
    t = min(pref, n)
    while n % t:
        t //= 2
    return t


def _pick_lane_tile(n, pref):
    assert n % LANES == 0
    units = n // LANES
    best = 1
    for u in range(1, units + 1):
        if units % u == 0 and u * LANES <= pref:
            best = u
    return best * LANES


def _matmul(a, b, out_dtype=F32, tm=1024, tn=1024):
    M, K = a.shape
    _, N = b.shape
    tm = _pick_tile(M, tm)
    tn = _pick_lane_tile(N, tn)
    return pl.pallas_call(
        _mm_kernel,
        grid=(N // tn, M // tm),
        in_specs=[pl.BlockSpec((tm, K), lambda j, i: (i, 0)),
                  pl.BlockSpec((K, tn), lambda j, i: (0, j))],
        out_specs=pl.BlockSpec((tm, tn), lambda j, i: (i, j)),
        out_shape=jax.ShapeDtypeStruct((M, N), out_dtype),
        compiler_params=pltpu.CompilerParams(dimension_semantics=("parallel", "parallel"),
                                             vmem_limit_bytes=V7X_VMEM_LIMIT_BYTES),
        name="dense_matmul",
    )(a, b)


def _mm_res_kernel(*refs, has_extra):
    if has_extra:
        a_ref, b_ref, res_ref, gate_ref, extra_ref, o_ref = refs
    else:
        a_ref, b_ref, res_ref, gate_ref, o_ref = refs
    y = jnp.dot(a_ref[...], b_ref[...], preferred_element_type=F32)
    if has_extra:
        y = y + extra_ref[...]
    o_ref[...] = res_ref[...] + gate_ref[0] * y


def _matmul_residual(a, b, res, gate, seq_len, extra=None, tm=1024, tn=1024):
    M, K = a.shape
    _, N = b.shape
    tm = _pick_tile(seq_len, tm)
    tn = _pick_lane_tile(N, tn)
    row_tile = pl.BlockSpec((tm, tn), lambda j, i: (i, j))
    in_specs = [pl.BlockSpec((tm, K), lambda j, i: (i, 0)),
                pl.BlockSpec((K, tn), lambda j, i: (0, j)),
                row_tile,
                pl.BlockSpec((1, 1, tn), lambda j, i: ((i * tm) // seq_len, 0, j))]
    args = [a, b, res, gate]
    if extra is not None:
        in_specs.append(row_tile)
        args.append(extra)
    return pl.pallas_call(
        functools.partial(_mm_res_kernel, has_extra=extra is not None),
        grid=(N // tn, M // tm),
        in_specs=in_specs,
        out_specs=row_tile,
        out_shape=jax.ShapeDtypeStruct((M, N), F32),
        compiler_params=pltpu.CompilerParams(dimension_semantics=("parallel", "parallel"),
                                             vmem_limit_bytes=V7X_VMEM_LIMIT_BYTES),
        name="matmul_gated_residual",
    )(*args)


def _normmod_kernel(*refs, with_router):
    if with_router:
        x_ref, g_ref, sc_ref, sh_ref, rt_ref, o_ref, lg_ref = refs
    else:
        x_ref, g_ref, sc_ref, sh_ref, o_ref = refs
    x = x_ref[...]
    y = x * lax.rsqrt(jnp.mean(x * x, axis=-1, keepdims=True) + NORM_EPS) * g_ref[...]
    h = y * (1.0 + sc_ref[0]) + sh_ref[0]
    o_ref[...] = h.astype(o_ref.dtype)
    if with_router:
        lg_ref[...] = jnp.dot(h, rt_ref[...], preferred_element_type=F32, precision=HIGHEST)


def _normmod(x2d, g, scale, shift, seq_len, router=None, tm=512):
    M, D = x2d.shape
    tm = _pick_tile(seq_len, tm)
    mod_spec = pl.BlockSpec((1, 1, D), lambda i: ((i * tm) // seq_len, 0, 0))
    row_spec = pl.BlockSpec((tm, D), lambda i: (i, 0))
    in_specs = [row_spec, pl.BlockSpec((1, D), lambda i: (0, 0)), mod_spec, mod_spec]
    args = [x2d, g.reshape(1, D), scale, shift]
    out_shape = [jax.ShapeDtypeStruct((M, D), BF16)]
    out_specs = [row_spec]
    if router is not None:
        n_exp = router.shape[1]
        E = -(-n_exp // LANES) * LANES
        in_specs.append(pl.BlockSpec((D, E), lambda i: (0, 0)))
        args.append(jnp.pad(router, ((0, 0), (0, E - n_exp))))
        out_shape.append(jax.ShapeDtypeStruct((M, E), F32))
        out_specs.append(pl.BlockSpec((tm, E), lambda i: (i, 0)))
    out = pl.pallas_call(
        functools.partial(_normmod_kernel, with_router=router is not None),
        grid=(M // tm,),
        in_specs=in_specs,
        out_specs=out_specs,
        out_shape=out_shape,
        compiler_params=pltpu.CompilerParams(dimension_semantics=("parallel",),
                                             vmem_limit_bytes=V7X_VMEM_LIMIT_BYTES),
        name="rmsnorm_modulate",
    )(*args)
    return (out[0], out[1][:, :router.shape[1]]) if router is not None else out[0]


def _merge_kernel(g0, g1, g2, g3, y0, y1, y2, y3, wb_ref, o_ref):
    acc = None
    for m, (g_ref, y_ref) in enumerate(((g0, y0), (g1, y1), (g2, y2), (g3, y3))):
        t = jnp.dot(y_ref[...], wb_ref[m], preferred_element_type=F32) * jax.nn.sigmoid(g_ref[...].astype(F32))
        acc = t if acc is None else acc + t
    o_ref[...] = acc.astype(o_ref.dtype)


def _merge_branches(p_gate, ys, w_branch, tm=512, tn=512):
    M = p_gate.shape[0]
    n_mix, kb, D = w_branch.shape
    tm = _pick_tile(M, tm)
    tn = _pick_lane_tile(D, tn)
    nj = D // tn
    gate_specs = [pl.BlockSpec((tm, tn), functools.partial(lambda j, i, m: (i, m * nj + j), m=m)) for m in range(n_mix)]
    y_specs = [pl.BlockSpec((tm, kb), lambda j, i: (i, 0)) for _ in range(n_mix)]
    return pl.pallas_call(
        _merge_kernel,
        grid=(nj, M // tm),
        in_specs=gate_specs + y_specs + [pl.BlockSpec((n_mix, kb, tn), lambda j, i: (0, 0, j))],
        out_specs=pl.BlockSpec((tm, tn), lambda j, i: (i, j)),
        out_shape=jax.ShapeDtypeStruct((M, D), BF16),
        compiler_params=pltpu.CompilerParams(dimension_semantics=("parallel", "parallel"),
                                             vmem_limit_bytes=V7X_VMEM_LIMIT_BYTES),
        name="merge_gated_branches",
    )(*([p_gate] * n_mix), *ys, w_branch)


ATTN_KEY_CHUNK = 512
NSA_Q_TILE = 128
MASKED_SCORE = -1e30
RUNNING_MAX_FLOOR = -1e20


def _attn_kernel(*refs, tqt, n_rep, kc, window, use_bm, scale):
    if use_bm:
        q_ref, k_ref, v_ref, bm_ref, e_ref, o_ref = refs
    else:
        q_ref, k_ref, v_ref, o_ref = refs
    t0 = pl.program_id(2) * tqt
    d = q_ref.shape[-1]
    rows = n_rep * tqt
    q = q_ref[0].reshape(rows, d)
    dmat = lax.broadcasted_iota(jnp.int32, (tqt, kc), 1) - lax.broadcasted_iota(jnp.int32, (tqt, kc), 0)

    def body(c, carry):
        m, l, acc = carry
        k0 = pl.multiple_of(c * kc, kc)
        kch = k_ref[0, 0, pl.ds(k0, kc), :]
        vch = v_ref[0, 0, pl.ds(k0, kc), :]
        s = lax.dot_general(q, kch, (((1,), (1,)), ((), ())), preferred_element_type=F32) * scale
        off = t0 - k0
        ok = dmat <= off
        if window is not None:
            ok = ok & (dmat > off - window)
        if use_bm:
            ok = ok & (jnp.dot(bm_ref[0, 0], e_ref[c], preferred_element_type=F32) > 0.5)
        bias = jnp.where(ok, 0.0, MASKED_SCORE)
        s = (s.reshape(n_rep, tqt, kc) + bias[None]).reshape(rows, kc)
        m_new = jnp.maximum(m, jnp.max(s, axis=-1, keepdims=True))
        alpha = jnp.exp(m - m_new)
        p = jnp.exp(s - m_new)
        l = alpha * l + jnp.sum(p, axis=-1, keepdims=True)
        acc = alpha * acc + jnp.dot(p.astype(BF16), vch, preferred_element_type=F32)
        return m_new, l, acc

    lo = 0 if window is None else jnp.maximum(t0 - window + 1, 0) // kc
    hi = (t0 + tqt + kc - 1) // kc
    init = (jnp.full((rows, 1), RUNNING_MAX_FLOOR, F32), jnp.zeros((rows, 1), F32), jnp.zeros((rows, d), F32))
    _, l, acc = lax.fori_loop(lo, hi, body, init)
    o_ref[0] = (acc / l).reshape(n_rep, tqt, d).astype(o_ref.dtype)


def _block_expander(seq_len, kc, block, n_lanes=LANES):
    key_blk = (np.arange(seq_len) // block).reshape(seq_len // kc, 1, kc)
    return jnp.asarray(key_blk == np.arange(n_lanes).reshape(1, n_lanes, 1), dtype=BF16)


def _attention(q, k, v, tqt, scale, block_mask=None, block=None, window=None):
    B, HQ, S, d = q.shape
    G = k.shape[1]
    n_rep = HQ // G
    kc = _pick_tile(S, ATTN_KEY_CHUNK)
    tqt = _pick_tile(S, tqt)
    use_bm = block_mask is not None
    qspec = pl.BlockSpec((1, n_rep, tqt, d), lambda b, g, i: (b, g, i, 0))
    kvspec = pl.BlockSpec((1, 1, S, d), lambda b, g, i: (b, g, 0, 0))
    in_specs = [qspec, kvspec, kvspec]
    args = [q, k, v]
    if use_bm:
        in_specs += [pl.BlockSpec((1, 1, tqt, LANES), lambda b, g, i: (b, g, i, 0)),
                     pl.BlockSpec((S // kc, LANES, kc), lambda b, g, i: (0, 0, 0))]
        args += [block_mask, _block_expander(S, kc, block)]
    return pl.pallas_call(
        functools.partial(_attn_kernel, tqt=tqt, n_rep=n_rep, kc=kc, window=window, use_bm=use_bm, scale=scale),
        grid=(B, G, S // tqt),
        in_specs=in_specs,
        out_specs=qspec,
        out_shape=jax.ShapeDtypeStruct((B, HQ, S, d), F32),
        compiler_params=pltpu.CompilerParams(dimension_semantics=("parallel", "parallel", "parallel"),
                                             vmem_limit_bytes=V7X_VMEM_LIMIT_BYTES),
        name="block_masked_attention",
    )(*args)


def _nsa_cmp_kernel(q_ref, kc_ref, vc_ref, ov_ref, o_ref, imp_ref, *, tqt, n_rep, ncmp, scale):
    t0 = pl.program_id(1) * tqt
    d = q_ref.shape[-1]
    ncp = kc_ref.shape[1]
    rows = n_rep * tqt
    q = q_ref[0].reshape(rows, d)
    s = lax.dot_general(q, kc_ref[0], (((1,), (1,)), ((), ())), preferred_element_type=F32) * scale
    n_io = lax.broadcasted_iota(jnp.int32, (tqt, ncp), 1)
    t_io = t0 + lax.broadcasted_iota(jnp.int32, (tqt, ncp), 0)
    valid = (n_io * NSA_CMP_STRIDE + (NSA_CMP_BLOCK - 1) <= t_io) & (n_io < ncmp)
    bias = jnp.where(valid, 0.0, MASKED_SCORE)
    s = s.reshape(n_rep, tqt, ncp) + bias[None]
    m = jnp.maximum(jnp.max(s, axis=-1, keepdims=True), RUNNING_MAX_FLOOR)
    e = jnp.exp(s - m)
    l = jnp.sum(e, axis=-1, keepdims=True)
    p = e / jnp.where(l > 0.0, l, 1.0)
    o = jnp.dot(p.reshape(rows, ncp).astype(BF16), vc_ref[0], preferred_element_type=F32)
    o_ref[0] = o.reshape(n_rep, tqt, d)
    imp_ref[0] = jnp.dot(jnp.sum(p, axis=0), ov_ref[...], preferred_element_type=F32, precision=HIGHEST)


def _nsa_cmp_attention(q, k_cmp, v_cmp, overlap, ncmp, scale, tqt=128):
    B, H, S, d = q.shape
    ncp = k_cmp.shape[1]
    tqt = _pick_tile(S, tqt)
    qspec = pl.BlockSpec((1, H, tqt, d), lambda b, i: (b, 0, i, 0))
    cspec = pl.BlockSpec((1, ncp, d), lambda b, i: (b, 0, 0))
    return pl.pallas_call(
        functools.partial(_nsa_cmp_kernel, tqt=tqt, n_rep=H, ncmp=ncmp, scale=scale),
        grid=(B, S // tqt),
        in_specs=[qspec, cspec, cspec, pl.BlockSpec((ncp, LANES), lambda b, i: (0, 0))],
        out_specs=[qspec, pl.BlockSpec((1, tqt, LANES), lambda b, i: (b, i, 0))],
        out_shape=[jax.ShapeDtypeStruct((B, H, S, d), F32), jax.ShapeDtypeStruct((B, S, LANES), F32)],
        compiler_params=pltpu.CompilerParams(dimension_semantics=("parallel", "parallel"),
                                             vmem_limit_bytes=V7X_VMEM_LIMIT_BYTES),
        name="nsa_compressed_attention",
    )(q, k_cmp, v_cmp, overlap)


RWKV_TIME_CHUNK = 32
RWKV_ROWS = RWKV_HEAD_DIM // 2
RWKV_ROW_GROUPS = RWKV_ROWS // SUBLANES
RWKV_KEY_TILES = RWKV_HEAD_DIM // SUBLANES


def _rwkv_scan_kernel(r_ref, w_ref, k_ref, kk_ref, b_ref, v_ref, y_ref, st_ref):
    @pl.when(pl.program_id(0) == 0)
    def _():
        st_ref[...] = jnp.zeros_like(st_ref)

    n_t = r_ref.shape[0]
    sub_iota = lax.broadcasted_iota(jnp.int32, (SUBLANES, st_ref.shape[-1]), 0)

    def step(t, carry):
        r = r_ref[t]
        w = w_ref[t]
        k = k_ref[t]
        kk = kk_ref[t]
        bb = b_ref[t]

        def group(g, c2):
            y_tile = jnp.zeros(sub_iota.shape, F32)
            for ii in range(SUBLANES):
                i = g * SUBLANES + ii
                s = st_ref[i]
                sa = jnp.sum(jnp.sum(s * kk, axis=0), axis=0, keepdims=True)
                vrow = v_ref[t, g, pl.ds(ii, 1), :]
                s = s * w - sa[None] * bb + vrow[None] * k
                st_ref[i] = s
                yrow = jnp.sum(jnp.sum(s * r, axis=0), axis=0, keepdims=True)
                y_tile = jnp.where(sub_iota == ii, jnp.broadcast_to(yrow, sub_iota.shape), y_tile)
            y_ref[t, g] = y_tile
            return c2

        lax.fori_loop(0, RWKV_ROW_GROUPS, group, 0)
        return carry

    lax.fori_loop(0, n_t, step, 0)


def _rwkv_scan(r, w, k, v, kk, bvec):
    B, S, H, N = r.shape
    chains = B * H
    lanes = 2 * chains

    def key_layout(z):
        zt = jnp.transpose(z, (1, 3, 0, 2)).reshape(S, N, chains)
        zt = jnp.concatenate([zt, zt], axis=-1)
        return zt.reshape(S, RWKV_KEY_TILES, SUBLANES, lanes)

    def val_layout(z):
        zt = jnp.transpose(z.reshape(B, S, H, 2, RWKV_ROWS), (1, 4, 3, 0, 2))
        return zt.reshape(S, RWKV_ROW_GROUPS, SUBLANES, lanes)

    tc = _pick_tile(S, RWKV_TIME_CHUNK)
    kspec = pl.BlockSpec((tc, RWKV_KEY_TILES, SUBLANES, lanes), lambda c: (c, 0, 0, 0))
    vspec = pl.BlockSpec((tc, RWKV_ROW_GROUPS, SUBLANES, lanes), lambda c: (c, 0, 0, 0))
    y = pl.pallas_call(
        _rwkv_scan_kernel,
        grid=(S // tc,),
        in_specs=[kspec, kspec, kspec, kspec, kspec, vspec],
        out_specs=vspec,
        out_shape=jax.ShapeDtypeStruct((S, RWKV_ROW_GROUPS, SUBLANES, lanes), F32),
        scratch_shapes=[pltpu.VMEM((RWKV_ROWS, RWKV_KEY_TILES, SUBLANES, lanes), F32)],
        compiler_params=pltpu.CompilerParams(dimension_semantics=("arbitrary",),
                                             vmem_limit_bytes=V7X_VMEM_LIMIT_BYTES),
        name="rwkv7_scan",
    )(key_layout(r), key_layout(w), key_layout(k), key_layout(kk), key_layout(bvec), val_layout(v))
    y = y.reshape(S, RWKV_ROWS, 2, B, H)
    return jnp.transpose(y, (3, 0, 4, 2, 1)).reshape(B, S, H, N)


def _moe_kernel(be_ref, nused_ref, x_ref, wg_ref, wu_ref, wd_ref, o_ref, wg_s, wu_s, wd_s):
    i = pl.program_id(0)
    active = i < nused_ref[0]

    @pl.when(active & ((i == 0) | (be_ref[i] != be_ref[jnp.maximum(i - 1, 0)])))
    def _():
        wg_s[...] = wg_ref[0].astype(BF16)
        wu_s[...] = wu_ref[0].astype(BF16)
        wd_s[...] = wd_ref[0].astype(BF16)

    @pl.when(active)
    def _():
        x = x_ref[...]
        g = jnp.dot(x, wg_s[...], preferred_element_type=F32)
        u = jnp.dot(x, wu_s[...], preferred_element_type=F32)
        hmid = (g * jax.nn.sigmoid(g) * u).astype(BF16)
        y = jnp.dot(hmid, wd_s[...], preferred_element_type=F32)
        o_ref[...] = y

    @pl.when(jnp.logical_not(active))
    def _():
        o_ref[...] = jnp.zeros_like(o_ref)


def _moe_grouped(xg, blk_e, n_used, w_gate, w_up, w_down):
    R, D = xg.shape
    n_blk = R // MOE_BLOCK
    E, _, F = w_gate.shape

    def xmap(i, be, nu):
        return (jnp.minimum(i, jnp.maximum(nu[0] - 1, 0)), 0)

    grid_spec = pltpu.PrefetchScalarGridSpec(
        num_scalar_prefetch=2,
        grid=(n_blk,),
        in_specs=[pl.BlockSpec((MOE_BLOCK, D), xmap),
                  pl.BlockSpec((1, D, F), lambda i, be, nu: (be[i], 0, 0)),
                  pl.BlockSpec((1, D, F), lambda i, be, nu: (be[i], 0, 0)),
                  pl.BlockSpec((1, F, D), lambda i, be, nu: (be[i], 0, 0))],
        out_specs=pl.BlockSpec((MOE_BLOCK, D), lambda i, be, nu: (i, 0)),
        scratch_shapes=[pltpu.VMEM((D, F), BF16), pltpu.VMEM((D, F), BF16), pltpu.VMEM((F, D), BF16)],
    )
    return pl.pallas_call(
        _moe_kernel,
        grid_spec=grid_spec,
        out_shape=jax.ShapeDtypeStruct((R, D), F32),
        compiler_params=pltpu.CompilerParams(dimension_semantics=("arbitrary",),
                                             vmem_limit_bytes=V7X_VMEM_LIMIT_BYTES),
        name="moe_grouped_swiglu",
    )(blk_e, n_used, xg, w_gate, w_up, w_down)


def _split(z, widths):
    cuts = [int(i) for i in np.cumsum(widths)[:-1]]
    return jnp.split(z, cuts, axis=-1)


def _rms(z, g, eps=NORM_EPS):
    return z * lax.rsqrt(jnp.mean(z * z, axis=-1, keepdims=True) + eps) * g


def _heads(z, n_heads):
    B, S, C = z.shape
    return z.reshape(B, S, n_heads, C // n_heads).transpose(0, 2, 1, 3)


def _merge_heads(z):
    B, H, S, d = z.shape
    return z.transpose(0, 2, 1, 3).reshape(B, S, H * d)


def _rope(z, pos):
    half = ROPE_DIM // 2
    inv_freq = ROPE_THETA ** (-jnp.arange(half, dtype=F32) / half)
    ang = pos.astype(F32)[:, None] * inv_freq[None, :]
    cos = jnp.cos(ang)
    sin = jnp.sin(ang)
    z1, z2, zr = z[..., :half], z[..., half:ROPE_DIM], z[..., ROPE_DIM:]
    return jnp.concatenate([z1 * cos - z2 * sin, z2 * cos + z1 * sin, zr], axis=-1)


def _token_shift(z):
    return jnp.pad(z, ((0, 0), (1, 0), (0, 0)))[:, :-1]


def _rwkv7(p, v_first, v_mix, mu, w0, w_up, a0, a_up, g_up, k_k, k_a, r_k, ln_w, ln_b):
    B, S, _ = p.shape
    H, N = RWKV_HEADS, RWKV_HEAD_DIM
    xs = p + (_token_shift(p) - p) * mu
    r, k, v, xw, xa, xg = _split(xs, (RWKV_DIM, RWKV_DIM, RWKV_DIM, RWKV_DECAY_LORA, RWKV_ICLR_LORA, RWKV_GATE_LORA))
    w = -jax.nn.softplus(-(w0 + jnp.tanh(xw) @ w_up)) - 0.5
    decay = jnp.exp(-jnp.exp(w))
    a = jax.nn.sigmoid(a0 + xa @ a_up)
    g = jax.nn.sigmoid(xg) @ g_up
    kk = (k * k_k).reshape(B, S, H, N)
    kk = kk / jnp.maximum(jnp.sqrt(jnp.sum(kk * kk, axis=-1, keepdims=True)), 1e-12)
    k = k * (1.0 + (a - 1.0) * k_a)
    if v_mix is None:
        v_first = v
    else:
        v0, v_down, v_up = v_mix
        v = v + (v_first - v) * jax.nn.sigmoid(v0 + (v @ v_down) @ v_up)

    def hd(z):
        return z.reshape(B, S, H, N)

    y = _rwkv_scan(hd(r), hd(decay), hd(k), hd(v), kk, kk * hd(a))
    mean = jnp.mean(y, axis=-1, keepdims=True)
    var = jnp.mean(jnp.square(y - mean), axis=-1, keepdims=True)
    y = ((y - mean) * lax.rsqrt(var + RWKV_GN_EPS)).reshape(B, S, RWKV_DIM) * ln_w + ln_b
    bonus = jnp.sum((r * k * r_k).reshape(B, S, H, N), axis=-1, keepdims=True) * v.reshape(B, S, H, N)
    y = (y + bonus.reshape(B, S, RWKV_DIM)) * g
    return y, v_first


def _mlstm_chunkwise(q, k, v, i_pre, logf):
    B, H, S, dk = q.shape
    dv = v.shape[-1]
    L = MLSTM_CHUNK
    NC = S // L
    q = q.reshape(B, H, NC, L, dk)
    k = k.reshape(B, H, NC, L, dk)
    v = v.reshape(B, H, NC, L, dv)
    ic = i_pre.reshape(B, H, NC, L)
    bcum = jnp.cumsum(logf.reshape(B, H, NC, L), axis=-1)
    gtot = bcum[..., -1]
    a_log = gtot[..., None] - bcum + ic
    a_max = jnp.max(a_log, axis=-1)

    def step(carry, inp):
        C, n, m = carry
        g_c, a_c, amax_c, k_c, v_c = inp
        m_new = jnp.maximum(g_c + m, amax_c)
        wgt = jnp.exp(a_c - m_new[..., None])
        dec = jnp.exp(g_c + m - m_new)
        C_new = dec[..., None, None] * C + jnp.einsum('bhl,bhlk,bhlv->bhkv', wgt, k_c, v_c)
        n_new = dec[..., None] * n + jnp.einsum('bhl,bhlk->bhk', wgt, k_c)
        return (C_new, n_new, m_new), (C, n, m)

    init = (jnp.zeros((B, H, dk, dv), F32), jnp.zeros((B, H, dk), F32), jnp.zeros((B, H), F32))
    xs = (jnp.moveaxis(gtot, 2, 0), jnp.moveaxis(a_log, 2, 0), jnp.moveaxis(a_max, 2, 0),
          jnp.moveaxis(k, 2, 0), jnp.moveaxis(v, 2, 0))
    _, (C_prev, n_prev, m_prev) = lax.scan(step, init, xs)
    C_prev = jnp.moveaxis(C_prev, 0, 2)
    n_prev = jnp.moveaxis(n_prev, 0, 2)
    m_prev = jnp.moveaxis(m_prev, 0, 2)

    causal = jnp.tril(jnp.ones((L, L), dtype=bool))
    logD = jnp.where(causal, bcum[..., :, None] - bcum[..., None, :] + ic[..., None, :], -jnp.inf)
    m_inter = bcum + m_prev[..., None]
    m_t = jnp.maximum(m_inter, jnp.max(logD, axis=-1))
    Dm = jnp.exp(logD - m_t[..., None])
    sqk = jnp.einsum('bhctd,bhcsd->bhcts', q, k) * Dm
    inter = jnp.exp(m_inter - m_t)
    num = jnp.einsum('bhcts,bhcsv->bhctv', sqk, v) + inter[..., None] * jnp.einsum('bhctk,bhckv->bhctv', q, C_prev)
    den = jnp.sum(sqk, axis=-1) + inter * jnp.einsum('bhctk,bhck->bhct', q, n_prev)
    h = num / jnp.maximum(jnp.abs(den), jnp.exp(-m_t))[..., None]
    return h.reshape(B, H, S, dv)


def _mlstm(p, p_gates, conv_w, conv_b, i_b, f_b, norm_g):
    H = MLSTM_HEADS
    qk, v, o = _split(p, (2 * MLSTM_QK_WIDTH, MLSTM_DIM, MLSTM_DIM))
    ig, fg = _split(p_gates, (H, H))
    S = qk.shape[1]
    qkp = jnp.pad(qk, ((0, 0), (MLSTM_CONV - 1, 0), (0, 0)))
    conv = conv_b
    for j in range(MLSTM_CONV):
        conv = conv + qkp[:, j:j + S] * conv_w[j]
    qk = jax.nn.silu(conv)
    q, k = jnp.split(qk, 2, axis=-1)
    q = _heads(q, H) * (MLSTM_QK_DIM ** -0.5)
    k = _heads(k, H)
    v = _heads(v, H)
    i_pre = jnp.moveaxis(ig + i_b, -1, 1)
    logf = jax.nn.log_sigmoid(jnp.moveaxis(fg + f_b, -1, 1))
    h = _mlstm_chunkwise(q, k, v, i_pre, logf)
    h = _rms(h, norm_g.reshape(H, 1, MLSTM_V_DIM))
    return jax.nn.sigmoid(o) * _merge_heads(h)


def _moba(q, k, v):
    B, H, S, d = q.shape
    BLK = MOBA_BLOCK
    nb = S // BLK
    kb = k.reshape(B, H, nb, BLK, d)
    pos = jnp.arange(S)
    qblk = pos // BLK
    gate = jnp.einsum('bhsd,bhnd->bhsn', q, jnp.mean(kb, axis=3), precision=HIGHEST)
    gate = jnp.where(jnp.arange(nb)[None, :] < qblk[:, None], gate, NEG)
    n_top = min(MOBA_TOPK, nb)
    _, sel = lax.top_k(gate, n_top)
    slot_ok = jnp.arange(n_top)[None, :] < qblk[:, None]
    lanes = jnp.arange(LANES)
    hit = (sel[..., None] == lanes) & slot_ok[:, :, None]
    blk_mask = jnp.any(hit, axis=3) | (lanes[None, :] == qblk[:, None])
    return _attention(q.astype(BF16), k.astype(BF16), v.astype(BF16), MOBA_BLOCK, d ** -0.5,
                      block_mask=blk_mask.astype(BF16), block=BLK)


def _nsa(p, gl, pos, q_norm, k_norm, cmp_pe, cmp_w1, cmp_b1, cmp_w2):
    B, S, _ = p.shape
    H, d = NSA_HEADS, ATTN_HEAD_DIM
    scale = d ** -0.5
    q, kc_in, vc_in, ks_in, vs_in, kw_in, vw_in = _split(p, (NSA_DIM, d, d, d, d, d, d))
    q = _rope(_rms(_heads(q, H), q_norm), pos)

    ncmp = (S - NSA_CMP_BLOCK) // NSA_CMP_STRIDE + 1
    cstart = NSA_CMP_STRIDE * jnp.arange(ncmp)
    cend = cstart + NSA_CMP_BLOCK - 1
    cidx = cstart[:, None] + jnp.arange(NSA_CMP_BLOCK)[None, :]

    def compress(z, i):
        blocks = (z[:, cidx] + cmp_pe[i]).reshape(B, ncmp, NSA_CMP_BLOCK * d)
        return jax.nn.gelu(blocks @ cmp_w1[i] + cmp_b1[i]) @ cmp_w2[i]

    k_cmp = _rope(_rms(compress(kc_in, 0), k_norm[0]), cend)
    v_cmp = compress(vc_in, 1)
    SB = NSA_SLC_BLOCK
    nsb = S // SB
    assert nsb <= LANES
    ncp = -(-ncmp // LANES) * LANES
    cs_np = NSA_CMP_STRIDE * np.arange(ncp)
    ss_np = SB * np.arange(LANES)
    overlap = ((cs_np[:, None] <= ss_np[None, :] + SB - 1) & (cs_np[:, None] + NSA_CMP_BLOCK - 1 >= ss_np[None, :])
               & (np.arange(ncp)[:, None] < ncmp) & (np.arange(LANES)[None, :] < nsb))
    qb = q.astype(BF16)
    cpad = ((0, 0), (0, ncp - ncmp), (0, 0))
    o_cmp, imp = _nsa_cmp_attention(qb, jnp.pad(k_cmp, cpad).astype(BF16), jnp.pad(v_cmp, cpad).astype(BF16),
                                    jnp.asarray(overlap, F32), ncmp, scale)
    imp = imp[:, :, :nsb]
    cur = pos // SB
    blk = jnp.arange(nsb)[None, :]
    forced = (blk == 0) | (blk == cur[:, None]) | (blk == cur[:, None] - 1)
    imp = jnp.where(forced, BIG, jnp.where(blk <= cur[:, None], imp, NEG))
    n_sel = min(NSA_SLC_TOPK, nsb)
    _, sel = lax.top_k(imp, n_sel)
    blk_mask = jnp.any(sel[..., None] == jnp.arange(LANES), axis=2)
    k_slc = _rope(_rms(ks_in, k_norm[1]), pos)
    o_slc = _attention(qb, k_slc[:, None].astype(BF16), vs_in[:, None].astype(BF16), NSA_Q_TILE, scale,
                       block_mask=blk_mask[:, None].astype(BF16), block=SB)

    k_win = _rope(_rms(kw_in, k_norm[2]), pos)
    o_win = _attention(qb, k_win[:, None].astype(BF16), vw_in[:, None].astype(BF16), NSA_Q_TILE, scale,
                       window=NSA_WINDOW)

    gates = jax.nn.sigmoid(gl).reshape(B, S, H, 3).transpose(0, 2, 1, 3)
    o = gates[..., 0:1] * o_cmp + gates[..., 1:2] * o_slc + gates[..., 2:3] * o_win
    return _merge_heads(o)


def _moe(xb, logits, x_res, gate, seq_len, bias, w_gate, w_up, w_down, s_gate, s_up, s_down):
    T, D = xb.shape
    A = T * TOP_K
    scores = jax.nn.sigmoid(logits)
    biased = scores + bias
    grp_score = lax.top_k(biased.reshape(T, N_GROUPS, N_EXPERTS // N_GROUPS), 2)[0].sum(-1)
    _, gidx = lax.top_k(grp_score, TOPK_GROUPS)
    gmask = jnp.any(gidx[:, :, None] == jnp.arange(N_GROUPS)[None, None, :], axis=1)
    emask = jnp.repeat(gmask, N_EXPERTS // N_GROUPS, axis=1)
    _, eidx = lax.top_k(jnp.where(emask, biased, NEG), TOP_K)
    wts = jnp.take_along_axis(scores, eidx, axis=1)
    wts = wts / jnp.sum(wts, axis=-1, keepdims=True) * ROUTED_SCALE

    experts = jnp.arange(N_EXPERTS, dtype=jnp.int32)
    eidx = eidx.astype(jnp.int32)
    hot = eidx[:, :, None] == experts
    per_tok = jnp.sum(hot, axis=1, dtype=F32)
    ch = _pick_tile(T, LANES)
    tri = jnp.tril(jnp.ones((ch, ch), F32))
    within = jnp.einsum('ij,cjk->cik', tri, per_tok.reshape(T // ch, ch, N_EXPERTS), precision=HIGHEST)
    tot = within[:, -1, :]
    before = jnp.cumsum(tot, axis=0) - tot
    prefix = (within + before[:, None, :]).reshape(T, N_EXPERTS) - per_tok
    counts = jnp.sum(tot, axis=0).astype(jnp.int32)
    padded = (counts + MOE_BLOCK - 1) // MOE_BLOCK * MOE_BLOCK
    pstart = jnp.cumsum(padded) - padded
    slot = jnp.sum(jnp.where(hot, prefix[:, None, :] + pstart.astype(F32), 0.0), axis=-1).astype(jnp.int32)

    n_blk = -(-(A + N_EXPERTS * (MOE_BLOCK - 1)) // MOE_BLOCK)
    R = n_blk * MOE_BLOCK
    blk_e = jnp.minimum(jnp.sum((pstart + padded)[None, :] <= (jnp.arange(n_blk) * MOE_BLOCK)[:, None], axis=1),
                        N_EXPERTS - 1).astype(jnp.int32)
    n_used = (jnp.sum(padded) // MOE_BLOCK).astype(jnp.int32).reshape(1)
    tok_bits = max(int(T - 1).bit_length(), 1)
    tok_ids = jnp.arange(T, dtype=jnp.int32)[:, None]
    real_keys = (eidx << (tok_bits + 1)) | tok_ids
    n_fill = MOE_BLOCK - 1
    fill_j = jnp.arange(n_fill, dtype=jnp.int32)[None, :]
    unused = jnp.int32(N_EXPERTS << (tok_bits + 1))
    fill_keys = jnp.where(fill_j < (padded - counts)[:, None], (experts[:, None] << (tok_bits + 1)) | (1 << tok_bits), unused)
    tail = jnp.full((R - A - N_EXPERTS * n_fill,), unused, jnp.int32)
    keys = lax.sort(jnp.concatenate([real_keys.reshape(A), fill_keys.reshape(-1), tail]))
    tok_buf = jnp.where((keys >> tok_bits) & 1 == 1, 0, keys & ((1 << tok_bits) - 1))
    tok_buf = jnp.where(keys >= unused, 0, tok_buf)

    yg = _moe_grouped(xb[tok_buf], blk_e, n_used, w_gate, w_up, w_down)
    routed = jnp.sum(yg[slot.reshape(A)].reshape(T, TOP_K, D) * wts[:, :, None], axis=1)

    gu = _matmul(xb, jnp.concatenate([s_gate, s_up], axis=1).astype(BF16))
    F = s_gate.shape[1]
    hmid = (jax.nn.silu(gu[:, :F]) * gu[:, F:]).astype(BF16)
    return _matmul_residual(hmid, s_down.astype(BF16), x_res, gate, seq_len, extra=routed)


def _projection_layout():
    o_b = RWKV_IN_DIM
    o_c = o_b + MLSTM_IN_DIM
    o_d = o_c + MOBA_IN_DIM
    o_g = o_d + NSA_IN_DIM
    mlstm_main = MLSTM_IN_DIM - 2 * MLSTM_HEADS
    nsa_main = NSA_IN_DIM - 3 * NSA_HEADS
    spans = (('rwkv', 0, o_b), ('mlstm', o_b, o_b + mlstm_main), ('moba', o_c, o_d), ('nsa', o_d, o_d + nsa_main),
             ('mlstm_gates', o_b + mlstm_main, o_c), ('nsa_gates', o_d + nsa_main, o_g))
    pieces, cur = [], 0
    for name, lo, hi in spans:
        pieces.append((name, lo, hi, cur))
        cur += -(-(hi - lo) // LANES) * LANES
    return tuple(pieces), cur, o_g


def _relayout_kernel(w_ref, mix_ref, gates_ref, *, pieces, gate_lo):
    for _, lo, hi, off in pieces:
        width = hi - lo
        mix_ref[:, off:off + width] = w_ref[0, :, lo:hi].astype(BF16)
        pad = -width % LANES
        if pad:
            mix_ref[:, off + width:off + width + pad] = jnp.zeros((mix_ref.shape[0], pad), BF16)
    gates_ref[...] = w_ref[0, :, gate_lo:].astype(BF16)


def _projection_weights(w_in, layer, rows=128):
    _, D, n_in = w_in.shape
    pieces, mix_width, gate_lo = _projection_layout()
    w_mix, w_gates = pl.pallas_call(
        functools.partial(_relayout_kernel, pieces=pieces, gate_lo=gate_lo),
        grid=(D // rows,),
        in_specs=[pl.BlockSpec((1, rows, n_in), lambda i: (layer, i, 0))],
        out_specs=[pl.BlockSpec((rows, mix_width), lambda i: (i, 0)),
                   pl.BlockSpec((rows, n_in - gate_lo), lambda i: (i, 0))],
        out_shape=[jax.ShapeDtypeStruct((D, mix_width), BF16), jax.ShapeDtypeStruct((D, n_in - gate_lo), BF16)],
        compiler_params=pltpu.CompilerParams(dimension_semantics=("parallel",),
                                             vmem_limit_bytes=V7X_VMEM_LIMIT_BYTES),
        name="projection_weight_relayout",
    )(w_in)
    return w_mix, w_gates, {name: off for name, _, _, off in pieces}


def kernel(x, c, ada_w, ada_b, norm_mix_g, norm_ffn_g, w_in, rwkv_mu, rwkv_w0, rwkv_w_up, rwkv_a0, rwkv_a_up, rwkv_g_up, rwkv_k_k, rwkv_k_a, rwkv_r_k, rwkv_ln_w, rwkv_ln_b, rwkv_v0, rwkv_v_down, rwkv_v_up, mlstm_conv_w, mlstm_conv_b, mlstm_i_b, mlstm_f_b, mlstm_norm_g, moba_q_norm, moba_k_norm, nsa_q_norm, nsa_k_norm, nsa_cmp_pe, nsa_cmp_w1, nsa_cmp_b1, nsa_cmp_w2, w_branch, w_out, moe_router, moe_bias, moe_w_gate, moe_w_up, moe_w_down, shared_w_gate, shared_w_up, shared_w_down):
    B, S, D = x.shape
    T = B * S
    depth = ada_w.shape[0]
    pos = jnp.arange(S)
    cond = jax.nn.silu(c)
    v_first = None
    x2d = x.reshape(T, D)
    for l in range(depth):
        mod = jnp.dot(cond, ada_w[l], precision=HIGHEST) + ada_b[l]
        sh_mix, sc_mix, gate_mix, sh_ffn, sc_ffn, gate_ffn = [z[:, None, :] for z in jnp.split(mod, 6, axis=-1)]

        hb = _normmod(x2d, norm_mix_g[l], sc_mix, sh_mix, S)
        w_mix, w_gates, offs = _projection_weights(w_in, l)
        proj = _matmul(hb, w_mix).reshape(B, S, -1)
        p_gate = _matmul(hb, w_gates, out_dtype=BF16)

        def seg(name, width):
            return proj[:, :, offs[name]:offs[name] + width]

        v_mix = None if l == 0 else (rwkv_v0[l - 1], rwkv_v_down[l - 1], rwkv_v_up[l - 1])
        y_a, v_first = _rwkv7(seg('rwkv', RWKV_IN_DIM), v_first, v_mix, rwkv_mu[l], rwkv_w0[l], rwkv_w_up[l],
                              rwkv_a0[l], rwkv_a_up[l], rwkv_g_up[l], rwkv_k_k[l], rwkv_k_a[l], rwkv_r_k[l],
                              rwkv_ln_w[l], rwkv_ln_b[l])
        y_b = _mlstm(seg('mlstm', MLSTM_IN_DIM - 2 * MLSTM_HEADS), seg('mlstm_gates', 2 * MLSTM_HEADS),
                     mlstm_conv_w[l], mlstm_conv_b[l], mlstm_i_b[l], mlstm_f_b[l], mlstm_norm_g[l])
        q_c, k_c, v_c = _split(seg('moba', MOBA_IN_DIM), (MOBA_DIM, MOBA_DIM, MOBA_DIM))
        q_c = _rope(_rms(_heads(q_c, MOBA_HEADS), moba_q_norm[l]), pos)
        k_c = _rope(_rms(_heads(k_c, MOBA_HEADS), moba_k_norm[l]), pos)
        y_c = _merge_heads(_moba(q_c, k_c, _heads(v_c, MOBA_HEADS)))
        y_d = _nsa(seg('nsa', NSA_IN_DIM - 3 * NSA_HEADS), seg('nsa_gates', 3 * NSA_HEADS), pos, nsa_q_norm[l],
                   nsa_k_norm[l], nsa_cmp_pe[l], nsa_cmp_w1[l], nsa_cmp_b1[l], nsa_cmp_w2[l])
        ys = [y.reshape(T, BRANCH_DIM).astype(BF16) for y in (y_a, y_b, y_c, y_d)]
        merged = _merge_branches(p_gate, ys, w_branch[l].astype(BF16))
        x2d = _matmul_residual(merged, w_out[l].astype(BF16), x2d, gate_mix, S)

        hb, logits = _normmod(x2d, norm_ffn_g[l], sc_ffn, sh_ffn, S, router=moe_router[l])
        x2d = _moe(hb, logits, x2d, gate_ffn, S, moe_bias[l], moe_w_gate[l], moe_w_up[l], moe_w_down[l],
                   shared_w_gate[l], shared_w_up[l], shared_w_down[l])
    return x2d.reshape(B, S, D)
```

```python
import functools

import numpy as np
import jax
import jax.numpy as jnp
from jax import lax
from jax.experimental import pallas as pl
from jax.experimental.pallas import tpu as pltpu

F32 = jnp.float32
BF16 = jnp.bfloat16
HIGHEST = lax.Precision.HIGHEST

D_MODEL = 2048
N_MIXERS = 4
BRANCH_DIM = 512
NORM_EPS = 1e-6
NEG = -1e30
BIG = 1e9

RWKV_HEADS = 8
RWKV_HEAD_DIM = 64
RWKV_DIM = RWKV_HEADS * RWKV_HEAD_DIM
RWKV_DECAY_LORA = 96
RWKV_ICLR_LORA = 96
RWKV_GATE_LORA = 256
RWKV_GN_EPS = 64e-5
RWKV_IN_DIM = 3 * RWKV_DIM + RWKV_DECAY_LORA + RWKV_ICLR_LORA + RWKV_GATE_LORA

MLSTM_HEADS = 4
MLSTM_QK_DIM = 128
MLSTM_V_DIM = 128
MLSTM_QK_WIDTH = MLSTM_HEADS * MLSTM_QK_DIM
MLSTM_DIM = MLSTM_HEADS * MLSTM_V_DIM
MLSTM_CONV = 4
MLSTM_CHUNK = 64
MLSTM_IN_DIM = 2 * MLSTM_QK_WIDTH + 2 * MLSTM_DIM + 2 * MLSTM_HEADS

ATTN_HEAD_DIM = 128
ROPE_DIM = ATTN_HEAD_DIM // 4
ROPE_THETA = 500000.0

MOBA_HEADS = 4
MOBA_DIM = MOBA_HEADS * ATTN_HEAD_DIM
MOBA_BLOCK = 256
MOBA_TOPK = 3
MOBA_IN_DIM = 3 * MOBA_DIM

NSA_HEADS = 4
NSA_DIM = NSA_HEADS * ATTN_HEAD_DIM
NSA_CMP_BLOCK = 32
NSA_CMP_STRIDE = 16
NSA_SLC_BLOCK = 64
NSA_SLC_TOPK = 16
NSA_WINDOW = 512
NSA_IN_DIM = NSA_DIM + 6 * ATTN_HEAD_DIM + 3 * NSA_HEADS

MERGE_IN_DIM = N_MIXERS * D_MODEL
IN_WIDTHS = (RWKV_IN_DIM, MLSTM_IN_DIM, MOBA_IN_DIM, NSA_IN_DIM, MERGE_IN_DIM)

N_EXPERTS = 64
TOP_K = 8
N_GROUPS = 8
TOPK_GROUPS = 4
EXPERT_DIM = 512
ROUTED_SCALE = 2.5
MOE_BLOCK = 256

V7X_VMEM_LIMIT_BYTES = 48 * 1024 * 1024
LANES = 128
SUBLANES = 8


def _mm_kernel(a_ref, b_ref, o_ref):
    o_ref[...] = jnp.dot(a_ref[...], b_ref[...], preferred_element_type=F32).astype(o_ref.dtype)


def _pick_tile(n, pref):
    t = min(pref, n)
    while n % t:
        t //= 2
    return t


def _pick_lane_tile(n, pref):
    assert n % LANES == 0
    units = n // LANES
    best = 1
    for u in range(1, units + 1):
        if units % u == 0 and u * LANES <= pref:
            best = u
    return best * LANES


def _matmul(a, b, out_dtype=F32, tm=1024, tn=1024, cols=None):
    M, K = a.shape
    off, N = (0, b.shape[1]) if cols is None else cols
    tm = _pick_tile(M, tm)
    tn = _pick_lane_tile(int(np.gcd(N, off)) if off else N, tn)
    first = off // tn
    return pl.pallas_call(
        _mm_kernel,
        grid=(N // tn, M // tm),
        in_specs=[pl.BlockSpec((tm, K), lambda j, i: (i, 0)),
                  pl.BlockSpec((K, tn), lambda j, i: (0, first + j))],
        out_specs=pl.BlockSpec((tm, tn), lambda j, i: (i, j)),
        out_shape=jax.ShapeDtypeStruct((M, N), out_dtype),
        compiler_params=pltpu.CompilerParams(dimension_semantics=("parallel", "parallel"),
                                             vmem_limit_bytes=V7X_VMEM_LIMIT_BYTES),
        name="dense_matmul",
    )(a, b)


def _mm_res_kernel(*refs, has_extra):
    if has_extra:
        a_ref, b_ref, res_ref, gate_ref, extra_ref, o_ref = refs
    else:
        a_ref, b_ref, res_ref, gate_ref, o_ref = refs
    y = jnp.dot(a_ref[...], b_ref[...], preferred_element_type=F32)
    if has_extra:
        y = y + extra_ref[...]
    o_ref[...] = res_ref[...] + gate_ref[0] * y


def _matmul_residual(a, b, res, gate, seq_len, extra=None, tm=1024, tn=1024):
    M, K = a.shape
    _, N = b.shape
    tm = _pick_tile(seq_len, tm)
    tn = _pick_lane_tile(N, tn)
    row_tile = pl.BlockSpec((tm, tn), lambda j, i: (i, j))
    in_specs = [pl.BlockSpec((tm, K), lambda j, i: (i, 0)),
                pl.BlockSpec((K, tn), lambda j, i: (0, j)),
                row_tile,
                pl.BlockSpec((1, 1, tn), lambda j, i: ((i * tm) // seq_len, 0, j))]
    args = [a, b, res, gate]
    if extra is not None:
        in_specs.append(row_tile)
        args.append(extra)
    return pl.pallas_call(
        functools.partial(_mm_res_kernel, has_extra=extra is not None),
        grid=(N // tn, M // tm),
        in_specs=in_specs,
        out_specs=row_tile,
        out_shape=jax.ShapeDtypeStruct((M, N), F32),
        compiler_params=pltpu.CompilerParams(dimension_semantics=("parallel", "parallel"),
                                             vmem_limit_bytes=V7X_VMEM_LIMIT_BYTES),
        name="matmul_gated_residual",
    )(*args)


def _normmod_kernel(*refs, with_router):
    if with_router:
        x_ref, g_ref, sc_ref, sh_ref, rt_ref, o_ref, lg_ref = refs
    else:
        x_ref, g_ref, sc_ref, sh_ref, o_ref = refs
    x = x_ref[...]
    y = x * lax.rsqrt(jnp.mean(x * x, axis=-1, keepdims=True) + NORM_EPS) * g_ref[...]
    h = y * (1.0 + sc_ref[0]) + sh_ref[0]
    o_ref[...] = h.astype(o_ref.dtype)
    if with_router:
        lg_ref[...] = lax.dot_general(rt_ref[...], h, (((1,), (1,)), ((), ())),
                                      preferred_element_type=F32, precision=HIGHEST)


def _normmod(x2d, g, scale, shift, seq_len, router=None, tm=512):
    M, D = x2d.shape
    tm = _pick_tile(seq_len, tm)
    mod_spec = pl.BlockSpec((1, 1, D), lambda i: ((i * tm) // seq_len, 0, 0))
    row_spec = pl.BlockSpec((tm, D), lambda i: (i, 0))
    in_specs = [row_spec, pl.BlockSpec((1, D), lambda i: (0, 0)), mod_spec, mod_spec]
    args = [x2d, g.reshape(1, D), scale, shift]
    out_shape = [jax.ShapeDtypeStruct((M, D), BF16)]
    out_specs = [row_spec]
    if router is not None:
        n_exp = router.shape[1]
        in_specs.append(pl.BlockSpec((n_exp, D), lambda i: (0, 0)))
        args.append(router.T)
        out_shape.append(jax.ShapeDtypeStruct((n_exp, M), F32))
        out_specs.append(pl.BlockSpec((n_exp, tm), lambda i: (0, i)))
    out = pl.pallas_call(
        functools.partial(_normmod_kernel, with_router=router is not None),
        grid=(M // tm,),
        in_specs=in_specs,
        out_specs=out_specs,
        out_shape=out_shape,
        compiler_params=pltpu.CompilerParams(dimension_semantics=("parallel",),
                                             vmem_limit_bytes=V7X_VMEM_LIMIT_BYTES),
        name="rmsnorm_modulate",
    )(*args)
    return tuple(out) if router is not None else out[0]


def _merge_kernel(g0, g1, g2, g3, y0, y1, y2, y3, wb_ref, o_ref):
    acc = None
    for m, (g_ref, y_ref) in enumerate(((g0, y0), (g1, y1), (g2, y2), (g3, y3))):
        t = jnp.dot(y_ref[...], wb_ref[m], preferred_element_type=F32) * jax.nn.sigmoid(g_ref[...].astype(F32))
        acc = t if acc is None else acc + t
    o_ref[...] = acc.astype(o_ref.dtype)


def _merge_branches(p_gate, ys, w_branch, tm=512, tn=512):
    M = p_gate.shape[0]
    n_mix, kb, D = w_branch.shape
    tm = _pick_tile(M, tm)
    tn = _pick_lane_tile(D, tn)
    nj = D // tn
    gate_specs = [pl.BlockSpec((tm, tn), functools.partial(lambda j, i, m: (i, m * nj + j), m=m)) for m in range(n_mix)]
    y_specs = [pl.BlockSpec((tm, kb), lambda j, i: (i, 0)) for _ in range(n_mix)]
    return pl.pallas_call(
        _merge_kernel,
        grid=(nj, M // tm),
        in_specs=gate_specs + y_specs + [pl.BlockSpec((n_mix, kb, tn), lambda j, i: (0, 0, j))],
        out_specs=pl.BlockSpec((tm, tn), lambda j, i: (i, j)),
        out_shape=jax.ShapeDtypeStruct((M, D), BF16),
        compiler_params=pltpu.CompilerParams(dimension_semantics=("parallel", "parallel"),
                                             vmem_limit_bytes=V7X_VMEM_LIMIT_BYTES),
        name="merge_gated_branches",
    )(*([p_gate] * n_mix), *ys, w_branch)


ATTN_KEY_CHUNK = 512
NSA_Q_TILE = 128
MASKED_SCORE = -1e30
RUNNING_MAX_FLOOR = -1e20


def _attn_kernel(*refs, tqt, n_rep, kc, window, use_bm, scale):
    if use_bm:
        q_ref, k_ref, v_ref, bm_ref, e_ref, o_ref = refs
    else:
        q_ref, k_ref, v_ref, o_ref = refs
    t0 = pl.program_id(2) * tqt
    d = q_ref.shape[-1]
    rows = n_rep * tqt
    q = q_ref[0].reshape(rows, d)
    dmat = lax.broadcasted_iota(jnp.int32, (tqt, kc), 1) - lax.broadcasted_iota(jnp.int32, (tqt, kc), 0)

    def body(c, carry):
        m, l, acc = carry
        k0 = pl.multiple_of(c * kc, kc)
        kch = k_ref[0, 0, pl.ds(k0, kc), :]
        vch = v_ref[0, 0, pl.ds(k0, kc), :]
        s = lax.dot_general(q, kch, (((1,), (1,)), ((), ())), preferred_element_type=F32) * scale
        off = t0 - k0
        ok = dmat <= off
        if window is not None:
            ok = ok & (dmat > off - window)
        if use_bm:
            ok = ok & (jnp.dot(bm_ref[0, 0], e_ref[c], preferred_element_type=F32) > 0.5)
        bias = jnp.where(ok, 0.0, MASKED_SCORE)
        s = (s.reshape(n_rep, tqt, kc) + bias[None]).reshape(rows, kc)
        m_new = jnp.maximum(m, jnp.max(s, axis=-1, keepdims=True))
        alpha = jnp.exp(m - m_new)
        p = jnp.exp(s - m_new)
        l = alpha * l + jnp.sum(p, axis=-1, keepdims=True)
        acc = alpha * acc + jnp.dot(p.astype(BF16), vch, preferred_element_type=F32)
        return m_new, l, acc

    lo = 0 if window is None else jnp.maximum(t0 - window + 1, 0) // kc
    hi = (t0 + tqt + kc - 1) // kc
    init = (jnp.full((rows, 1), RUNNING_MAX_FLOOR, F32), jnp.zeros((rows, 1), F32), jnp.zeros((rows, d), F32))
    _, l, acc = lax.fori_loop(lo, hi, body, init)
    o_ref[0] = (acc / l).reshape(n_rep, tqt, d).astype(o_ref.dtype)


def _block_expander(seq_len, kc, block, n_lanes=LANES):
    key_blk = (np.arange(seq_len) // block).reshape(seq_len // kc, 1, kc)
    return jnp.asarray(key_blk == np.arange(n_lanes).reshape(1, n_lanes, 1), dtype=BF16)


def _attention(q, k, v, tqt, scale, block_mask=None, block=None, window=None):
    B, HQ, S, d = q.shape
    G = k.shape[1]
    n_rep = HQ // G
    kc = _pick_tile(S, ATTN_KEY_CHUNK)
    tqt = _pick_tile(S, tqt)
    use_bm = block_mask is not None
    qspec = pl.BlockSpec((1, n_rep, tqt, d), lambda b, g, i: (b, g, i, 0))
    kvspec = pl.BlockSpec((1, 1, S, d), lambda b, g, i: (b, g, 0, 0))
    in_specs = [qspec, kvspec, kvspec]
    args = [q, k, v]
    if use_bm:
        in_specs += [pl.BlockSpec((1, 1, tqt, LANES), lambda b, g, i: (b, g, i, 0)),
                     pl.BlockSpec((S // kc, LANES, kc), lambda b, g, i: (0, 0, 0))]
        args += [block_mask, _block_expander(S, kc, block)]
    return pl.pallas_call(
        functools.partial(_attn_kernel, tqt=tqt, n_rep=n_rep, kc=kc, window=window, use_bm=use_bm, scale=scale),
        grid=(B, G, S // tqt),
        in_specs=in_specs,
        out_specs=qspec,
        out_shape=jax.ShapeDtypeStruct((B, HQ, S, d), F32),
        compiler_params=pltpu.CompilerParams(dimension_semantics=("parallel", "parallel", "parallel"),
                                             vmem_limit_bytes=V7X_VMEM_LIMIT_BYTES),
        name="block_masked_attention",
    )(*args)


def _nsa_cmp_kernel(q_ref, kc_ref, vc_ref, ov_ref, o_ref, imp_ref, *, tqt, n_rep, ncmp, scale):
    t0 = pl.program_id(1) * tqt
    d = q_ref.shape[-1]
    ncp = kc_ref.shape[1]
    rows = n_rep * tqt
    q = q_ref[0].reshape(rows, d)
    s = lax.dot_general(q, kc_ref[0], (((1,), (1,)), ((), ())), preferred_element_type=F32) * scale
    n_io = lax.broadcasted_iota(jnp.int32, (tqt, ncp), 1)
    t_io = t0 + lax.broadcasted_iota(jnp.int32, (tqt, ncp), 0)
    valid = (n_io * NSA_CMP_STRIDE + (NSA_CMP_BLOCK - 1) <= t_io) & (n_io < ncmp)
    bias = jnp.where(valid, 0.0, MASKED_SCORE)
    s = s.reshape(n_rep, tqt, ncp) + bias[None]
    m = jnp.maximum(jnp.max(s, axis=-1, keepdims=True), RUNNING_MAX_FLOOR)
    e = jnp.exp(s - m)
    l = jnp.sum(e, axis=-1, keepdims=True)
    p = e / jnp.where(l > 0.0, l, 1.0)
    o = jnp.dot(p.reshape(rows, ncp).astype(BF16), vc_ref[0], preferred_element_type=F32)
    o_ref[0] = o.reshape(n_rep, tqt, d)
    imp_ref[0] = jnp.dot(jnp.sum(p, axis=0), ov_ref[...], preferred_element_type=F32, precision=HIGHEST)


def _nsa_cmp_attention(q, k_cmp, v_cmp, overlap, ncmp, scale, tqt=128):
    B, H, S, d = q.shape
    ncp = k_cmp.shape[1]
    tqt = _pick_tile(S, tqt)
    qspec = pl.BlockSpec((1, H, tqt, d), lambda b, i: (b, 0, i, 0))
    cspec = pl.BlockSpec((1, ncp, d), lambda b, i: (b, 0, 0))
    return pl.pallas_call(
        functools.partial(_nsa_cmp_kernel, tqt=tqt, n_rep=H, ncmp=ncmp, scale=scale),
        grid=(B, S // tqt),
        in_specs=[qspec, cspec, cspec, pl.BlockSpec((ncp, LANES), lambda b, i: (0, 0))],
        out_specs=[qspec, pl.BlockSpec((1, tqt, LANES), lambda b, i: (b, i, 0))],
        out_shape=[jax.ShapeDtypeStruct((B, H, S, d), F32), jax.ShapeDtypeStruct((B, S, LANES), F32)],
        compiler_params=pltpu.CompilerParams(dimension_semantics=("parallel", "parallel"),
                                             vmem_limit_bytes=V7X_VMEM_LIMIT_BYTES),
        name="nsa_compressed_attention",
    )(q, k_cmp, v_cmp, overlap)


RWKV_TIME_CHUNK = 32
RWKV_ROWS = RWKV_HEAD_DIM // 2
RWKV_ROW_GROUPS = RWKV_ROWS // SUBLANES
RWKV_KEY_TILES = RWKV_HEAD_DIM // SUBLANES


def _rwkv_scan_kernel(r_ref, w_ref, k_ref, kk_ref, b_ref, v_ref, y_ref, st_ref):
    @pl.when(pl.program_id(0) == 0)
    def _():
        st_ref[...] = jnp.zeros_like(st_ref)

    n_t = r_ref.shape[0]
    sub_iota = lax.broadcasted_iota(jnp.int32, (SUBLANES, st_ref.shape[-1]), 0)

    def step(t, carry):
        r = r_ref[t]
        w = w_ref[t]
        k = k_ref[t]
        kk = kk_ref[t]
        bb = b_ref[t]

        def group(g, c2):
            y_tile = jnp.zeros(sub_iota.shape, F32)
            for ii in range(SUBLANES):
                i = g * SUBLANES + ii
                s = st_ref[i]
                sa = jnp.sum(jnp.sum(s * kk, axis=0), axis=0, keepdims=True)
                vrow = v_ref[t, g, pl.ds(ii, 1), :]
                s = s * w - sa[None] * bb + vrow[None] * k
                st_ref[i] = s
                yrow = jnp.sum(jnp.sum(s * r, axis=0), axis=0, keepdims=True)
                y_tile = jnp.where(sub_iota == ii, jnp.broadcast_to(yrow, sub_iota.shape), y_tile)
            y_ref[t, g] = y_tile
            return c2

        lax.fori_loop(0, RWKV_ROW_GROUPS, group, 0)
        return carry

    lax.fori_loop(0, n_t, step, 0)


def _rwkv_scan(r, w, k, v, kk, bvec):
    B, S, H, N = r.shape
    chains = B * H
    lanes = 2 * chains

    def key_layout(z):
        zt = jnp.transpose(z, (1, 3, 0, 2)).reshape(S, N, chains)
        zt = jnp.concatenate([zt, zt], axis=-1)
        return zt.reshape(S, RWKV_KEY_TILES, SUBLANES, lanes)

    def val_layout(z):
        zt = jnp.transpose(z.reshape(B, S, H, 2, RWKV_ROWS), (1, 4, 3, 0, 2))
        return zt.reshape(S, RWKV_ROW_GROUPS, SUBLANES, lanes)

    tc = _pick_tile(S, RWKV_TIME_CHUNK)
    kspec = pl.BlockSpec((tc, RWKV_KEY_TILES, SUBLANES, lanes), lambda c: (c, 0, 0, 0))
    vspec = pl.BlockSpec((tc, RWKV_ROW_GROUPS, SUBLANES, lanes), lambda c: (c, 0, 0, 0))
    y = pl.pallas_call(
        _rwkv_scan_kernel,
        grid=(S // tc,),
        in_specs=[kspec, kspec, kspec, kspec, kspec, vspec],
        out_specs=vspec,
        out_shape=jax.ShapeDtypeStruct((S, RWKV_ROW_GROUPS, SUBLANES, lanes), F32),
        scratch_shapes=[pltpu.VMEM((RWKV_ROWS, RWKV_KEY_TILES, SUBLANES, lanes), F32)],
        compiler_params=pltpu.CompilerParams(dimension_semantics=("arbitrary",),
                                             vmem_limit_bytes=V7X_VMEM_LIMIT_BYTES),
        name="rwkv7_scan",
    )(key_layout(r), key_layout(w), key_layout(k), key_layout(kk), key_layout(bvec), val_layout(v))
    y = y.reshape(S, RWKV_ROWS, 2, B, H)
    return jnp.transpose(y, (3, 0, 4, 2, 1)).reshape(B, S, H, N)


ROUTE_TOKENS = 512


def _first_index_of_max(vals, index, n, axes):
    m = vals
    for ax in axes:
        m = jnp.max(m, axis=ax, keepdims=True)
    idx = jnp.where(vals == m, index, float(n))
    for ax in axes:
        idx = jnp.min(idx, axis=ax, keepdims=True)
    return m, idx


def _route_kernel(lg_ref, bias_ref, upper_ref, eidx_ref, wts_ref, rank_ref, cnt_ref, carry_ref):
    @pl.when(pl.program_id(0) == 0)
    def _():
        carry_ref[...] = jnp.zeros_like(carry_ref)

    E, tn = lg_ref.shape
    G = N_GROUPS
    J = E // G
    s3 = jax.nn.sigmoid(lg_ref[...]).reshape(G, J, tn)
    b3 = s3 + bias_ref[...].reshape(G, J, 1)
    j_io = lax.broadcasted_iota(jnp.int32, (G, J, tn), 1).astype(F32)
    g_io = lax.broadcasted_iota(jnp.int32, (G, 1, tn), 0).astype(F32)
    e_io = lax.broadcasted_iota(jnp.int32, (G, J, tn), 0).astype(F32) * J + j_io

    m1, first = _first_index_of_max(b3, j_io, J, (1,))
    m2 = jnp.max(jnp.where(j_io == first, -jnp.inf, b3), axis=1, keepdims=True)
    cur = m1 + m2
    keep = jnp.zeros((G, 1, tn), F32)
    for _ in range(TOPK_GROUPS):
        _, idx = _first_index_of_max(cur, g_io, G, (0,))
        hit = g_io == idx
        keep = jnp.where(hit, 1.0, keep)
        cur = jnp.where(hit, -jnp.inf, cur)

    cur = jnp.where(keep > 0.5, b3, NEG)
    hot = jnp.zeros((G, J, tn), F32)
    picks, weights = [], []
    for _ in range(TOP_K):
        _, idx = _first_index_of_max(cur, e_io, E, (0, 1))
        hit = e_io == idx
        w = jnp.sum(jnp.sum(jnp.where(hit, s3, 0.0), axis=0, keepdims=True), axis=1, keepdims=True)
        cur = jnp.where(hit, -jnp.inf, cur)
        hot = jnp.where(hit, 1.0, hot)
        picks.append(idx)
        weights.append(w)
    wsum = weights[0]
    for w in weights[1:]:
        wsum = wsum + w

    hot2 = hot.reshape(E, tn)
    before = jnp.dot(hot2.astype(BF16), upper_ref[...], preferred_element_type=F32) + carry_ref[:, 0:1]
    before3 = before.reshape(G, J, tn)
    for k in range(TOP_K):
        eidx_ref[k:k + 1, :] = picks[k].reshape(1, tn).astype(jnp.int32)
        wts_ref[k:k + 1, :] = (weights[k] / wsum * ROUTED_SCALE).reshape(1, tn)
        r = jnp.sum(jnp.sum(jnp.where(e_io == picks[k], before3, 0.0), axis=0, keepdims=True), axis=1, keepdims=True)
        rank_ref[k:k + 1, :] = r.reshape(1, tn).astype(jnp.int32)
    carry_ref[...] = carry_ref[...] + jnp.sum(hot2, axis=1, keepdims=True)
    cnt_ref[...] = carry_ref[...]


def _route(logits_t, bias):
    E, T = logits_t.shape
    tn = _pick_tile(T, ROUTE_TOKENS)
    upper = jnp.asarray(np.triu(np.ones((tn, tn), np.float32), 1), BF16)
    tok_spec = pl.BlockSpec((TOP_K, tn), lambda i: (0, i))
    eidx, wts, rank, cnt = pl.pallas_call(
        _route_kernel,
        grid=(T // tn,),
        in_specs=[pl.BlockSpec((E, tn), lambda i: (0, i)),
                  pl.BlockSpec((E, 1), lambda i: (0, 0)),
                  pl.BlockSpec((tn, tn), lambda i: (0, 0))],
        out_specs=[tok_spec, tok_spec, tok_spec, pl.BlockSpec((E, LANES), lambda i: (0, 0))],
        out_shape=[jax.ShapeDtypeStruct((TOP_K, T), jnp.int32), jax.ShapeDtypeStruct((TOP_K, T), F32),
                   jax.ShapeDtypeStruct((TOP_K, T), jnp.int32), jax.ShapeDtypeStruct((E, LANES), F32)],
        scratch_shapes=[pltpu.VMEM((E, LANES), F32)],
        compiler_params=pltpu.CompilerParams(dimension_semantics=("arbitrary",),
                                             vmem_limit_bytes=V7X_VMEM_LIMIT_BYTES),
        name="moe_route",
    )(logits_t, bias.reshape(E, 1), upper)
    return eidx, wts, rank, cnt[:, 0].astype(jnp.int32)


def _moe_kernel(be_ref, nused_ref, x_ref, wg_ref, wu_ref, wd_ref, o_ref, wg_s, wu_s, wd_s):
    i = pl.program_id(0)
    active = i < nused_ref[0]

    @pl.when(active & ((i == 0) | (be_ref[i] != be_ref[jnp.maximum(i - 1, 0)])))
    def _():
        wg_s[...] = wg_ref[0].astype(BF16)
        wu_s[...] = wu_ref[0].astype(BF16)
        wd_s[...] = wd_ref[0].astype(BF16)

    @pl.when(active)
    def _():
        x = x_ref[...]
        g = jnp.dot(x, wg_s[...], preferred_element_type=F32)
        u = jnp.dot(x, wu_s[...], preferred_element_type=F32)
        hmid = (g * jax.nn.sigmoid(g) * u).astype(BF16)
        y = jnp.dot(hmid, wd_s[...], preferred_element_type=F32)
        half = y.shape[1] // 2
        lo = pltpu.bitcast(y[:, :half].astype(BF16).astype(F32), jnp.uint32) >> 16
        hi = pltpu.bitcast(y[:, half:].astype(BF16).astype(F32), jnp.uint32) & jnp.uint32(0xFFFF0000)
        o_ref[...] = hi | lo

    @pl.when(jnp.logical_not(active))
    def _():
        o_ref[...] = jnp.zeros_like(o_ref)


def _moe_grouped(xg, blk_e, n_used, w_gate, w_up, w_down, layer):
    R, D = xg.shape
    n_blk = R // MOE_BLOCK
    n_layers, E, _, F = w_gate.shape
    w_gate, w_up, w_down = (w.reshape((n_layers * E,) + w.shape[2:]) for w in (w_gate, w_up, w_down))

    def xmap(i, be, nu):
        return (jnp.minimum(i, jnp.maximum(nu[0] - 1, 0)), 0)

    def wmap(i, be, nu):
        return (layer * E + be[i], 0, 0)

    grid_spec = pltpu.PrefetchScalarGridSpec(
        num_scalar_prefetch=2,
        grid=(n_blk,),
        in_specs=[pl.BlockSpec((MOE_BLOCK, D), xmap),
                  pl.BlockSpec((1, D, F), wmap),
                  pl.BlockSpec((1, D, F), wmap),
                  pl.BlockSpec((1, F, D), wmap)],
        out_specs=pl.BlockSpec((MOE_BLOCK, D // 2), lambda i, be, nu: (i, 0)),
        scratch_shapes=[pltpu.VMEM((D, F), BF16), pltpu.VMEM((D, F), BF16), pltpu.VMEM((F, D), BF16)],
    )
    return pl.pallas_call(
        _moe_kernel,
        grid_spec=grid_spec,
        out_shape=jax.ShapeDtypeStruct((R, D // 2), jnp.uint32),
        compiler_params=pltpu.CompilerParams(dimension_semantics=("arbitrary",),
                                             vmem_limit_bytes=V7X_VMEM_LIMIT_BYTES),
        name="moe_grouped_swiglu",
    )(blk_e, n_used, xg, w_gate, w_up, w_down)


MOE_COMBINE_ROWS = 256


def _moe_combine_kernel(a_ref, b_ref, res_ref, gate_ref, mine_ref, w_ref, o_ref):
    half = o_ref.shape[1] // 2
    y = jnp.dot(a_ref[...], b_ref[...], preferred_element_type=F32)
    lo_acc = y[:, :half]
    hi_acc = y[:, half:]
    for k in range(mine_ref.shape[0]):
        word = mine_ref[k]
        wk = w_ref[:, k:k + 1]
        lo_acc = lo_acc + wk * pltpu.bitcast(word << 16, F32)
        hi_acc = hi_acc + wk * pltpu.bitcast(word & jnp.uint32(0xFFFF0000), F32)
    g = gate_ref[0]
    o_ref[:, :half] = res_ref[:, :half] + g[:, :half] * lo_acc
    o_ref[:, half:] = res_ref[:, half:] + g[:, half:] * hi_acc


def _moe_combine(hmid, s_down, x_res, gate, seq_len, mine, wts):
    M, F = hmid.shape
    N = s_down.shape[1]
    n_k = mine.shape[0]
    tm = _pick_tile(seq_len, MOE_COMBINE_ROWS)
    row_tile = pl.BlockSpec((tm, N), lambda i: (i, 0))
    return pl.pallas_call(
        _moe_combine_kernel,
        grid=(M // tm,),
        in_specs=[pl.BlockSpec((tm, F), lambda i: (i, 0)),
                  pl.BlockSpec((F, N), lambda i: (0, 0)),
                  row_tile,
                  pl.BlockSpec((1, 1, N), lambda i: ((i * tm) // seq_len, 0, 0)),
                  pl.BlockSpec((n_k, tm, N // 2), lambda i: (0, i, 0)),
                  pl.BlockSpec((tm, n_k), lambda i: (i, 0))],
        out_specs=row_tile,
        out_shape=jax.ShapeDtypeStruct((M, N), F32),
        compiler_params=pltpu.CompilerParams(dimension_semantics=("parallel",),
                                             vmem_limit_bytes=V7X_VMEM_LIMIT_BYTES),
        name="moe_combine_residual",
    )(hmid, s_down, x_res, gate, mine, wts)


def _split(z, widths):
    cuts = [int(i) for i in np.cumsum(widths)[:-1]]
    return jnp.split(z, cuts, axis=-1)


def _rms(z, g, eps=NORM_EPS):
    return z * lax.rsqrt(jnp.mean(z * z, axis=-1, keepdims=True) + eps) * g


def _heads(z, n_heads):
    B, S, C = z.shape
    return z.reshape(B, S, n_heads, C // n_heads).transpose(0, 2, 1, 3)


def _merge_heads(z):
    B, H, S, d = z.shape
    return z.transpose(0, 2, 1, 3).reshape(B, S, H * d)


def _rope(z, pos):
    half = ROPE_DIM // 2
    inv_freq = ROPE_THETA ** (-jnp.arange(half, dtype=F32) / half)
    ang = pos.astype(F32)[:, None] * inv_freq[None, :]
    cos = jnp.cos(ang)
    sin = jnp.sin(ang)
    z1, z2, zr = z[..., :half], z[..., half:ROPE_DIM], z[..., ROPE_DIM:]
    return jnp.concatenate([z1 * cos - z2 * sin, z2 * cos + z1 * sin, zr], axis=-1)


def _token_shift(z):
    return jnp.pad(z, ((0, 0), (1, 0), (0, 0)))[:, :-1]


def _rwkv7(p, v_first, v_mix, mu, w0, w_up, a0, a_up, g_up, k_k, k_a, r_k, ln_w, ln_b):
    B, S, _ = p.shape
    H, N = RWKV_HEADS, RWKV_HEAD_DIM
    xs = p + (_token_shift(p) - p) * mu
    r, k, v, xw, xa, xg = _split(xs, (RWKV_DIM, RWKV_DIM, RWKV_DIM, RWKV_DECAY_LORA, RWKV_ICLR_LORA, RWKV_GATE_LORA))
    w = -jax.nn.softplus(-(w0 + jnp.tanh(xw) @ w_up)) - 0.5
    decay = jnp.exp(-jnp.exp(w))
    a = jax.nn.sigmoid(a0 + xa @ a_up)
    g = jax.nn.sigmoid(xg) @ g_up
    kk = (k * k_k).reshape(B, S, H, N)
    kk = kk / jnp.maximum(jnp.sqrt(jnp.sum(kk * kk, axis=-1, keepdims=True)), 1e-12)
    k = k * (1.0 + (a - 1.0) * k_a)
    if v_mix is None:
        v_first = v
    else:
        v0, v_down, v_up = v_mix
        v = v + (v_first - v) * jax.nn.sigmoid(v0 + (v @ v_down) @ v_up)

    def hd(z):
        return z.reshape(B, S, H, N)

    y = _rwkv_scan(hd(r), hd(decay), hd(k), hd(v), kk, kk * hd(a))
    mean = jnp.mean(y, axis=-1, keepdims=True)
    var = jnp.mean(jnp.square(y - mean), axis=-1, keepdims=True)
    y = ((y - mean) * lax.rsqrt(var + RWKV_GN_EPS)).reshape(B, S, RWKV_DIM) * ln_w + ln_b
    bonus = jnp.sum((r * k * r_k).reshape(B, S, H, N), axis=-1, keepdims=True) * v.reshape(B, S, H, N)
    y = (y + bonus.reshape(B, S, RWKV_DIM)) * g
    return y, v_first


def _mlstm_chunkwise(q, k, v, i_pre, logf):
    B, H, S, dk = q.shape
    dv = v.shape[-1]
    L = MLSTM_CHUNK
    NC = S // L
    q = q.reshape(B, H, NC, L, dk)
    k = k.reshape(B, H, NC, L, dk)
    v = v.reshape(B, H, NC, L, dv)
    ic = i_pre.reshape(B, H, NC, L)
    bcum = jnp.cumsum(logf.reshape(B, H, NC, L), axis=-1)
    gtot = bcum[..., -1]
    a_log = gtot[..., None] - bcum + ic
    a_max = jnp.max(a_log, axis=-1)

    def step(carry, inp):
        C, n, m = carry
        g_c, a_c, amax_c, k_c, v_c = inp
        m_new = jnp.maximum(g_c + m, amax_c)
        wgt = jnp.exp(a_c - m_new[..., None])
        dec = jnp.exp(g_c + m - m_new)
        C_new = dec[..., None, None] * C + jnp.einsum('bhl,bhlk,bhlv->bhkv', wgt, k_c, v_c)
        n_new = dec[..., None] * n + jnp.einsum('bhl,bhlk->bhk', wgt, k_c)
        return (C_new, n_new, m_new), (C, n, m)

    init = (jnp.zeros((B, H, dk, dv), F32), jnp.zeros((B, H, dk), F32), jnp.zeros((B, H), F32))
    xs = (jnp.moveaxis(gtot, 2, 0), jnp.moveaxis(a_log, 2, 0), jnp.moveaxis(a_max, 2, 0),
          jnp.moveaxis(k, 2, 0), jnp.moveaxis(v, 2, 0))
    _, (C_prev, n_prev, m_prev) = lax.scan(step, init, xs)
    C_prev = jnp.moveaxis(C_prev, 0, 2)
    n_prev = jnp.moveaxis(n_prev, 0, 2)
    m_prev = jnp.moveaxis(m_prev, 0, 2)

    causal = jnp.tril(jnp.ones((L, L), dtype=bool))
    logD = jnp.where(causal, bcum[..., :, None] - bcum[..., None, :] + ic[..., None, :], -jnp.inf)
    m_inter = bcum + m_prev[..., None]
    m_t = jnp.maximum(m_inter, jnp.max(logD, axis=-1))
    Dm = jnp.exp(logD - m_t[..., None])
    sqk = jnp.einsum('bhctd,bhcsd->bhcts', q, k) * Dm
    inter = jnp.exp(m_inter - m_t)
    num = jnp.einsum('bhcts,bhcsv->bhctv', sqk, v) + inter[..., None] * jnp.einsum('bhctk,bhckv->bhctv', q, C_prev)
    den = jnp.sum(sqk, axis=-1) + inter * jnp.einsum('bhctk,bhck->bhct', q, n_prev)
    h = num / jnp.maximum(jnp.abs(den), jnp.exp(-m_t))[..., None]
    return h.reshape(B, H, S, dv)


def _mlstm(p, p_gates, conv_w, conv_b, i_b, f_b, norm_g):
    H = MLSTM_HEADS
    qk, v, o = _split(p, (2 * MLSTM_QK_WIDTH, MLSTM_DIM, MLSTM_DIM))
    ig, fg = _split(p_gates, (H, H))
    S = qk.shape[1]
    qkp = jnp.pad(qk, ((0, 0), (MLSTM_CONV - 1, 0), (0, 0)))
    conv = conv_b
    for j in range(MLSTM_CONV):
        conv = conv + qkp[:, j:j + S] * conv_w[j]
    qk = jax.nn.silu(conv)
    q, k = jnp.split(qk, 2, axis=-1)
    q = _heads(q, H) * (MLSTM_QK_DIM ** -0.5)
    k = _heads(k, H)
    v = _heads(v, H)
    i_pre = jnp.moveaxis(ig + i_b, -1, 1)
    logf = jax.nn.log_sigmoid(jnp.moveaxis(fg + f_b, -1, 1))
    h = _mlstm_chunkwise(q, k, v, i_pre, logf)
    h = _rms(h, norm_g.reshape(H, 1, MLSTM_V_DIM))
    return jax.nn.sigmoid(o) * _merge_heads(h)


def _moba(q, k, v):
    B, H, S, d = q.shape
    BLK = MOBA_BLOCK
    nb = S // BLK
    kb = k.reshape(B, H, nb, BLK, d)
    pos = jnp.arange(S)
    qblk = pos // BLK
    gate = jnp.einsum('bhsd,bhnd->bhsn', q, jnp.mean(kb, axis=3), precision=HIGHEST)
    gate = jnp.where(jnp.arange(nb)[None, :] < qblk[:, None], gate, NEG)
    n_top = min(MOBA_TOPK, nb)
    _, sel = lax.top_k(gate, n_top)
    slot_ok = jnp.arange(n_top)[None, :] < qblk[:, None]
    lanes = jnp.arange(LANES)
    hit = (sel[..., None] == lanes) & slot_ok[:, :, None]
    blk_mask = jnp.any(hit, axis=3) | (lanes[None, :] == qblk[:, None])
    return _attention(q.astype(BF16), k.astype(BF16), v.astype(BF16), MOBA_BLOCK, d ** -0.5,
                      block_mask=blk_mask.astype(BF16), block=BLK)


def _nsa(p, gl, pos, q_norm, k_norm, cmp_pe, cmp_w1, cmp_b1, cmp_w2):
    B, S, _ = p.shape
    H, d = NSA_HEADS, ATTN_HEAD_DIM
    scale = d ** -0.5
    q, kc_in, vc_in, ks_in, vs_in, kw_in, vw_in = _split(p, (NSA_DIM, d, d, d, d, d, d))
    q = _rope(_rms(_heads(q, H), q_norm), pos)

    ncmp = (S - NSA_CMP_BLOCK) // NSA_CMP_STRIDE + 1
    cstart = NSA_CMP_STRIDE * jnp.arange(ncmp)
    cend = cstart + NSA_CMP_BLOCK - 1
    cidx = cstart[:, None] + jnp.arange(NSA_CMP_BLOCK)[None, :]

    def compress(z, i):
        blocks = (z[:, cidx] + cmp_pe[i]).reshape(B, ncmp, NSA_CMP_BLOCK * d)
        return jax.nn.gelu(blocks @ cmp_w1[i] + cmp_b1[i]) @ cmp_w2[i]

    k_cmp = _rope(_rms(compress(kc_in, 0), k_norm[0]), cend)
    v_cmp = compress(vc_in, 1)
    SB = NSA_SLC_BLOCK
    nsb = S // SB
    assert nsb <= LANES
    ncp = -(-ncmp // LANES) * LANES
    cs_np = NSA_CMP_STRIDE * np.arange(ncp)
    ss_np = SB * np.arange(LANES)
    overlap = ((cs_np[:, None] <= ss_np[None, :] + SB - 1) & (cs_np[:, None] + NSA_CMP_BLOCK - 1 >= ss_np[None, :])
               & (np.arange(ncp)[:, None] < ncmp) & (np.arange(LANES)[None, :] < nsb))
    qb = q.astype(BF16)
    cpad = ((0, 0), (0, ncp - ncmp), (0, 0))
    o_cmp, imp = _nsa_cmp_attention(qb, jnp.pad(k_cmp, cpad).astype(BF16), jnp.pad(v_cmp, cpad).astype(BF16),
                                    jnp.asarray(overlap, F32), ncmp, scale)
    imp = imp[:, :, :nsb]
    cur = pos // SB
    blk = jnp.arange(nsb)[None, :]
    forced = (blk == 0) | (blk == cur[:, None]) | (blk == cur[:, None] - 1)
    imp = jnp.where(forced, BIG, jnp.where(blk <= cur[:, None], imp, NEG))
    n_sel = min(NSA_SLC_TOPK, nsb)
    _, sel = lax.top_k(imp, n_sel)
    blk_mask = jnp.any(sel[..., None] == jnp.arange(LANES), axis=2)
    k_slc = _rope(_rms(ks_in, k_norm[1]), pos)
    o_slc = _attention(qb, k_slc[:, None].astype(BF16), vs_in[:, None].astype(BF16), NSA_Q_TILE, scale,
                       block_mask=blk_mask[:, None].astype(BF16), block=SB)

    k_win = _rope(_rms(kw_in, k_norm[2]), pos)
    o_win = _attention(qb, k_win[:, None].astype(BF16), vw_in[:, None].astype(BF16), NSA_Q_TILE, scale,
                       window=NSA_WINDOW)

    gates = jax.nn.sigmoid(gl).reshape(B, S, H, 3).transpose(0, 2, 1, 3)
    o = gates[..., 0:1] * o_cmp + gates[..., 1:2] * o_slc + gates[..., 2:3] * o_win
    return _merge_heads(o)


def _moe(xb, logits, x_res, gate, seq_len, bias, w_gate, w_up, w_down, layer, s_gate, s_up, s_down):
    T, D = xb.shape
    A = T * TOP_K
    eidx, wts, rank, counts = _route(logits, bias)

    experts = jnp.arange(N_EXPERTS, dtype=jnp.int32)
    padded = (counts + MOE_BLOCK - 1) // MOE_BLOCK * MOE_BLOCK
    pstart = jnp.cumsum(padded) - padded
    slot = rank + jnp.sum(jnp.where(eidx[None] == experts[:, None, None], pstart[:, None, None], 0), axis=0)

    n_blk = -(-(A + N_EXPERTS * (MOE_BLOCK - 1)) // MOE_BLOCK)
    R = n_blk * MOE_BLOCK
    blk_e = jnp.minimum(jnp.sum((pstart + padded)[None, :] <= (jnp.arange(n_blk) * MOE_BLOCK)[:, None], axis=1),
                        N_EXPERTS - 1).astype(jnp.int32)
    n_used = (jnp.sum(padded) // MOE_BLOCK).astype(jnp.int32).reshape(1)
    tok_bits = max(int(T - 1).bit_length(), 1)
    tok_ids = jnp.arange(T, dtype=jnp.int32)[None, :]
    real_keys = (eidx << (tok_bits + 1)) | tok_ids
    n_fill = MOE_BLOCK - 1
    fill_j = jnp.arange(n_fill, dtype=jnp.int32)[None, :]
    unused = jnp.int32(N_EXPERTS << (tok_bits + 1))
    fill_keys = jnp.where(fill_j < (padded - counts)[:, None], (experts[:, None] << (tok_bits + 1)) | (1 << tok_bits), unused)
    tail = jnp.full((R - A - N_EXPERTS * n_fill,), unused, jnp.int32)
    keys = lax.sort(jnp.concatenate([real_keys.reshape(A), fill_keys.reshape(-1), tail]))
    tok_buf = jnp.where((keys >> tok_bits) & 1 == 1, 0, keys & ((1 << tok_bits) - 1))
    tok_buf = jnp.where(keys >= unused, 0, tok_buf)

    packed = _moe_grouped(xb[tok_buf], blk_e, n_used, w_gate, w_up, w_down, layer)
    mine = packed[slot.reshape(A)].reshape(TOP_K, T, D // 2)

    gu = _matmul(xb, jnp.concatenate([s_gate, s_up], axis=1).astype(BF16))
    F = s_gate.shape[1]
    hmid = (jax.nn.silu(gu[:, :F]) * gu[:, F:]).astype(BF16)
    return _moe_combine(hmid, s_down.astype(BF16), x_res, gate, seq_len, mine, wts.T)


def _projection_layout():
    o_b = RWKV_IN_DIM
    o_c = o_b + MLSTM_IN_DIM
    o_d = o_c + MOBA_IN_DIM
    o_g = o_d + NSA_IN_DIM
    mlstm_main = MLSTM_IN_DIM - 2 * MLSTM_HEADS
    nsa_main = NSA_IN_DIM - 3 * NSA_HEADS
    spans = (('rwkv', 0, o_b), ('mlstm', o_b, o_b + mlstm_main), ('moba', o_c, o_d), ('nsa', o_d, o_d + nsa_main),
             ('mlstm_gates', o_b + mlstm_main, o_c), ('nsa_gates', o_d + nsa_main, o_g))
    pieces, cur = [], 0
    for name, lo, hi in spans:
        pieces.append((name, lo, hi, cur))
        cur += -(-(hi - lo) // LANES) * LANES
    return tuple(pieces), cur, o_g


def _relayout_kernel(w_ref, mix_ref, gates_ref, *, pieces, gate_lo):
    for _, lo, hi, off in pieces:
        width = hi - lo
        mix_ref[:, off:off + width] = w_ref[0, :, lo:hi].astype(BF16)
        pad = -width % LANES
        if pad:
            mix_ref[:, off + width:off + width + pad] = jnp.zeros((mix_ref.shape[0], pad), BF16)
    gates_ref[...] = w_ref[0, :, gate_lo:].astype(BF16)


def _projection_weights(w_in, layer, rows=128):
    _, D, n_in = w_in.shape
    pieces, mix_width, gate_lo = _projection_layout()
    w_mix, w_gates = pl.pallas_call(
        functools.partial(_relayout_kernel, pieces=pieces, gate_lo=gate_lo),
        grid=(D // rows,),
        in_specs=[pl.BlockSpec((1, rows, n_in), lambda i: (layer, i, 0))],
        out_specs=[pl.BlockSpec((rows, mix_width), lambda i: (i, 0)),
                   pl.BlockSpec((rows, n_in - gate_lo), lambda i: (i, 0))],
        out_shape=[jax.ShapeDtypeStruct((D, mix_width), BF16), jax.ShapeDtypeStruct((D, n_in - gate_lo), BF16)],
        compiler_params=pltpu.CompilerParams(dimension_semantics=("parallel",),
                                             vmem_limit_bytes=V7X_VMEM_LIMIT_BYTES),
        name="projection_weight_relayout",
    )(w_in)
    return w_mix, w_gates, {name: off for name, _, _, off in pieces}


def kernel(x, c, ada_w, ada_b, norm_mix_g, norm_ffn_g, w_in, rwkv_mu, rwkv_w0, rwkv_w_up, rwkv_a0, rwkv_a_up, rwkv_g_up, rwkv_k_k, rwkv_k_a, rwkv_r_k, rwkv_ln_w, rwkv_ln_b, rwkv_v0, rwkv_v_down, rwkv_v_up, mlstm_conv_w, mlstm_conv_b, mlstm_i_b, mlstm_f_b, mlstm_norm_g, moba_q_norm, moba_k_norm, nsa_q_norm, nsa_k_norm, nsa_cmp_pe, nsa_cmp_w1, nsa_cmp_b1, nsa_cmp_w2, w_branch, w_out, moe_router, moe_bias, moe_w_gate, moe_w_up, moe_w_down, shared_w_gate, shared_w_up, shared_w_down):
    B, S, D = x.shape
    T = B * S
    depth = ada_w.shape[0]
    pos = jnp.arange(S)
    cond = jax.nn.silu(c)
    v_first = None
    x2d = x.reshape(T, D)
    for l in range(depth):
        mod = jnp.dot(cond, ada_w[l], precision=HIGHEST) + ada_b[l]
        sh_mix, sc_mix, gate_mix, sh_ffn, sc_ffn, gate_ffn = [z[:, None, :] for z in jnp.split(mod, 6, axis=-1)]

        hb = _normmod(x2d, norm_mix_g[l], sc_mix, sh_mix, S)
        w_mix, w_gates, offs = _projection_weights(w_in, l)
        p_gate = _matmul(hb, w_gates, out_dtype=BF16)

        def proj(name, width, dtype=F32, n_pieces=1):
            padded = -(-width // LANES) * LANES * n_pieces
            return _matmul(hb, w_mix, out_dtype=dtype, cols=(offs[name], padded)).reshape(B, S, padded)

        small = proj('mlstm_gates', 2 * MLSTM_HEADS, n_pieces=2)
        nsa_gate_off = offs['nsa_gates'] - offs['mlstm_gates']
        v_mix = None if l == 0 else (rwkv_v0[l - 1], rwkv_v_down[l - 1], rwkv_v_up[l - 1])
        y_a, v_first = _rwkv7(proj('rwkv', RWKV_IN_DIM)[:, :, :RWKV_IN_DIM], v_first, v_mix, rwkv_mu[l], rwkv_w0[l],
                              rwkv_w_up[l], rwkv_a0[l], rwkv_a_up[l], rwkv_g_up[l], rwkv_k_k[l], rwkv_k_a[l],
                              rwkv_r_k[l], rwkv_ln_w[l], rwkv_ln_b[l])
        y_b = _mlstm(proj('mlstm', MLSTM_IN_DIM - 2 * MLSTM_HEADS), small[:, :, :2 * MLSTM_HEADS],
                     mlstm_conv_w[l], mlstm_conv_b[l], mlstm_i_b[l], mlstm_f_b[l], mlstm_norm_g[l])
        q_c, k_c, v_c = _split(proj('moba', MOBA_IN_DIM, BF16), (MOBA_DIM, MOBA_DIM, MOBA_DIM))
        q_c = _rope(_rms(_heads(q_c, MOBA_HEADS).astype(F32), moba_q_norm[l]), pos)
        k_c = _rope(_rms(_heads(k_c, MOBA_HEADS).astype(F32), moba_k_norm[l]), pos)
        y_c = _merge_heads(_moba(q_c, k_c, _heads(v_c, MOBA_HEADS)))
        y_d = _nsa(proj('nsa', NSA_IN_DIM - 3 * NSA_HEADS, BF16).astype(F32),
                   small[:, :, nsa_gate_off:nsa_gate_off + 3 * NSA_HEADS], pos, nsa_q_norm[l],
                   nsa_k_norm[l], nsa_cmp_pe[l], nsa_cmp_w1[l], nsa_cmp_b1[l], nsa_cmp_w2[l])
        ys = [y.reshape(T, BRANCH_DIM).astype(BF16) for y in (y_a, y_b, y_c, y_d)]
        merged = _merge_branches(p_gate, ys, w_branch[l].astype(BF16))
        x2d = _matmul_residual(merged, w_out[l].astype(BF16), x2d, gate_mix, S)

        hb, logits = _normmod(x2d, norm_ffn_g[l], sc_ffn, sh_ffn, S, router=moe_router[l])
        x2d = _moe(hb, logits, x2d, gate_ffn, S, moe_bias[l], moe_w_gate, moe_w_up, moe_w_down, l,
                   shared_w_gate[l], shared_w_up[l], shared_w_down[l])
    return x2d.reshape(B, S, D)
```

```python
import functools

import numpy as np
import jax
import jax.numpy as jnp
from jax import lax
from jax.experimental import pallas as pl
from jax.experimental.pallas import tpu as pltpu

F32 = jnp.float32
BF16 = jnp.bfloat16
HIGHEST = lax.Precision.HIGHEST

D_MODEL = 2048
N_MIXERS = 4
BRANCH_DIM = 512
NORM_EPS = 1e-6
NEG = -1e30
BIG = 1e9

RWKV_HEADS = 8
RWKV_HEAD_DIM = 64
RWKV_DIM = RWKV_HEADS * RWKV_HEAD_DIM
RWKV_DECAY_LORA = 96
RWKV_ICLR_LORA = 96
RWKV_GATE_LORA = 256
RWKV_GN_EPS = 64e-5
RWKV_IN_DIM = 3 * RWKV_DIM + RWKV_DECAY_LORA + RWKV_ICLR_LORA + RWKV_GATE_LORA

MLSTM_HEADS = 4
MLSTM_QK_DIM = 128
MLSTM_V_DIM = 128
MLSTM_QK_WIDTH = MLSTM_HEADS * MLSTM_QK_DIM
MLSTM_DIM = MLSTM_HEADS * MLSTM_V_DIM
MLSTM_CONV = 4
MLSTM_CHUNK = 64
MLSTM_IN_DIM = 2 * MLSTM_QK_WIDTH + 2 * MLSTM_DIM + 2 * MLSTM_HEADS

ATTN_HEAD_DIM = 128
ROPE_DIM = ATTN_HEAD_DIM // 4
ROPE_THETA = 500000.0

MOBA_HEADS = 4
MOBA_DIM = MOBA_HEADS * ATTN_HEAD_DIM
MOBA_BLOCK = 256
MOBA_TOPK = 3
MOBA_IN_DIM = 3 * MOBA_DIM

NSA_HEADS = 4
NSA_DIM = NSA_HEADS * ATTN_HEAD_DIM
NSA_CMP_BLOCK = 32
NSA_CMP_STRIDE = 16
NSA_SLC_BLOCK = 64
NSA_SLC_TOPK = 16
NSA_WINDOW = 512
NSA_IN_DIM = NSA_DIM + 6 * ATTN_HEAD_DIM + 3 * NSA_HEADS

MERGE_IN_DIM = N_MIXERS * D_MODEL
IN_WIDTHS = (RWKV_IN_DIM, MLSTM_IN_DIM, MOBA_IN_DIM, NSA_IN_DIM, MERGE_IN_DIM)

N_EXPERTS = 64
TOP_K = 8
N_GROUPS = 8
TOPK_GROUPS = 4
EXPERT_DIM = 512
ROUTED_SCALE = 2.5
MOE_BLOCK = 256

V7X_VMEM_LIMIT_BYTES = 48 * 1024 * 1024
LANES = 128
SUBLANES = 8


def _mm_kernel(a_ref, b_ref, o_ref):
    o_ref[...] = jnp.dot(a_ref[...], b_ref[...], preferred_element_type=F32).astype(o_ref.dtype)


def _pick_tile(n, pref):
    t = min(pref, n)
    while n % t:
        t //= 2
    return t


def _pick_lane_tile(n, pref):
    assert n % LANES == 0
    units = n // LANES
    best = 1
    for u in range(1, units + 1):
        if units % u == 0 and u * LANES <= pref:
            best = u
    return best * LANES


def _matmul(a, b, out_dtype=F32, tm=1024, tn=1024, cols=None):
    M, K = a.shape
    off, N = (0, b.shape[1]) if cols is None else cols
    tm = _pick_tile(M, tm)
    tn = _pick_lane_tile(int(np.gcd(N, off)) if off else N, tn)
    first = off // tn
    return pl.pallas_call(
        _mm_kernel,
        grid=(N // tn, M // tm),
        in_specs=[pl.BlockSpec((tm, K), lambda j, i: (i, 0)),
                  pl.BlockSpec((K, tn), lambda j, i: (0, first + j))],
        out_specs=pl.BlockSpec((tm, tn), lambda j, i: (i, j)),
        out_shape=jax.ShapeDtypeStruct((M, N), out_dtype),
        compiler_params=pltpu.CompilerParams(dimension_semantics=("parallel", "parallel"),
                                             vmem_limit_bytes=V7X_VMEM_LIMIT_BYTES),
        name="dense_matmul",
    )(a, b)


def _mm_res_kernel(*refs, has_extra):
    if has_extra:
        a_ref, b_ref, res_ref, gate_ref, extra_ref, o_ref = refs
    else:
        a_ref, b_ref, res_ref, gate_ref, o_ref = refs
    y = jnp.dot(a_ref[...], b_ref[...], preferred_element_type=F32)
    if has_extra:
        y = y + extra_ref[...]
    o_ref[...] = res_ref[...] + gate_ref[0] * y


def _matmul_residual(a, b, res, gate, seq_len, extra=None, tm=1024, tn=1024):
    M, K = a.shape
    _, N = b.shape
    tm = _pick_tile(seq_len, tm)
    tn = _pick_lane_tile(N, tn)
    row_tile = pl.BlockSpec((tm, tn), lambda j, i: (i, j))
    in_specs = [pl.BlockSpec((tm, K), lambda j, i: (i, 0)),
                pl.BlockSpec((K, tn), lambda j, i: (0, j)),
                row_tile,
                pl.BlockSpec((1, 1, tn), lambda j, i: ((i * tm) // seq_len, 0, j))]
    args = [a, b, res, gate]
    if extra is not None:
        in_specs.append(row_tile)
        args.append(extra)
    return pl.pallas_call(
        functools.partial(_mm_res_kernel, has_extra=extra is not None),
        grid=(N // tn, M // tm),
        in_specs=in_specs,
        out_specs=row_tile,
        out_shape=jax.ShapeDtypeStruct((M, N), F32),
        compiler_params=pltpu.CompilerParams(dimension_semantics=("parallel", "parallel"),
                                             vmem_limit_bytes=V7X_VMEM_LIMIT_BYTES),
        name="matmul_gated_residual",
    )(*args)


def _normmod_kernel(*refs, with_router):
    if with_router:
        x_ref, g_ref, sc_ref, sh_ref, rt_ref, o_ref, lg_ref, pk_ref = refs
    else:
        x_ref, g_ref, sc_ref, sh_ref, o_ref = refs
    x = x_ref[...]
    y = x * lax.rsqrt(jnp.mean(x * x, axis=-1, keepdims=True) + NORM_EPS) * g_ref[...]
    h = y * (1.0 + sc_ref[0]) + sh_ref[0]
    o_ref[...] = h.astype(o_ref.dtype)
    if with_router:
        lg_ref[...] = lax.dot_general(rt_ref[...], h, (((1,), (1,)), ((), ())),
                                      preferred_element_type=F32, precision=HIGHEST)
        pk_ref[...] = _pack_bf16_halves(h)


def _pack_bf16_halves(y):
    half = y.shape[1] // 2
    lo = pltpu.bitcast(y[:, :half].astype(BF16).astype(F32), jnp.uint32) >> 16
    hi = pltpu.bitcast(y[:, half:].astype(BF16).astype(F32), jnp.uint32) & jnp.uint32(0xFFFF0000)
    return hi | lo


def _unpack_bf16_halves(word):
    return pltpu.bitcast(word << 16, F32), pltpu.bitcast(word & jnp.uint32(0xFFFF0000), F32)


def _normmod(x2d, g, scale, shift, seq_len, router=None, tm=512):
    M, D = x2d.shape
    tm = _pick_tile(seq_len, tm)
    mod_spec = pl.BlockSpec((1, 1, D), lambda i: ((i * tm) // seq_len, 0, 0))
    row_spec = pl.BlockSpec((tm, D), lambda i: (i, 0))
    in_specs = [row_spec, pl.BlockSpec((1, D), lambda i: (0, 0)), mod_spec, mod_spec]
    args = [x2d, g.reshape(1, D), scale, shift]
    out_shape = [jax.ShapeDtypeStruct((M, D), BF16)]
    out_specs = [row_spec]
    if router is not None:
        n_exp = router.shape[1]
        in_specs.append(pl.BlockSpec((n_exp, D), lambda i: (0, 0)))
        args.append(router.T)
        out_shape.append(jax.ShapeDtypeStruct((n_exp, M), F32))
        out_specs.append(pl.BlockSpec((n_exp, tm), lambda i: (0, i)))
        out_shape.append(jax.ShapeDtypeStruct((M, D // 2), jnp.uint32))
        out_specs.append(pl.BlockSpec((tm, D // 2), lambda i: (i, 0)))
    out = pl.pallas_call(
        functools.partial(_normmod_kernel, with_router=router is not None),
        grid=(M // tm,),
        in_specs=in_specs,
        out_specs=out_specs,
        out_shape=out_shape,
        compiler_params=pltpu.CompilerParams(dimension_semantics=("parallel",),
                                             vmem_limit_bytes=V7X_VMEM_LIMIT_BYTES),
        name="rmsnorm_modulate",
    )(*args)
    return tuple(out) if router is not None else out[0]


def _merge_kernel(g0, g1, g2, g3, y0, y1, y2, y3, wb_ref, o_ref):
    acc = None
    for m, (g_ref, y_ref) in enumerate(((g0, y0), (g1, y1), (g2, y2), (g3, y3))):
        t = jnp.dot(y_ref[...], wb_ref[m], preferred_element_type=F32) * jax.nn.sigmoid(g_ref[...].astype(F32))
        acc = t if acc is None else acc + t
    o_ref[...] = acc.astype(o_ref.dtype)


def _merge_branches(p_gate, ys, w_branch, tm=512, tn=512):
    M = p_gate.shape[0]
    n_mix, kb, D = w_branch.shape
    tm = _pick_tile(M, tm)
    tn = _pick_lane_tile(D, tn)
    nj = D // tn
    gate_specs = [pl.BlockSpec((tm, tn), functools.partial(lambda j, i, m: (i, m * nj + j), m=m)) for m in range(n_mix)]
    y_specs = [pl.BlockSpec((tm, kb), lambda j, i: (i, 0)) for _ in range(n_mix)]
    return pl.pallas_call(
        _merge_kernel,
        grid=(nj, M // tm),
        in_specs=gate_specs + y_specs + [pl.BlockSpec((n_mix, kb, tn), lambda j, i: (0, 0, j))],
        out_specs=pl.BlockSpec((tm, tn), lambda j, i: (i, j)),
        out_shape=jax.ShapeDtypeStruct((M, D), BF16),
        compiler_params=pltpu.CompilerParams(dimension_semantics=("parallel", "parallel"),
                                             vmem_limit_bytes=V7X_VMEM_LIMIT_BYTES),
        name="merge_gated_branches",
    )(*([p_gate] * n_mix), *ys, w_branch)


ATTN_KEY_CHUNK = 512
NSA_Q_TILE = 128
MASKED_SCORE = -1e30
RUNNING_MAX_FLOOR = -1e20


def _attn_kernel(*refs, tqt, n_rep, kc, window, use_bm, scale):
    if use_bm:
        q_ref, k_ref, v_ref, bm_ref, e_ref, o_ref = refs
    else:
        q_ref, k_ref, v_ref, o_ref = refs
    t0 = pl.program_id(2) * tqt
    d = q_ref.shape[-1]
    rows = n_rep * tqt
    q = q_ref[0].reshape(rows, d)
    dmat = lax.broadcasted_iota(jnp.int32, (tqt, kc), 1) - lax.broadcasted_iota(jnp.int32, (tqt, kc), 0)

    def body(c, carry):
        m, l, acc = carry
        k0 = pl.multiple_of(c * kc, kc)
        kch = k_ref[0, 0, pl.ds(k0, kc), :]
        vch = v_ref[0, 0, pl.ds(k0, kc), :]
        s = lax.dot_general(q, kch, (((1,), (1,)), ((), ())), preferred_element_type=F32) * scale
        off = t0 - k0
        ok = dmat <= off
        if window is not None:
            ok = ok & (dmat > off - window)
        if use_bm:
            ok = ok & (jnp.dot(bm_ref[0, 0], e_ref[c], preferred_element_type=F32) > 0.5)
        bias = jnp.where(ok, 0.0, MASKED_SCORE)
        s = (s.reshape(n_rep, tqt, kc) + bias[None]).reshape(rows, kc)
        m_new = jnp.maximum(m, jnp.max(s, axis=-1, keepdims=True))
        alpha = jnp.exp(m - m_new)
        p = jnp.exp(s - m_new)
        l = alpha * l + jnp.sum(p, axis=-1, keepdims=True)
        acc = alpha * acc + jnp.dot(p.astype(BF16), vch, preferred_element_type=F32)
        return m_new, l, acc

    lo = 0 if window is None else jnp.maximum(t0 - window + 1, 0) // kc
    hi = (t0 + tqt + kc - 1) // kc
    init = (jnp.full((rows, 1), RUNNING_MAX_FLOOR, F32), jnp.zeros((rows, 1), F32), jnp.zeros((rows, d), F32))
    _, l, acc = lax.fori_loop(lo, hi, body, init)
    o_ref[0] = (acc / l).reshape(n_rep, tqt, d).astype(o_ref.dtype)


def _block_expander(seq_len, kc, block, n_lanes=LANES):
    key_blk = (np.arange(seq_len) // block).reshape(seq_len // kc, 1, kc)
    return jnp.asarray(key_blk == np.arange(n_lanes).reshape(1, n_lanes, 1), dtype=BF16)


def _attention(q, k, v, tqt, scale, block_mask=None, block=None, window=None):
    B, HQ, S, d = q.shape
    G = k.shape[1]
    n_rep = HQ // G
    kc = _pick_tile(S, ATTN_KEY_CHUNK)
    tqt = _pick_tile(S, tqt)
    use_bm = block_mask is not None
    qspec = pl.BlockSpec((1, n_rep, tqt, d), lambda b, g, i: (b, g, i, 0))
    kvspec = pl.BlockSpec((1, 1, S, d), lambda b, g, i: (b, g, 0, 0))
    in_specs = [qspec, kvspec, kvspec]
    args = [q, k, v]
    if use_bm:
        in_specs += [pl.BlockSpec((1, 1, tqt, LANES), lambda b, g, i: (b, g, i, 0)),
                     pl.BlockSpec((S // kc, LANES, kc), lambda b, g, i: (0, 0, 0))]
        args += [block_mask, _block_expander(S, kc, block)]
    return pl.pallas_call(
        functools.partial(_attn_kernel, tqt=tqt, n_rep=n_rep, kc=kc, window=window, use_bm=use_bm, scale=scale),
        grid=(B, G, S // tqt),
        in_specs=in_specs,
        out_specs=qspec,
        out_shape=jax.ShapeDtypeStruct((B, HQ, S, d), F32),
        compiler_params=pltpu.CompilerParams(dimension_semantics=("parallel", "parallel", "parallel"),
                                             vmem_limit_bytes=V7X_VMEM_LIMIT_BYTES),
        name="block_masked_attention",
    )(*args)


def _nsa_cmp_kernel(q_ref, kc_ref, vc_ref, ov_ref, o_ref, imp_ref, *, tqt, n_rep, ncmp, scale):
    t0 = pl.program_id(1) * tqt
    d = q_ref.shape[-1]
    ncp = kc_ref.shape[1]
    rows = n_rep * tqt
    q = q_ref[0].reshape(rows, d)
    s = lax.dot_general(q, kc_ref[0], (((1,), (1,)), ((), ())), preferred_element_type=F32) * scale
    n_io = lax.broadcasted_iota(jnp.int32, (tqt, ncp), 1)
    t_io = t0 + lax.broadcasted_iota(jnp.int32, (tqt, ncp), 0)
    valid = (n_io * NSA_CMP_STRIDE + (NSA_CMP_BLOCK - 1) <= t_io) & (n_io < ncmp)
    bias = jnp.where(valid, 0.0, MASKED_SCORE)
    s = s.reshape(n_rep, tqt, ncp) + bias[None]
    m = jnp.maximum(jnp.max(s, axis=-1, keepdims=True), RUNNING_MAX_FLOOR)
    e = jnp.exp(s - m)
    l = jnp.sum(e, axis=-1, keepdims=True)
    p = e / jnp.where(l > 0.0, l, 1.0)
    o = jnp.dot(p.reshape(rows, ncp).astype(BF16), vc_ref[0], preferred_element_type=F32)
    o_ref[0] = o.reshape(n_rep, tqt, d)
    imp_ref[0] = jnp.dot(jnp.sum(p, axis=0), ov_ref[...], preferred_element_type=F32, precision=HIGHEST)


def _nsa_cmp_attention(q, k_cmp, v_cmp, overlap, ncmp, scale, tqt=128):
    B, H, S, d = q.shape
    ncp = k_cmp.shape[1]
    tqt = _pick_tile(S, tqt)
    qspec = pl.BlockSpec((1, H, tqt, d), lambda b, i: (b, 0, i, 0))
    cspec = pl.BlockSpec((1, ncp, d), lambda b, i: (b, 0, 0))
    return pl.pallas_call(
        functools.partial(_nsa_cmp_kernel, tqt=tqt, n_rep=H, ncmp=ncmp, scale=scale),
        grid=(B, S // tqt),
        in_specs=[qspec, cspec, cspec, pl.BlockSpec((ncp, LANES), lambda b, i: (0, 0))],
        out_specs=[qspec, pl.BlockSpec((1, tqt, LANES), lambda b, i: (b, i, 0))],
        out_shape=[jax.ShapeDtypeStruct((B, H, S, d), F32), jax.ShapeDtypeStruct((B, S, LANES), F32)],
        compiler_params=pltpu.CompilerParams(dimension_semantics=("parallel", "parallel"),
                                             vmem_limit_bytes=V7X_VMEM_LIMIT_BYTES),
        name="nsa_compressed_attention",
    )(q, k_cmp, v_cmp, overlap)


RWKV_TIME_CHUNK = 32
RWKV_ROWS = RWKV_HEAD_DIM // 2
RWKV_ROW_GROUPS = RWKV_ROWS // SUBLANES
RWKV_KEY_TILES = RWKV_HEAD_DIM // SUBLANES


def _rwkv_scan_kernel(keys_ref, v_ref, y_ref, st_ref):
    @pl.when(pl.program_id(0) == 0)
    def _():
        st_ref[...] = jnp.zeros_like(st_ref)

    n_t = keys_ref.shape[0]
    sub_iota = lax.broadcasted_iota(jnp.int32, (SUBLANES, st_ref.shape[-1]), 0)

    def step(t, carry):
        r = keys_ref[t, 0]
        w = keys_ref[t, 1]
        k = keys_ref[t, 2]
        kk = keys_ref[t, 3]
        bb = keys_ref[t, 4]

        def group(g, c2):
            y_tile = jnp.zeros(sub_iota.shape, F32)
            for ii in range(SUBLANES):
                i = g * SUBLANES + ii
                s = st_ref[i]
                sa = jnp.sum(jnp.sum(s * kk, axis=0), axis=0, keepdims=True)
                vrow = v_ref[t, g, pl.ds(ii, 1), :]
                s = s * w - sa[None] * bb + vrow[None] * k
                st_ref[i] = s
                yrow = jnp.sum(jnp.sum(s * r, axis=0), axis=0, keepdims=True)
                y_tile = jnp.where(sub_iota == ii, jnp.broadcast_to(yrow, sub_iota.shape), y_tile)
            y_ref[t, g] = y_tile
            return c2

        lax.fori_loop(0, RWKV_ROW_GROUPS, group, 0)
        return carry

    lax.fori_loop(0, n_t, step, 0)


def _rwkv_scan(r, w, k, v, kk, bvec):
    B, S, H, N = r.shape
    chains = B * H
    lanes = 2 * chains

    keys = jnp.transpose(jnp.stack([r, w, k, kk, bvec]), (2, 0, 4, 1, 3)).reshape(S, 5, N, chains)
    keys = jnp.concatenate([keys, keys], axis=-1).reshape(S, 5, RWKV_KEY_TILES, SUBLANES, lanes)

    def val_layout(z):
        zt = jnp.transpose(z.reshape(B, S, H, 2, RWKV_ROWS), (1, 4, 3, 0, 2))
        return zt.reshape(S, RWKV_ROW_GROUPS, SUBLANES, lanes)

    tc = _pick_tile(S, RWKV_TIME_CHUNK)
    kspec = pl.BlockSpec((tc, 5, RWKV_KEY_TILES, SUBLANES, lanes), lambda c: (c, 0, 0, 0, 0))
    vspec = pl.BlockSpec((tc, RWKV_ROW_GROUPS, SUBLANES, lanes), lambda c: (c, 0, 0, 0))
    y = pl.pallas_call(
        _rwkv_scan_kernel,
        grid=(S // tc,),
        in_specs=[kspec, vspec],
        out_specs=vspec,
        out_shape=jax.ShapeDtypeStruct((S, RWKV_ROW_GROUPS, SUBLANES, lanes), F32),
        scratch_shapes=[pltpu.VMEM((RWKV_ROWS, RWKV_KEY_TILES, SUBLANES, lanes), F32)],
        compiler_params=pltpu.CompilerParams(dimension_semantics=("arbitrary",),
                                             vmem_limit_bytes=V7X_VMEM_LIMIT_BYTES),
        name="rwkv7_scan",
    )(keys, val_layout(v))
    y = y.reshape(S, RWKV_ROWS, 2, B, H)
    return jnp.transpose(y, (3, 0, 4, 2, 1)).reshape(B, S, H, N)


ROUTE_TOKENS = 512


def _first_index_of_max(vals, index, n, axes):
    m = vals
    for ax in axes:
        m = jnp.max(m, axis=ax, keepdims=True)
    idx = jnp.where(vals == m, index, float(n))
    for ax in axes:
        idx = jnp.min(idx, axis=ax, keepdims=True)
    return m, idx


def _route_kernel(lg_ref, bias_ref, upper_ref, eidx_ref, wts_ref, rank_ref, cnt_ref, carry_ref):
    @pl.when(pl.program_id(0) == 0)
    def _():
        carry_ref[...] = jnp.zeros_like(carry_ref)

    E, tn = lg_ref.shape
    G = N_GROUPS
    J = E // G
    s3 = jax.nn.sigmoid(lg_ref[...]).reshape(G, J, tn)
    b3 = s3 + bias_ref[...].reshape(G, J, 1)
    j_io = lax.broadcasted_iota(jnp.int32, (G, J, tn), 1).astype(F32)
    g_io = lax.broadcasted_iota(jnp.int32, (G, 1, tn), 0).astype(F32)
    e_io = lax.broadcasted_iota(jnp.int32, (G, J, tn), 0).astype(F32) * J + j_io

    m1, first = _first_index_of_max(b3, j_io, J, (1,))
    m2 = jnp.max(jnp.where(j_io == first, -jnp.inf, b3), axis=1, keepdims=True)
    cur = m1 + m2
    keep = jnp.zeros((G, 1, tn), F32)
    for _ in range(TOPK_GROUPS):
        _, idx = _first_index_of_max(cur, g_io, G, (0,))
        hit = g_io == idx
        keep = jnp.where(hit, 1.0, keep)
        cur = jnp.where(hit, -jnp.inf, cur)

    cur = jnp.where(keep > 0.5, b3, NEG)
    hot = jnp.zeros((G, J, tn), F32)
    picks, weights = [], []
    for _ in range(TOP_K):
        _, idx = _first_index_of_max(cur, e_io, E, (0, 1))
        hit = e_io == idx
        w = jnp.sum(jnp.sum(jnp.where(hit, s3, 0.0), axis=0, keepdims=True), axis=1, keepdims=True)
        cur = jnp.where(hit, -jnp.inf, cur)
        hot = jnp.where(hit, 1.0, hot)
        picks.append(idx)
        weights.append(w)
    wsum = weights[0]
    for w in weights[1:]:
        wsum = wsum + w

    hot2 = hot.reshape(E, tn)
    before = jnp.dot(hot2.astype(BF16), upper_ref[...], preferred_element_type=F32) + carry_ref[:, 0:1]
    before3 = before.reshape(G, J, tn)
    for k in range(TOP_K):
        eidx_ref[k:k + 1, :] = picks[k].reshape(1, tn).astype(jnp.int32)
        wts_ref[k:k + 1, :] = (weights[k] / wsum * ROUTED_SCALE).reshape(1, tn)
        r = jnp.sum(jnp.sum(jnp.where(e_io == picks[k], before3, 0.0), axis=0, keepdims=True), axis=1, keepdims=True)
        rank_ref[k:k + 1, :] = r.reshape(1, tn).astype(jnp.int32)
    carry_ref[...] = carry_ref[...] + jnp.sum(hot2, axis=1, keepdims=True)
    cnt_ref[...] = carry_ref[...]


def _route(logits_t, bias):
    E, T = logits_t.shape
    tn = _pick_tile(T, ROUTE_TOKENS)
    upper = jnp.asarray(np.triu(np.ones((tn, tn), np.float32), 1), BF16)
    tok_spec = pl.BlockSpec((TOP_K, tn), lambda i: (0, i))
    eidx, wts, rank, cnt = pl.pallas_call(
        _route_kernel,
        grid=(T // tn,),
        in_specs=[pl.BlockSpec((E, tn), lambda i: (0, i)),
                  pl.BlockSpec((E, 1), lambda i: (0, 0)),
                  pl.BlockSpec((tn, tn), lambda i: (0, 0))],
        out_specs=[tok_spec, tok_spec, tok_spec, pl.BlockSpec((E, LANES), lambda i: (0, 0))],
        out_shape=[jax.ShapeDtypeStruct((TOP_K, T), jnp.int32), jax.ShapeDtypeStruct((TOP_K, T), F32),
                   jax.ShapeDtypeStruct((TOP_K, T), jnp.int32), jax.ShapeDtypeStruct((E, LANES), F32)],
        scratch_shapes=[pltpu.VMEM((E, LANES), F32)],
        compiler_params=pltpu.CompilerParams(dimension_semantics=("arbitrary",),
                                             vmem_limit_bytes=V7X_VMEM_LIMIT_BYTES),
        name="moe_route",
    )(logits_t, bias.reshape(E, 1), upper)
    return eidx, wts, rank, cnt[:, 0].astype(jnp.int32)


def _moe_kernel(be_ref, nused_ref, x_ref, wg_ref, wu_ref, wd_ref, o_ref, wg_s, wu_s, wd_s):
    i = pl.program_id(0)
    active = i < nused_ref[0]

    @pl.when(active & ((i == 0) | (be_ref[i] != be_ref[jnp.maximum(i - 1, 0)])))
    def _():
        wg_s[...] = wg_ref[0].astype(BF16)
        wu_s[...] = wu_ref[0].astype(BF16)
        wd_s[...] = wd_ref[0].astype(BF16)

    @pl.when(active)
    def _():
        x_lo, x_hi = (z.astype(BF16) for z in _unpack_bf16_halves(x_ref[...]))
        half = x_lo.shape[1]
        g = (jnp.dot(x_lo, wg_s[:half, :], preferred_element_type=F32)
             + jnp.dot(x_hi, wg_s[half:, :], preferred_element_type=F32))
        u = (jnp.dot(x_lo, wu_s[:half, :], preferred_element_type=F32)
             + jnp.dot(x_hi, wu_s[half:, :], preferred_element_type=F32))
        hmid = (g * jax.nn.sigmoid(g) * u).astype(BF16)
        o_ref[...] = _pack_bf16_halves(jnp.dot(hmid, wd_s[...], preferred_element_type=F32))

    @pl.when(jnp.logical_not(active))
    def _():
        o_ref[...] = jnp.zeros_like(o_ref)


def _moe_grouped(xg, blk_e, n_used, w_gate, w_up, w_down, layer):
    R = xg.shape[0]
    D = 2 * xg.shape[1]
    n_blk = R // MOE_BLOCK
    n_layers, E, _, F = w_gate.shape
    w_gate, w_up, w_down = (w.reshape((n_layers * E,) + w.shape[2:]) for w in (w_gate, w_up, w_down))

    def xmap(i, be, nu):
        return (jnp.minimum(i, jnp.maximum(nu[0] - 1, 0)), 0)

    def wmap(i, be, nu):
        return (layer * E + be[i], 0, 0)

    grid_spec = pltpu.PrefetchScalarGridSpec(
        num_scalar_prefetch=2,
        grid=(n_blk,),
        in_specs=[pl.BlockSpec((MOE_BLOCK, D // 2), xmap),
                  pl.BlockSpec((1, D, F), wmap),
                  pl.BlockSpec((1, D, F), wmap),
                  pl.BlockSpec((1, F, D), wmap)],
        out_specs=pl.BlockSpec((MOE_BLOCK, D // 2), lambda i, be, nu: (i, 0)),
        scratch_shapes=[pltpu.VMEM((D, F), BF16), pltpu.VMEM((D, F), BF16), pltpu.VMEM((F, D), BF16)],
    )
    return pl.pallas_call(
        _moe_kernel,
        grid_spec=grid_spec,
        out_shape=jax.ShapeDtypeStruct((R, D // 2), jnp.uint32),
        compiler_params=pltpu.CompilerParams(dimension_semantics=("arbitrary",),
                                             vmem_limit_bytes=V7X_VMEM_LIMIT_BYTES),
        name="moe_grouped_swiglu",
    )(blk_e, n_used, xg, w_gate, w_up, w_down)


MOE_COMBINE_ROWS = 256


def _moe_combine_kernel(a_ref, b_ref, res_ref, gate_ref, mine_ref, w_ref, o_ref):
    half = o_ref.shape[1] // 2
    y = jnp.dot(a_ref[...], b_ref[...], preferred_element_type=F32)
    lo_acc = y[:, :half]
    hi_acc = y[:, half:]
    for k in range(mine_ref.shape[0]):
        lo, hi = _unpack_bf16_halves(mine_ref[k])
        wk = w_ref[:, k:k + 1]
        lo_acc = lo_acc + wk * lo
        hi_acc = hi_acc + wk * hi
    g = gate_ref[0]
    o_ref[:, :half] = res_ref[:, :half] + g[:, :half] * lo_acc
    o_ref[:, half:] = res_ref[:, half:] + g[:, half:] * hi_acc


def _moe_combine(hmid, s_down, x_res, gate, seq_len, mine, wts):
    M, F = hmid.shape
    N = s_down.shape[1]
    n_k = mine.shape[0]
    tm = _pick_tile(seq_len, MOE_COMBINE_ROWS)
    row_tile = pl.BlockSpec((tm, N), lambda i: (i, 0))
    return pl.pallas_call(
        _moe_combine_kernel,
        grid=(M // tm,),
        in_specs=[pl.BlockSpec((tm, F), lambda i: (i, 0)),
                  pl.BlockSpec((F, N), lambda i: (0, 0)),
                  row_tile,
                  pl.BlockSpec((1, 1, N), lambda i: ((i * tm) // seq_len, 0, 0)),
                  pl.BlockSpec((n_k, tm, N // 2), lambda i: (0, i, 0)),
                  pl.BlockSpec((tm, n_k), lambda i: (i, 0))],
        out_specs=row_tile,
        out_shape=jax.ShapeDtypeStruct((M, N), F32),
        compiler_params=pltpu.CompilerParams(dimension_semantics=("parallel",),
                                             vmem_limit_bytes=V7X_VMEM_LIMIT_BYTES),
        name="moe_combine_residual",
    )(hmid, s_down, x_res, gate, mine, wts)


def _split(z, widths):
    cuts = [int(i) for i in np.cumsum(widths)[:-1]]
    return jnp.split(z, cuts, axis=-1)


def _rms(z, g, eps=NORM_EPS):
    return z * lax.rsqrt(jnp.mean(z * z, axis=-1, keepdims=True) + eps) * g


def _heads(z, n_heads):
    B, S, C = z.shape
    return z.reshape(B, S, n_heads, C // n_heads).transpose(0, 2, 1, 3)


def _merge_heads(z):
    B, H, S, d = z.shape
    return z.transpose(0, 2, 1, 3).reshape(B, S, H * d)


def _rope(z, pos):
    half = ROPE_DIM // 2
    inv_freq = ROPE_THETA ** (-jnp.arange(half, dtype=F32) / half)
    ang = pos.astype(F32)[:, None] * inv_freq[None, :]
    cos = jnp.cos(ang)
    sin = jnp.sin(ang)
    z1, z2, zr = z[..., :half], z[..., half:ROPE_DIM], z[..., ROPE_DIM:]
    return jnp.concatenate([z1 * cos - z2 * sin, z2 * cos + z1 * sin, zr], axis=-1)


def _token_shift(z):
    return jnp.pad(z, ((0, 0), (1, 0), (0, 0)))[:, :-1]


def _rwkv7(p, v_first, v_mix, mu, w0, w_up, a0, a_up, g_up, k_k, k_a, r_k, ln_w, ln_b):
    B, S, _ = p.shape
    H, N = RWKV_HEADS, RWKV_HEAD_DIM
    xs = p + (_token_shift(p) - p) * mu
    r, k, v, xw, xa, xg = _split(xs, (RWKV_DIM, RWKV_DIM, RWKV_DIM, RWKV_DECAY_LORA, RWKV_ICLR_LORA, RWKV_GATE_LORA))
    w = -jax.nn.softplus(-(w0 + jnp.tanh(xw) @ w_up)) - 0.5
    decay = jnp.exp(-jnp.exp(w))
    a = jax.nn.sigmoid(a0 + xa @ a_up)
    g = jax.nn.sigmoid(xg) @ g_up
    kk = (k * k_k).reshape(B, S, H, N)
    kk = kk / jnp.maximum(jnp.sqrt(jnp.sum(kk * kk, axis=-1, keepdims=True)), 1e-12)
    k = k * (1.0 + (a - 1.0) * k_a)
    if v_mix is None:
        v_first = v
    else:
        v0, v_down, v_up = v_mix
        v = v + (v_first - v) * jax.nn.sigmoid(v0 + (v @ v_down) @ v_up)

    def hd(z):
        return z.reshape(B, S, H, N)

    y = _rwkv_scan(hd(r), hd(decay), hd(k), hd(v), kk, kk * hd(a))
    mean = jnp.mean(y, axis=-1, keepdims=True)
    var = jnp.mean(jnp.square(y - mean), axis=-1, keepdims=True)
    y = ((y - mean) * lax.rsqrt(var + RWKV_GN_EPS)).reshape(B, S, RWKV_DIM) * ln_w + ln_b
    bonus = jnp.sum((r * k * r_k).reshape(B, S, H, N), axis=-1, keepdims=True) * v.reshape(B, S, H, N)
    y = (y + bonus.reshape(B, S, RWKV_DIM)) * g
    return y, v_first


def _mlstm_chunkwise(q, k, v, i_pre, logf):
    B, H, S, dk = q.shape
    dv = v.shape[-1]
    L = MLSTM_CHUNK
    NC = S // L
    q = q.reshape(B, H, NC, L, dk)
    k = k.reshape(B, H, NC, L, dk)
    v = v.reshape(B, H, NC, L, dv)
    ic = i_pre.reshape(B, H, NC, L)
    bcum = jnp.cumsum(logf.reshape(B, H, NC, L), axis=-1)
    gtot = bcum[..., -1]
    a_log = gtot[..., None] - bcum + ic
    a_max = jnp.max(a_log, axis=-1)

    def step(carry, inp):
        C, n, m = carry
        g_c, a_c, amax_c, k_c, v_c = inp
        m_new = jnp.maximum(g_c + m, amax_c)
        wgt = jnp.exp(a_c - m_new[..., None])
        dec = jnp.exp(g_c + m - m_new)
        C_new = dec[..., None, None] * C + jnp.einsum('bhl,bhlk,bhlv->bhkv', wgt, k_c, v_c)
        n_new = dec[..., None] * n + jnp.einsum('bhl,bhlk->bhk', wgt, k_c)
        return (C_new, n_new, m_new), (C, n, m)

    init = (jnp.zeros((B, H, dk, dv), F32), jnp.zeros((B, H, dk), F32), jnp.zeros((B, H), F32))
    xs = (jnp.moveaxis(gtot, 2, 0), jnp.moveaxis(a_log, 2, 0), jnp.moveaxis(a_max, 2, 0),
          jnp.moveaxis(k, 2, 0), jnp.moveaxis(v, 2, 0))
    _, (C_prev, n_prev, m_prev) = lax.scan(step, init, xs)
    C_prev = jnp.moveaxis(C_prev, 0, 2)
    n_prev = jnp.moveaxis(n_prev, 0, 2)
    m_prev = jnp.moveaxis(m_prev, 0, 2)

    causal = jnp.tril(jnp.ones((L, L), dtype=bool))
    logD = jnp.where(causal, bcum[..., :, None] - bcum[..., None, :] + ic[..., None, :], -jnp.inf)
    m_inter = bcum + m_prev[..., None]
    m_t = jnp.maximum(m_inter, jnp.max(logD, axis=-1))
    Dm = jnp.exp(logD - m_t[..., None])
    sqk = jnp.einsum('bhctd,bhcsd->bhcts', q, k) * Dm
    inter = jnp.exp(m_inter - m_t)
    num = jnp.einsum('bhcts,bhcsv->bhctv', sqk, v) + inter[..., None] * jnp.einsum('bhctk,bhckv->bhctv', q, C_prev)
    den = jnp.sum(sqk, axis=-1) + inter * jnp.einsum('bhctk,bhck->bhct', q, n_prev)
    h = num / jnp.maximum(jnp.abs(den), jnp.exp(-m_t))[..., None]
    return h.reshape(B, H, S, dv)


def _mlstm(p, p_gates, conv_w, conv_b, i_b, f_b, norm_g):
    H = MLSTM_HEADS
    qk, v, o = _split(p, (2 * MLSTM_QK_WIDTH, MLSTM_DIM, MLSTM_DIM))
    ig, fg = _split(p_gates, (H, H))
    S = qk.shape[1]
    qkp = jnp.pad(qk, ((0, 0), (MLSTM_CONV - 1, 0), (0, 0)))
    conv = conv_b
    for j in range(MLSTM_CONV):
        conv = conv + qkp[:, j:j + S] * conv_w[j]
    qk = jax.nn.silu(conv)
    q, k = jnp.split(qk, 2, axis=-1)
    q = _heads(q, H) * (MLSTM_QK_DIM ** -0.5)
    k = _heads(k, H)
    v = _heads(v, H)
    i_pre = jnp.moveaxis(ig + i_b, -1, 1)
    logf = jax.nn.log_sigmoid(jnp.moveaxis(fg + f_b, -1, 1))
    h = _mlstm_chunkwise(q, k, v, i_pre, logf)
    h = _rms(h, norm_g.reshape(H, 1, MLSTM_V_DIM))
    return jax.nn.sigmoid(o) * _merge_heads(h)


def _moba(q, k, v):
    B, H, S, d = q.shape
    BLK = MOBA_BLOCK
    nb = S // BLK
    kb = k.reshape(B, H, nb, BLK, d)
    pos = jnp.arange(S)
    qblk = pos // BLK
    gate = jnp.einsum('bhsd,bhnd->bhsn', q, jnp.mean(kb, axis=3), precision=HIGHEST)
    gate = jnp.where(jnp.arange(nb)[None, :] < qblk[:, None], gate, NEG)
    n_top = min(MOBA_TOPK, nb)
    _, sel = lax.top_k(gate, n_top)
    slot_ok = jnp.arange(n_top)[None, :] < qblk[:, None]
    lanes = jnp.arange(LANES)
    hit = (sel[..., None] == lanes) & slot_ok[:, :, None]
    blk_mask = jnp.any(hit, axis=3) | (lanes[None, :] == qblk[:, None])
    return _attention(q.astype(BF16), k.astype(BF16), v.astype(BF16), MOBA_BLOCK, d ** -0.5,
                      block_mask=blk_mask.astype(BF16), block=BLK)


def _nsa(p, gl, pos, q_norm, k_norm, cmp_pe, cmp_w1, cmp_b1, cmp_w2):
    B, S, _ = p.shape
    H, d = NSA_HEADS, ATTN_HEAD_DIM
    scale = d ** -0.5
    q, kc_in, vc_in, ks_in, vs_in, kw_in, vw_in = _split(p, (NSA_DIM, d, d, d, d, d, d))
    q = _rope(_rms(_heads(q, H), q_norm), pos)

    ncmp = (S - NSA_CMP_BLOCK) // NSA_CMP_STRIDE + 1
    cstart = NSA_CMP_STRIDE * jnp.arange(ncmp)
    cend = cstart + NSA_CMP_BLOCK - 1
    cidx = cstart[:, None] + jnp.arange(NSA_CMP_BLOCK)[None, :]

    def compress(z, i):
        blocks = (z[:, cidx] + cmp_pe[i]).reshape(B, ncmp, NSA_CMP_BLOCK * d)
        return jax.nn.gelu(blocks @ cmp_w1[i] + cmp_b1[i]) @ cmp_w2[i]

    k_cmp = _rope(_rms(compress(kc_in, 0), k_norm[0]), cend)
    v_cmp = compress(vc_in, 1)
    SB = NSA_SLC_BLOCK
    nsb = S // SB
    assert nsb <= LANES
    ncp = -(-ncmp // LANES) * LANES
    cs_np = NSA_CMP_STRIDE * np.arange(ncp)
    ss_np = SB * np.arange(LANES)
    overlap = ((cs_np[:, None] <= ss_np[None, :] + SB - 1) & (cs_np[:, None] + NSA_CMP_BLOCK - 1 >= ss_np[None, :])
               & (np.arange(ncp)[:, None] < ncmp) & (np.arange(LANES)[None, :] < nsb))
    qb = q.astype(BF16)
    cpad = ((0, 0), (0, ncp - ncmp), (0, 0))
    o_cmp, imp = _nsa_cmp_attention(qb, jnp.pad(k_cmp, cpad).astype(BF16), jnp.pad(v_cmp, cpad).astype(BF16),
                                    jnp.asarray(overlap, F32), ncmp, scale)
    imp = imp[:, :, :nsb]
    cur = pos // SB
    blk = jnp.arange(nsb)[None, :]
    forced = (blk == 0) | (blk == cur[:, None]) | (blk == cur[:, None] - 1)
    imp = jnp.where(forced, BIG, jnp.where(blk <= cur[:, None], imp, NEG))
    n_sel = min(NSA_SLC_TOPK, nsb)
    _, sel = lax.top_k(imp, n_sel)
    blk_mask = jnp.any(sel[..., None] == jnp.arange(LANES), axis=2)
    k_slc = _rope(_rms(ks_in, k_norm[1]), pos)
    o_slc = _attention(qb, k_slc[:, None].astype(BF16), vs_in[:, None].astype(BF16), NSA_Q_TILE, scale,
                       block_mask=blk_mask[:, None].astype(BF16), block=SB)

    k_win = _rope(_rms(kw_in, k_norm[2]), pos)
    o_win = _attention(qb, k_win[:, None].astype(BF16), vw_in[:, None].astype(BF16), NSA_Q_TILE, scale,
                       window=NSA_WINDOW)

    gates = jax.nn.sigmoid(gl).reshape(B, S, H, 3).transpose(0, 2, 1, 3)
    o = gates[..., 0:1] * o_cmp + gates[..., 1:2] * o_slc + gates[..., 2:3] * o_win
    return _merge_heads(o)


def _moe(xb, logits, x_packed, x_res, gate, seq_len, bias, w_gate, w_up, w_down, layer, s_gate, s_up, s_down):
    T, D = xb.shape
    A = T * TOP_K
    eidx, wts, rank, counts = _route(logits, bias)

    experts = jnp.arange(N_EXPERTS, dtype=jnp.int32)
    padded = (counts + MOE_BLOCK - 1) // MOE_BLOCK * MOE_BLOCK
    pstart = jnp.cumsum(padded) - padded
    slot = rank + jnp.sum(jnp.where(eidx[None] == experts[:, None, None], pstart[:, None, None], 0), axis=0)

    n_blk = -(-(A + N_EXPERTS * (MOE_BLOCK - 1)) // MOE_BLOCK)
    R = n_blk * MOE_BLOCK
    blk_e = jnp.minimum(jnp.sum((pstart + padded)[None, :] <= (jnp.arange(n_blk) * MOE_BLOCK)[:, None], axis=1),
                        N_EXPERTS - 1).astype(jnp.int32)
    n_used = (jnp.sum(padded) // MOE_BLOCK).astype(jnp.int32).reshape(1)
    tok_bits = max(int(T - 1).bit_length(), 1)
    tok_ids = jnp.arange(T, dtype=jnp.int32)[None, :]
    real_keys = (eidx << (tok_bits + 1)) | tok_ids
    n_fill = MOE_BLOCK - 1
    fill_j = jnp.arange(n_fill, dtype=jnp.int32)[None, :]
    unused = jnp.int32(N_EXPERTS << (tok_bits + 1))
    fill_keys = jnp.where(fill_j < (padded - counts)[:, None], (experts[:, None] << (tok_bits + 1)) | (1 << tok_bits), unused)
    tail = jnp.full((R - A - N_EXPERTS * n_fill,), unused, jnp.int32)
    keys = lax.sort(jnp.concatenate([real_keys.reshape(A), fill_keys.reshape(-1), tail]))
    tok_buf = jnp.where((keys >> tok_bits) & 1 == 1, 0, keys & ((1 << tok_bits) - 1))
    tok_buf = jnp.where(keys >= unused, 0, tok_buf)

    packed = _moe_grouped(x_packed[tok_buf], blk_e, n_used, w_gate, w_up, w_down, layer)
    mine = packed[slot.reshape(A)].reshape(TOP_K, T, D // 2)

    gu = _matmul(xb, jnp.concatenate([s_gate, s_up], axis=1).astype(BF16))
    F = s_gate.shape[1]
    hmid = (jax.nn.silu(gu[:, :F]) * gu[:, F:]).astype(BF16)
    return _moe_combine(hmid, s_down.astype(BF16), x_res, gate, seq_len, mine, wts.T)


def _projection_layout():
    o_b = RWKV_IN_DIM
    o_c = o_b + MLSTM_IN_DIM
    o_d = o_c + MOBA_IN_DIM
    o_g = o_d + NSA_IN_DIM
    mlstm_main = MLSTM_IN_DIM - 2 * MLSTM_HEADS
    nsa_main = NSA_IN_DIM - 3 * NSA_HEADS
    spans = (('rwkv', 0, o_b), ('mlstm', o_b, o_b + mlstm_main), ('moba', o_c, o_d), ('nsa', o_d, o_d + nsa_main),
             ('mlstm_gates', o_b + mlstm_main, o_c), ('nsa_gates', o_d + nsa_main, o_g))
    pieces, cur = [], 0
    for name, lo, hi in spans:
        pieces.append((name, lo, hi, cur))
        cur += -(-(hi - lo) // LANES) * LANES
    return tuple(pieces), cur, o_g


def _relayout_kernel(w_ref, mix_ref, gates_ref, *, pieces, gate_lo):
    for _, lo, hi, off in pieces:
        width = hi - lo
        mix_ref[:, off:off + width] = w_ref[0, :, lo:hi].astype(BF16)
        pad = -width % LANES
        if pad:
            mix_ref[:, off + width:off + width + pad] = jnp.zeros((mix_ref.shape[0], pad), BF16)
    gates_ref[...] = w_ref[0, :, gate_lo:].astype(BF16)


def _projection_weights(w_in, layer, rows=128):
    _, D, n_in = w_in.shape
    pieces, mix_width, gate_lo = _projection_layout()
    w_mix, w_gates = pl.pallas_call(
        functools.partial(_relayout_kernel, pieces=pieces, gate_lo=gate_lo),
        grid=(D // rows,),
        in_specs=[pl.BlockSpec((1, rows, n_in), lambda i: (layer, i, 0))],
        out_specs=[pl.BlockSpec((rows, mix_width), lambda i: (i, 0)),
                   pl.BlockSpec((rows, n_in - gate_lo), lambda i: (i, 0))],
        out_shape=[jax.ShapeDtypeStruct((D, mix_width), BF16), jax.ShapeDtypeStruct((D, n_in - gate_lo), BF16)],
        compiler_params=pltpu.CompilerParams(dimension_semantics=("parallel",),
                                             vmem_limit_bytes=V7X_VMEM_LIMIT_BYTES),
        name="projection_weight_relayout",
    )(w_in)
    return w_mix, w_gates, {name: off for name, _, _, off in pieces}


def kernel(x, c, ada_w, ada_b, norm_mix_g, norm_ffn_g, w_in, rwkv_mu, rwkv_w0, rwkv_w_up, rwkv_a0, rwkv_a_up, rwkv_g_up, rwkv_k_k, rwkv_k_a, rwkv_r_k, rwkv_ln_w, rwkv_ln_b, rwkv_v0, rwkv_v_down, rwkv_v_up, mlstm_conv_w, mlstm_conv_b, mlstm_i_b, mlstm_f_b, mlstm_norm_g, moba_q_norm, moba_k_norm, nsa_q_norm, nsa_k_norm, nsa_cmp_pe, nsa_cmp_w1, nsa_cmp_b1, nsa_cmp_w2, w_branch, w_out, moe_router, moe_bias, moe_w_gate, moe_w_up, moe_w_down, shared_w_gate, shared_w_up, shared_w_down):
    B, S, D = x.shape
    T = B * S
    depth = ada_w.shape[0]
    pos = jnp.arange(S)
    cond = jax.nn.silu(c)
    v_first = None
    x2d = x.reshape(T, D)
    for l in range(depth):
        mod = jnp.dot(cond, ada_w[l], precision=HIGHEST) + ada_b[l]
        sh_mix, sc_mix, gate_mix, sh_ffn, sc_ffn, gate_ffn = [z[:, None, :] for z in jnp.split(mod, 6, axis=-1)]

        hb = _normmod(x2d, norm_mix_g[l], sc_mix, sh_mix, S)
        w_mix, w_gates, offs = _projection_weights(w_in, l)
        p_gate = _matmul(hb, w_gates, out_dtype=BF16)

        def proj(name, width, dtype=F32, n_pieces=1):
            padded = -(-width // LANES) * LANES * n_pieces
            return _matmul(hb, w_mix, out_dtype=dtype, cols=(offs[name], padded)).reshape(B, S, padded)

        small = proj('mlstm_gates', 2 * MLSTM_HEADS, n_pieces=2)
        nsa_gate_off = offs['nsa_gates'] - offs['mlstm_gates']
        v_mix = None if l == 0 else (rwkv_v0[l - 1], rwkv_v_down[l - 1], rwkv_v_up[l - 1])
        y_a, v_first = _rwkv7(proj('rwkv', RWKV_IN_DIM)[:, :, :RWKV_IN_DIM], v_first, v_mix, rwkv_mu[l], rwkv_w0[l],
                              rwkv_w_up[l], rwkv_a0[l], rwkv_a_up[l], rwkv_g_up[l], rwkv_k_k[l], rwkv_k_a[l],
                              rwkv_r_k[l], rwkv_ln_w[l], rwkv_ln_b[l])
        y_b = _mlstm(proj('mlstm', MLSTM_IN_DIM - 2 * MLSTM_HEADS), small[:, :, :2 * MLSTM_HEADS],
                     mlstm_conv_w[l], mlstm_conv_b[l], mlstm_i_b[l], mlstm_f_b[l], mlstm_norm_g[l])
        q_c, k_c, v_c = _split(proj('moba', MOBA_IN_DIM, BF16), (MOBA_DIM, MOBA_DIM, MOBA_DIM))
        q_c = _rope(_rms(_heads(q_c, MOBA_HEADS).astype(F32), moba_q_norm[l]), pos)
        k_c = _rope(_rms(_heads(k_c, MOBA_HEADS).astype(F32), moba_k_norm[l]), pos)
        y_c = _merge_heads(_moba(q_c, k_c, _heads(v_c, MOBA_HEADS)))
        y_d = _nsa(proj('nsa', NSA_IN_DIM - 3 * NSA_HEADS, BF16).astype(F32),
                   small[:, :, nsa_gate_off:nsa_gate_off + 3 * NSA_HEADS], pos, nsa_q_norm[l],
                   nsa_k_norm[l], nsa_cmp_pe[l], nsa_cmp_w1[l], nsa_cmp_b1[l], nsa_cmp_w2[l])
        ys = [y.reshape(T, BRANCH_DIM).astype(BF16) for y in (y_a, y_b, y_c, y_d)]
        merged = _merge_branches(p_gate, ys, w_branch[l].astype(BF16))
        x2d = _matmul_residual(merged, w_out[l].astype(BF16), x2d, gate_mix, S)

        hb, logits, hb_packed = _normmod(x2d, norm_ffn_g[l], sc_ffn, sh_ffn, S, router=moe_router[l])
        x2d = _moe(hb, logits, hb_packed, x2d, gate_ffn, S, moe_bias[l], moe_w_gate, moe_w_up, moe_w_down, l,
                   shared_w_gate[l], shared_w_up[l], shared_w_down[l])
    return x2d.reshape(B, S, D)
```

```python
import functools

import numpy as np
import jax
import jax.numpy as jnp
from jax import lax
from jax.experimental import pallas as pl
from jax.experimental.pallas import tpu as pltpu

F32 = jnp.float32
BF16 = jnp.bfloat16
HIGHEST = lax.Precision.HIGHEST

D_MODEL = 2048
N_MIXERS = 4
BRANCH_DIM = 512
NORM_EPS = 1e-6
NEG = -1e30
BIG = 1e9

RWKV_HEADS = 8
RWKV_HEAD_DIM = 64
RWKV_DIM = RWKV_HEADS * RWKV_HEAD_DIM
RWKV_DECAY_LORA = 96
RWKV_ICLR_LORA = 96
RWKV_GATE_LORA = 256
RWKV_GN_EPS = 64e-5
RWKV_IN_DIM = 3 * RWKV_DIM + RWKV_DECAY_LORA + RWKV_ICLR_LORA + RWKV_GATE_LORA

MLSTM_HEADS = 4
MLSTM_QK_DIM = 128
MLSTM_V_DIM = 128
MLSTM_QK_WIDTH = MLSTM_HEADS * MLSTM_QK_DIM
MLSTM_DIM = MLSTM_HEADS * MLSTM_V_DIM
MLSTM_CONV = 4
MLSTM_CHUNK = 64
MLSTM_IN_DIM = 2 * MLSTM_QK_WIDTH + 2 * MLSTM_DIM + 2 * MLSTM_HEADS

ATTN_HEAD_DIM = 128
ROPE_DIM = ATTN_HEAD_DIM // 4
ROPE_THETA = 500000.0

MOBA_HEADS = 4
MOBA_DIM = MOBA_HEADS * ATTN_HEAD_DIM
MOBA_BLOCK = 256
MOBA_TOPK = 3
MOBA_IN_DIM = 3 * MOBA_DIM

NSA_HEADS = 4
NSA_DIM = NSA_HEADS * ATTN_HEAD_DIM
NSA_CMP_BLOCK = 32
NSA_CMP_STRIDE = 16
NSA_SLC_BLOCK = 64
NSA_SLC_TOPK = 16
NSA_WINDOW = 512
NSA_IN_DIM = NSA_DIM + 6 * ATTN_HEAD_DIM + 3 * NSA_HEADS

MERGE_IN_DIM = N_MIXERS * D_MODEL
IN_WIDTHS = (RWKV_IN_DIM, MLSTM_IN_DIM, MOBA_IN_DIM, NSA_IN_DIM, MERGE_IN_DIM)

N_EXPERTS = 64
TOP_K = 8
N_GROUPS = 8
TOPK_GROUPS = 4
EXPERT_DIM = 512
ROUTED_SCALE = 2.5
MOE_BLOCK = 256

V7X_VMEM_LIMIT_BYTES = 48 * 1024 * 1024
LANES = 128
SUBLANES = 8


def _mm_kernel(a_ref, b_ref, o_ref):
    o_ref[...] = jnp.dot(a_ref[...], b_ref[...], preferred_element_type=F32).astype(o_ref.dtype)


def _pick_tile(n, pref):
    t = min(pref, n)
    while n % t:
        t //= 2
    return t


def _pick_lane_tile(n, pref):
    assert n % LANES == 0
    units = n // LANES
    best = 1
    for u in range(1, units + 1):
        if units % u == 0 and u * LANES <= pref:
            best = u
    return best * LANES


def _matmul(a, b, out_dtype=F32, tm=1024, tn=1024, cols=None):
    M, K = a.shape
    off, N = (0, b.shape[1]) if cols is None else cols
    tm = _pick_tile(M, tm)
    tn = _pick_lane_tile(int(np.gcd(N, off)) if off else N, tn)
    first = off // tn
    return pl.pallas_call(
        _mm_kernel,
        grid=(N // tn, M // tm),
        in_specs=[pl.BlockSpec((tm, K), lambda j, i: (i, 0)),
                  pl.BlockSpec((K, tn), lambda j, i: (0, first + j))],
        out_specs=pl.BlockSpec((tm, tn), lambda j, i: (i, j)),
        out_shape=jax.ShapeDtypeStruct((M, N), out_dtype),
        compiler_params=pltpu.CompilerParams(dimension_semantics=("parallel", "parallel"),
                                             vmem_limit_bytes=V7X_VMEM_LIMIT_BYTES),
        name="dense_matmul",
    )(a, b)


def _mm_res_kernel(*refs, has_extra):
    if has_extra:
        a_ref, b_ref, res_ref, gate_ref, extra_ref, o_ref = refs
    else:
        a_ref, b_ref, res_ref, gate_ref, o_ref = refs
    y = jnp.dot(a_ref[...], b_ref[...], preferred_element_type=F32)
    if has_extra:
        y = y + extra_ref[...]
    o_ref[...] = res_ref[...] + gate_ref[0] * y


def _matmul_residual(a, b, res, gate, seq_len, extra=None, tm=1024, tn=1024):
    M, K = a.shape
    _, N = b.shape
    tm = _pick_tile(seq_len, tm)
    tn = _pick_lane_tile(N, tn)
    row_tile = pl.BlockSpec((tm, tn), lambda j, i: (i, j))
    in_specs = [pl.BlockSpec((tm, K), lambda j, i: (i, 0)),
                pl.BlockSpec((K, tn), lambda j, i: (0, j)),
                row_tile,
                pl.BlockSpec((1, 1, tn), lambda j, i: ((i * tm) // seq_len, 0, j))]
    args = [a, b, res, gate]
    if extra is not None:
        in_specs.append(row_tile)
        args.append(extra)
    return pl.pallas_call(
        functools.partial(_mm_res_kernel, has_extra=extra is not None),
        grid=(N // tn, M // tm),
        in_specs=in_specs,
        out_specs=row_tile,
        out_shape=jax.ShapeDtypeStruct((M, N), F32),
        compiler_params=pltpu.CompilerParams(dimension_semantics=("parallel", "parallel"),
                                             vmem_limit_bytes=V7X_VMEM_LIMIT_BYTES),
        name="matmul_gated_residual",
    )(*args)


def _normmod_kernel(*refs, with_router):
    if with_router:
        x_ref, g_ref, sc_ref, sh_ref, rt_ref, o_ref, lg_ref, pk_ref = refs
    else:
        x_ref, g_ref, sc_ref, sh_ref, o_ref = refs
    x = x_ref[...]
    y = x * lax.rsqrt(jnp.mean(x * x, axis=-1, keepdims=True) + NORM_EPS) * g_ref[...]
    h = y * (1.0 + sc_ref[0]) + sh_ref[0]
    o_ref[...] = h.astype(o_ref.dtype)
    if with_router:
        lg_ref[...] = lax.dot_general(rt_ref[...], h, (((1,), (1,)), ((), ())),
                                      preferred_element_type=F32, precision=HIGHEST)
        pk_ref[...] = _pack_bf16_halves(h)


def _pack_bf16_halves(y):
    half = y.shape[1] // 2
    lo = pltpu.bitcast(y[:, :half].astype(BF16).astype(F32), jnp.uint32) >> 16
    hi = pltpu.bitcast(y[:, half:].astype(BF16).astype(F32), jnp.uint32) & jnp.uint32(0xFFFF0000)
    return hi | lo


def _unpack_bf16_halves(word):
    return pltpu.bitcast(word << 16, F32), pltpu.bitcast(word & jnp.uint32(0xFFFF0000), F32)


def _normmod(x2d, g, scale, shift, seq_len, router=None, tm=512):
    M, D = x2d.shape
    tm = _pick_tile(seq_len, tm)
    mod_spec = pl.BlockSpec((1, 1, D), lambda i: ((i * tm) // seq_len, 0, 0))
    row_spec = pl.BlockSpec((tm, D), lambda i: (i, 0))
    in_specs = [row_spec, pl.BlockSpec((1, D), lambda i: (0, 0)), mod_spec, mod_spec]
    args = [x2d, g.reshape(1, D), scale, shift]
    out_shape = [jax.ShapeDtypeStruct((M, D), BF16)]
    out_specs = [row_spec]
    if router is not None:
        n_exp = router.shape[1]
        in_specs.append(pl.BlockSpec((n_exp, D), lambda i: (0, 0)))
        args.append(router.T)
        out_shape.append(jax.ShapeDtypeStruct((n_exp, M), F32))
        out_specs.append(pl.BlockSpec((n_exp, tm), lambda i: (0, i)))
        out_shape.append(jax.ShapeDtypeStruct((M, D // 2), jnp.uint32))
        out_specs.append(pl.BlockSpec((tm, D // 2), lambda i: (i, 0)))
    out = pl.pallas_call(
        functools.partial(_normmod_kernel, with_router=router is not None),
        grid=(M // tm,),
        in_specs=in_specs,
        out_specs=out_specs,
        out_shape=out_shape,
        compiler_params=pltpu.CompilerParams(dimension_semantics=("parallel",),
                                             vmem_limit_bytes=V7X_VMEM_LIMIT_BYTES),
        name="rmsnorm_modulate",
    )(*args)
    return tuple(out) if router is not None else out[0]


def _merge_kernel(g0, g1, g2, g3, y0, y1, y2, y3, wb_ref, o_ref):
    acc = None
    for m, (g_ref, y_ref) in enumerate(((g0, y0), (g1, y1), (g2, y2), (g3, y3))):
        t = jnp.dot(y_ref[...], wb_ref[m], preferred_element_type=F32) * jax.nn.sigmoid(g_ref[...].astype(F32))
        acc = t if acc is None else acc + t
    o_ref[...] = acc.astype(o_ref.dtype)


def _merge_branches(p_gate, ys, w_branch, tm=512, tn=512):
    M = p_gate.shape[0]
    n_mix, kb, D = w_branch.shape
    tm = _pick_tile(M, tm)
    tn = _pick_lane_tile(D, tn)
    nj = D // tn
    gate_specs = [pl.BlockSpec((tm, tn), functools.partial(lambda j, i, m: (i, m * nj + j), m=m)) for m in range(n_mix)]
    y_specs = [pl.BlockSpec((tm, kb), lambda j, i: (i, 0)) for _ in range(n_mix)]
    return pl.pallas_call(
        _merge_kernel,
        grid=(nj, M // tm),
        in_specs=gate_specs + y_specs + [pl.BlockSpec((n_mix, kb, tn), lambda j, i: (0, 0, j))],
        out_specs=pl.BlockSpec((tm, tn), lambda j, i: (i, j)),
        out_shape=jax.ShapeDtypeStruct((M, D), BF16),
        compiler_params=pltpu.CompilerParams(dimension_semantics=("parallel", "parallel"),
                                             vmem_limit_bytes=V7X_VMEM_LIMIT_BYTES),
        name="merge_gated_branches",
    )(*([p_gate] * n_mix), *ys, w_branch)


HEAD_PREP_ROWS = 256


def _rope_tables(seq_len):
    half = ROPE_DIM // 2
    inv_freq = ROPE_THETA ** (-jnp.arange(half, dtype=F32) / half)
    ang = jnp.arange(seq_len, dtype=F32)[:, None] * inv_freq[None, :]
    cos, sin = jnp.cos(ang), jnp.sin(ang)
    rest = LANES - ROPE_DIM
    cosf = jnp.concatenate([cos, cos, jnp.ones((seq_len, rest), F32)], axis=1)
    sinf = jnp.concatenate([-sin, sin, jnp.zeros((seq_len, rest), F32)], axis=1)
    return cosf, sinf


def _head_prep_kernel(*refs, groups, with_mean):
    p_ref, g_ref, cos_ref, sin_ref = refs[:4]
    out_refs = refs[4:]
    half = ROPE_DIM // 2
    lane = lax.broadcasted_iota(jnp.int32, (p_ref.shape[0], LANES), 1)
    cosf = cos_ref[...]
    sinf = sin_ref[...]
    for gi, (n_heads, first_tile, gain_row) in enumerate(groups):
        for h in range(n_heads):
            c0 = (first_tile + h) * LANES
            z = p_ref[:, c0:c0 + LANES]
            if gain_row is not None:
                z = z.astype(F32)
                z = z * lax.rsqrt(jnp.mean(z * z, axis=-1, keepdims=True) + NORM_EPS) * g_ref[gain_row:gain_row + 1, :]
                partner = jnp.where(lane < half, pltpu.roll(z, LANES - half, axis=1), pltpu.roll(z, half, axis=1))
                z = z * cosf + partner * sinf
                if with_mean and gi == 1:
                    out_refs[-1][0, 0, h:h + 1, :] = jnp.mean(z, axis=0, keepdims=True)
            out_refs[gi][0, h] = z.astype(BF16)


def _head_prep(p, gains, groups, seq_len, with_mean=False):
    M, W = p.shape
    B = M // seq_len
    tm = _pick_tile(seq_len, HEAD_PREP_ROWS)
    per_seq = seq_len // tm
    cosf, sinf = _rope_tables(seq_len)
    rope_spec = pl.BlockSpec((tm, LANES), lambda i: (i % per_seq, 0))
    out_shape = [jax.ShapeDtypeStruct((B, n, seq_len, LANES), BF16) for n, _, _ in groups]
    out_specs = [pl.BlockSpec((1, n, tm, LANES), lambda i: (i // per_seq, 0, i % per_seq, 0)) for n, _, _ in groups]
    if with_mean:
        n = groups[1][0]
        out_shape.append(jax.ShapeDtypeStruct((B, per_seq, n, LANES), F32))
        out_specs.append(pl.BlockSpec((1, 1, n, LANES), lambda i: (i // per_seq, i % per_seq, 0, 0)))
    return pl.pallas_call(
        functools.partial(_head_prep_kernel, groups=tuple(groups), with_mean=with_mean),
        grid=(M // tm,),
        in_specs=[pl.BlockSpec((tm, W), lambda i: (i, 0)),
                  pl.BlockSpec(gains.shape, lambda i: (0, 0)),
                  rope_spec, rope_spec],
        out_specs=out_specs,
        out_shape=out_shape,
        compiler_params=pltpu.CompilerParams(dimension_semantics=("parallel",),
                                             vmem_limit_bytes=V7X_VMEM_LIMIT_BYTES),
        name="attention_head_prep",
    )(p, gains, cosf, sinf)


ATTN_KEY_CHUNK = 512
NSA_Q_TILE = 128
MASKED_SCORE = -1e30
RUNNING_MAX_FLOOR = -1e20


def _attn_kernel(*refs, tqt, n_rep, kc, window, use_bm, scale):
    if use_bm:
        q_ref, k_ref, v_ref, bm_ref, e_ref, o_ref = refs
    else:
        q_ref, k_ref, v_ref, o_ref = refs
    t0 = pl.program_id(2) * tqt
    d = q_ref.shape[-1]
    rows = n_rep * tqt
    q = q_ref[0].reshape(rows, d)
    dmat = lax.broadcasted_iota(jnp.int32, (tqt, kc), 1) - lax.broadcasted_iota(jnp.int32, (tqt, kc), 0)

    def body(c, carry):
        m, l, acc = carry
        k0 = pl.multiple_of(c * kc, kc)
        kch = k_ref[0, 0, pl.ds(k0, kc), :]
        vch = v_ref[0, 0, pl.ds(k0, kc), :]
        s = lax.dot_general(q, kch, (((1,), (1,)), ((), ())), preferred_element_type=F32) * scale
        off = t0 - k0
        ok = dmat <= off
        if window is not None:
            ok = ok & (dmat > off - window)
        if use_bm:
            ok = ok & (jnp.dot(bm_ref[0, 0], e_ref[c], preferred_element_type=F32) > 0.5)
        bias = jnp.where(ok, 0.0, MASKED_SCORE)
        s = (s.reshape(n_rep, tqt, kc) + bias[None]).reshape(rows, kc)
        m_new = jnp.maximum(m, jnp.max(s, axis=-1, keepdims=True))
        alpha = jnp.exp(m - m_new)
        p = jnp.exp(s - m_new)
        l = alpha * l + jnp.sum(p, axis=-1, keepdims=True)
        acc = alpha * acc + jnp.dot(p.astype(BF16), vch, preferred_element_type=F32)
        return m_new, l, acc

    lo = 0 if window is None else jnp.maximum(t0 - window + 1, 0) // kc
    hi = (t0 + tqt + kc - 1) // kc
    init = (jnp.full((rows, 1), RUNNING_MAX_FLOOR, F32), jnp.zeros((rows, 1), F32), jnp.zeros((rows, d), F32))
    _, l, acc = lax.fori_loop(lo, hi, body, init)
    o_ref[0] = (acc / l).reshape(n_rep, tqt, d).astype(o_ref.dtype)


def _block_expander(seq_len, kc, block, n_lanes=LANES):
    key_blk = (np.arange(seq_len) // block).reshape(seq_len // kc, 1, kc)
    return jnp.asarray(key_blk == np.arange(n_lanes).reshape(1, n_lanes, 1), dtype=BF16)


def _attention(q, k, v, tqt, scale, block_mask=None, block=None, window=None):
    B, HQ, S, d = q.shape
    G = k.shape[1]
    n_rep = HQ // G
    kc = _pick_tile(S, ATTN_KEY_CHUNK)
    tqt = _pick_tile(S, tqt)
    use_bm = block_mask is not None
    qspec = pl.BlockSpec((1, n_rep, tqt, d), lambda b, g, i: (b, g, i, 0))
    kvspec = pl.BlockSpec((1, 1, S, d), lambda b, g, i: (b, g, 0, 0))
    in_specs = [qspec, kvspec, kvspec]
    args = [q, k, v]
    if use_bm:
        in_specs += [pl.BlockSpec((1, 1, tqt, LANES), lambda b, g, i: (b, g, i, 0)),
                     pl.BlockSpec((S // kc, LANES, kc), lambda b, g, i: (0, 0, 0))]
        args += [block_mask, _block_expander(S, kc, block)]
    return pl.pallas_call(
        functools.partial(_attn_kernel, tqt=tqt, n_rep=n_rep, kc=kc, window=window, use_bm=use_bm, scale=scale),
        grid=(B, G, S // tqt),
        in_specs=in_specs,
        out_specs=qspec,
        out_shape=jax.ShapeDtypeStruct((B, HQ, S, d), F32),
        compiler_params=pltpu.CompilerParams(dimension_semantics=("parallel", "parallel", "parallel"),
                                             vmem_limit_bytes=V7X_VMEM_LIMIT_BYTES),
        name="block_masked_attention",
    )(*args)


def _nsa_cmp_kernel(q_ref, kc_ref, vc_ref, ov_ref, o_ref, imp_ref, *, tqt, n_rep, ncmp, scale):
    t0 = pl.program_id(1) * tqt
    d = q_ref.shape[-1]
    ncp = kc_ref.shape[1]
    rows = n_rep * tqt
    q = q_ref[0].reshape(rows, d)
    s = lax.dot_general(q, kc_ref[0], (((1,), (1,)), ((), ())), preferred_element_type=F32) * scale
    n_io = lax.broadcasted_iota(jnp.int32, (tqt, ncp), 1)
    t_io = t0 + lax.broadcasted_iota(jnp.int32, (tqt, ncp), 0)
    valid = (n_io * NSA_CMP_STRIDE + (NSA_CMP_BLOCK - 1) <= t_io) & (n_io < ncmp)
    bias = jnp.where(valid, 0.0, MASKED_SCORE)
    s = s.reshape(n_rep, tqt, ncp) + bias[None]
    m = jnp.maximum(jnp.max(s, axis=-1, keepdims=True), RUNNING_MAX_FLOOR)
    e = jnp.exp(s - m)
    l = jnp.sum(e, axis=-1, keepdims=True)
    p = e / jnp.where(l > 0.0, l, 1.0)
    o = jnp.dot(p.reshape(rows, ncp).astype(BF16), vc_ref[0], preferred_element_type=F32)
    o_ref[0] = o.reshape(n_rep, tqt, d)
    imp_ref[0] = jnp.dot(jnp.sum(p, axis=0), ov_ref[...], preferred_element_type=F32, precision=HIGHEST)


def _nsa_cmp_attention(q, k_cmp, v_cmp, overlap, ncmp, scale, tqt=128):
    B, H, S, d = q.shape
    ncp = k_cmp.shape[1]
    tqt = _pick_tile(S, tqt)
    qspec = pl.BlockSpec((1, H, tqt, d), lambda b, i: (b, 0, i, 0))
    cspec = pl.BlockSpec((1, ncp, d), lambda b, i: (b, 0, 0))
    return pl.pallas_call(
        functools.partial(_nsa_cmp_kernel, tqt=tqt, n_rep=H, ncmp=ncmp, scale=scale),
        grid=(B, S // tqt),
        in_specs=[qspec, cspec, cspec, pl.BlockSpec((ncp, LANES), lambda b, i: (0, 0))],
        out_specs=[qspec, pl.BlockSpec((1, tqt, LANES), lambda b, i: (b, i, 0))],
        out_shape=[jax.ShapeDtypeStruct((B, H, S, d), F32), jax.ShapeDtypeStruct((B, S, LANES), F32)],
        compiler_params=pltpu.CompilerParams(dimension_semantics=("parallel", "parallel"),
                                             vmem_limit_bytes=V7X_VMEM_LIMIT_BYTES),
        name="nsa_compressed_attention",
    )(q, k_cmp, v_cmp, overlap)


RWKV_TIME_CHUNK = 32
RWKV_ROWS = RWKV_HEAD_DIM // 2
RWKV_ROW_GROUPS = RWKV_ROWS // SUBLANES
RWKV_KEY_TILES = RWKV_HEAD_DIM // SUBLANES


def _rwkv_scan_kernel(keys_ref, v_ref, y_ref, st_ref):
    @pl.when(pl.program_id(0) == 0)
    def _():
        st_ref[...] = jnp.zeros_like(st_ref)

    n_t = keys_ref.shape[0]
    sub_iota = lax.broadcasted_iota(jnp.int32, (SUBLANES, st_ref.shape[-1]), 0)

    def step(t, carry):
        r = keys_ref[t, 0]
        w = keys_ref[t, 1]
        k = keys_ref[t, 2]
        kk = keys_ref[t, 3]
        bb = keys_ref[t, 4]

        def group(g, c2):
            y_tile = jnp.zeros(sub_iota.shape, F32)
            for ii in range(SUBLANES):
                i = g * SUBLANES + ii
                s = st_ref[i]
                sa = jnp.sum(jnp.sum(s * kk, axis=0), axis=0, keepdims=True)
                vrow = v_ref[t, g, pl.ds(ii, 1), :]
                s = s * w - sa[None] * bb + vrow[None] * k
                st_ref[i] = s
                yrow = jnp.sum(jnp.sum(s * r, axis=0), axis=0, keepdims=True)
                y_tile = jnp.where(sub_iota == ii, jnp.broadcast_to(yrow, sub_iota.shape), y_tile)
            y_ref[t, g] = y_tile
            return c2

        lax.fori_loop(0, RWKV_ROW_GROUPS, group, 0)
        return carry

    lax.fori_loop(0, n_t, step, 0)


def _rwkv_scan(r, w, k, v, kk, bvec):
    B, S, H, N = r.shape
    chains = B * H
    lanes = 2 * chains

    keys = jnp.transpose(jnp.stack([r, w, k, kk, bvec]), (2, 0, 4, 1, 3)).reshape(S, 5, N, chains)
    keys = jnp.concatenate([keys, keys], axis=-1).reshape(S, 5, RWKV_KEY_TILES, SUBLANES, lanes)

    def val_layout(z):
        zt = jnp.transpose(z.reshape(B, S, H, 2, RWKV_ROWS), (1, 4, 3, 0, 2))
        return zt.reshape(S, RWKV_ROW_GROUPS, SUBLANES, lanes)

    tc = _pick_tile(S, RWKV_TIME_CHUNK)
    kspec = pl.BlockSpec((tc, 5, RWKV_KEY_TILES, SUBLANES, lanes), lambda c: (c, 0, 0, 0, 0))
    vspec = pl.BlockSpec((tc, RWKV_ROW_GROUPS, SUBLANES, lanes), lambda c: (c, 0, 0, 0))
    y = pl.pallas_call(
        _rwkv_scan_kernel,
        grid=(S // tc,),
        in_specs=[kspec, vspec],
        out_specs=vspec,
        out_shape=jax.ShapeDtypeStruct((S, RWKV_ROW_GROUPS, SUBLANES, lanes), F32),
        scratch_shapes=[pltpu.VMEM((RWKV_ROWS, RWKV_KEY_TILES, SUBLANES, lanes), F32)],
        compiler_params=pltpu.CompilerParams(dimension_semantics=("arbitrary",),
                                             vmem_limit_bytes=V7X_VMEM_LIMIT_BYTES),
        name="rwkv7_scan",
    )(keys, val_layout(v))
    y = y.reshape(S, RWKV_ROWS, 2, B, H)
    return jnp.transpose(y, (3, 0, 4, 2, 1)).reshape(B, S, H, N)


ROUTE_TOKENS = 512


def _first_index_of_max(vals, index, n, axes):
    m = vals
    for ax in axes:
        m = jnp.max(m, axis=ax, keepdims=True)
    idx = jnp.where(vals == m, index, float(n))
    for ax in axes:
        idx = jnp.min(idx, axis=ax, keepdims=True)
    return m, idx


def _route_kernel(lg_ref, bias_ref, upper_ref, eidx_ref, wts_ref, rank_ref, cnt_ref, carry_ref):
    @pl.when(pl.program_id(0) == 0)
    def _():
        carry_ref[...] = jnp.zeros_like(carry_ref)

    E, tn = lg_ref.shape
    G = N_GROUPS
    J = E // G
    s3 = jax.nn.sigmoid(lg_ref[...]).reshape(G, J, tn)
    b3 = s3 + bias_ref[...].reshape(G, J, 1)
    j_io = lax.broadcasted_iota(jnp.int32, (G, J, tn), 1).astype(F32)
    g_io = lax.broadcasted_iota(jnp.int32, (G, 1, tn), 0).astype(F32)
    e_io = lax.broadcasted_iota(jnp.int32, (G, J, tn), 0).astype(F32) * J + j_io

    m1, first = _first_index_of_max(b3, j_io, J, (1,))
    m2 = jnp.max(jnp.where(j_io == first, -jnp.inf, b3), axis=1, keepdims=True)
    cur = m1 + m2
    keep = jnp.zeros((G, 1, tn), F32)
    for _ in range(TOPK_GROUPS):
        _, idx = _first_index_of_max(cur, g_io, G, (0,))
        hit = g_io == idx
        keep = jnp.where(hit, 1.0, keep)
        cur = jnp.where(hit, -jnp.inf, cur)

    cur = jnp.where(keep > 0.5, b3, NEG)
    hot = jnp.zeros((G, J, tn), F32)
    picks, weights = [], []
    for _ in range(TOP_K):
        _, idx = _first_index_of_max(cur, e_io, E, (0, 1))
        hit = e_io == idx
        w = jnp.sum(jnp.sum(jnp.where(hit, s3, 0.0), axis=0, keepdims=True), axis=1, keepdims=True)
        cur = jnp.where(hit, -jnp.inf, cur)
        hot = jnp.where(hit, 1.0, hot)
        picks.append(idx)
        weights.append(w)
    wsum = weights[0]
    for w in weights[1:]:
        wsum = wsum + w

    hot2 = hot.reshape(E, tn)
    before = jnp.dot(hot2.astype(BF16), upper_ref[...], preferred_element_type=F32) + carry_ref[:, 0:1]
    before3 = before.reshape(G, J, tn)
    for k in range(TOP_K):
        eidx_ref[k:k + 1, :] = picks[k].reshape(1, tn).astype(jnp.int32)
        wts_ref[k:k + 1, :] = (weights[k] / wsum * ROUTED_SCALE).reshape(1, tn)
        r = jnp.sum(jnp.sum(jnp.where(e_io == picks[k], before3, 0.0), axis=0, keepdims=True), axis=1, keepdims=True)
        rank_ref[k:k + 1, :] = r.reshape(1, tn).astype(jnp.int32)
    carry_ref[...] = carry_ref[...] + jnp.sum(hot2, axis=1, keepdims=True)
    cnt_ref[...] = carry_ref[...]


def _route(logits_t, bias):
    E, T = logits_t.shape
    tn = _pick_tile(T, ROUTE_TOKENS)
    upper = jnp.asarray(np.triu(np.ones((tn, tn), np.float32), 1), BF16)
    tok_spec = pl.BlockSpec((TOP_K, tn), lambda i: (0, i))
    eidx, wts, rank, cnt = pl.pallas_call(
        _route_kernel,
        grid=(T // tn,),
        in_specs=[pl.BlockSpec((E, tn), lambda i: (0, i)),
                  pl.BlockSpec((E, 1), lambda i: (0, 0)),
                  pl.BlockSpec((tn, tn), lambda i: (0, 0))],
        out_specs=[tok_spec, tok_spec, tok_spec, pl.BlockSpec((E, LANES), lambda i: (0, 0))],
        out_shape=[jax.ShapeDtypeStruct((TOP_K, T), jnp.int32), jax.ShapeDtypeStruct((TOP_K, T), F32),
                   jax.ShapeDtypeStruct((TOP_K, T), jnp.int32), jax.ShapeDtypeStruct((E, LANES), F32)],
        scratch_shapes=[pltpu.VMEM((E, LANES), F32)],
        compiler_params=pltpu.CompilerParams(dimension_semantics=("arbitrary",),
                                             vmem_limit_bytes=V7X_VMEM_LIMIT_BYTES),
        name="moe_route",
    )(logits_t, bias.reshape(E, 1), upper)
    return eidx, wts, rank, cnt[:, 0].astype(jnp.int32)


def _moe_kernel(be_ref, nused_ref, x_ref, wg_ref, wu_ref, wd_ref, o_ref, wg_s, wu_s, wd_s):
    i = pl.program_id(0)
    active = i < nused_ref[0]

    @pl.when(active & ((i == 0) | (be_ref[i] != be_ref[jnp.maximum(i - 1, 0)])))
    def _():
        wg_s[...] = wg_ref[0].astype(BF16)
        wu_s[...] = wu_ref[0].astype(BF16)
        wd_s[...] = wd_ref[0].astype(BF16)

    @pl.when(active)
    def _():
        x_lo, x_hi = (z.astype(BF16) for z in _unpack_bf16_halves(x_ref[...]))
        half = x_lo.shape[1]
        g = (jnp.dot(x_lo, wg_s[:half, :], preferred_element_type=F32)
             + jnp.dot(x_hi, wg_s[half:, :], preferred_element_type=F32))
        u = (jnp.dot(x_lo, wu_s[:half, :], preferred_element_type=F32)
             + jnp.dot(x_hi, wu_s[half:, :], preferred_element_type=F32))
        hmid = (g * jax.nn.sigmoid(g) * u).astype(BF16)
        o_ref[...] = _pack_bf16_halves(jnp.dot(hmid, wd_s[...], preferred_element_type=F32))

    @pl.when(jnp.logical_not(active))
    def _():
        o_ref[...] = jnp.zeros_like(o_ref)


def _moe_grouped(xg, blk_e, n_used, w_gate, w_up, w_down, layer):
    R = xg.shape[0]
    D = 2 * xg.shape[1]
    n_blk = R // MOE_BLOCK
    n_layers, E, _, F = w_gate.shape
    w_gate, w_up, w_down = (w.reshape((n_layers * E,) + w.shape[2:]) for w in (w_gate, w_up, w_down))

    def xmap(i, be, nu):
        return (jnp.minimum(i, jnp.maximum(nu[0] - 1, 0)), 0)

    def wmap(i, be, nu):
        return (layer * E + be[i], 0, 0)

    grid_spec = pltpu.PrefetchScalarGridSpec(
        num_scalar_prefetch=2,
        grid=(n_blk,),
        in_specs=[pl.BlockSpec((MOE_BLOCK, D // 2), xmap),
                  pl.BlockSpec((1, D, F), wmap),
                  pl.BlockSpec((1, D, F), wmap),
                  pl.BlockSpec((1, F, D), wmap)],
        out_specs=pl.BlockSpec((MOE_BLOCK, D // 2), lambda i, be, nu: (i, 0)),
        scratch_shapes=[pltpu.VMEM((D, F), BF16), pltpu.VMEM((D, F), BF16), pltpu.VMEM((F, D), BF16)],
    )
    return pl.pallas_call(
        _moe_kernel,
        grid_spec=grid_spec,
        out_shape=jax.ShapeDtypeStruct((R, D // 2), jnp.uint32),
        compiler_params=pltpu.CompilerParams(dimension_semantics=("arbitrary",),
                                             vmem_limit_bytes=V7X_VMEM_LIMIT_BYTES),
        name="moe_grouped_swiglu",
    )(blk_e, n_used, xg, w_gate, w_up, w_down)


MOE_COMBINE_ROWS = 256


def _moe_combine_kernel(a_ref, b_ref, res_ref, gate_ref, mine_ref, w_ref, o_ref):
    half = o_ref.shape[1] // 2
    y = jnp.dot(a_ref[...], b_ref[...], preferred_element_type=F32)
    lo_acc = y[:, :half]
    hi_acc = y[:, half:]
    for k in range(mine_ref.shape[0]):
        lo, hi = _unpack_bf16_halves(mine_ref[k])
        wk = w_ref[:, k:k + 1]
        lo_acc = lo_acc + wk * lo
        hi_acc = hi_acc + wk * hi
    g = gate_ref[0]
    o_ref[:, :half] = res_ref[:, :half] + g[:, :half] * lo_acc
    o_ref[:, half:] = res_ref[:, half:] + g[:, half:] * hi_acc


def _moe_combine(hmid, s_down, x_res, gate, seq_len, mine, wts):
    M, F = hmid.shape
    N = s_down.shape[1]
    n_k = mine.shape[0]
    tm = _pick_tile(seq_len, MOE_COMBINE_ROWS)
    row_tile = pl.BlockSpec((tm, N), lambda i: (i, 0))
    return pl.pallas_call(
        _moe_combine_kernel,
        grid=(M // tm,),
        in_specs=[pl.BlockSpec((tm, F), lambda i: (i, 0)),
                  pl.BlockSpec((F, N), lambda i: (0, 0)),
                  row_tile,
                  pl.BlockSpec((1, 1, N), lambda i: ((i * tm) // seq_len, 0, 0)),
                  pl.BlockSpec((n_k, tm, N // 2), lambda i: (0, i, 0)),
                  pl.BlockSpec((tm, n_k), lambda i: (i, 0))],
        out_specs=row_tile,
        out_shape=jax.ShapeDtypeStruct((M, N), F32),
        compiler_params=pltpu.CompilerParams(dimension_semantics=("parallel",),
                                             vmem_limit_bytes=V7X_VMEM_LIMIT_BYTES),
        name="moe_combine_residual",
    )(hmid, s_down, x_res, gate, mine, wts)


def _split(z, widths):
    cuts = [int(i) for i in np.cumsum(widths)[:-1]]
    return jnp.split(z, cuts, axis=-1)


def _rms(z, g, eps=NORM_EPS):
    return z * lax.rsqrt(jnp.mean(z * z, axis=-1, keepdims=True) + eps) * g


def _heads(z, n_heads):
    B, S, C = z.shape
    return z.reshape(B, S, n_heads, C // n_heads).transpose(0, 2, 1, 3)


def _merge_heads(z):
    B, H, S, d = z.shape
    return z.transpose(0, 2, 1, 3).reshape(B, S, H * d)


def _rope(z, pos):
    half = ROPE_DIM // 2
    inv_freq = ROPE_THETA ** (-jnp.arange(half, dtype=F32) / half)
    ang = pos.astype(F32)[:, None] * inv_freq[None, :]
    cos = jnp.cos(ang)
    sin = jnp.sin(ang)
    z1, z2, zr = z[..., :half], z[..., half:ROPE_DIM], z[..., ROPE_DIM:]
    return jnp.concatenate([z1 * cos - z2 * sin, z2 * cos + z1 * sin, zr], axis=-1)


def _token_shift(z):
    return jnp.pad(z, ((0, 0), (1, 0), (0, 0)))[:, :-1]


def _rwkv7(p, v_first, v_mix, mu, w0, w_up, a0, a_up, g_up, k_k, k_a, r_k, ln_w, ln_b):
    B, S, _ = p.shape
    H, N = RWKV_HEADS, RWKV_HEAD_DIM
    xs = p + (_token_shift(p) - p) * mu
    r, k, v, xw, xa, xg = _split(xs, (RWKV_DIM, RWKV_DIM, RWKV_DIM, RWKV_DECAY_LORA, RWKV_ICLR_LORA, RWKV_GATE_LORA))
    w = -jax.nn.softplus(-(w0 + jnp.tanh(xw) @ w_up)) - 0.5
    decay = jnp.exp(-jnp.exp(w))
    a = jax.nn.sigmoid(a0 + xa @ a_up)
    g = jax.nn.sigmoid(xg) @ g_up
    kk = (k * k_k).reshape(B, S, H, N)
    kk = kk / jnp.maximum(jnp.sqrt(jnp.sum(kk * kk, axis=-1, keepdims=True)), 1e-12)
    k = k * (1.0 + (a - 1.0) * k_a)
    if v_mix is None:
        v_first = v
    else:
        v0, v_down, v_up = v_mix
        v = v + (v_first - v) * jax.nn.sigmoid(v0 + (v @ v_down) @ v_up)

    def hd(z):
        return z.reshape(B, S, H, N)

    y = _rwkv_scan(hd(r), hd(decay), hd(k), hd(v), kk, kk * hd(a))
    mean = jnp.mean(y, axis=-1, keepdims=True)
    var = jnp.mean(jnp.square(y - mean), axis=-1, keepdims=True)
    y = ((y - mean) * lax.rsqrt(var + RWKV_GN_EPS)).reshape(B, S, RWKV_DIM) * ln_w + ln_b
    bonus = jnp.sum((r * k * r_k).reshape(B, S, H, N), axis=-1, keepdims=True) * v.reshape(B, S, H, N)
    y = (y + bonus.reshape(B, S, RWKV_DIM)) * g
    return y, v_first


def _mlstm_chunkwise(q, k, v, i_pre, logf):
    B, H, S, dk = q.shape
    dv = v.shape[-1]
    L = MLSTM_CHUNK
    NC = S // L
    q = q.reshape(B, H, NC, L, dk)
    k = k.reshape(B, H, NC, L, dk)
    v = v.reshape(B, H, NC, L, dv)
    ic = i_pre.reshape(B, H, NC, L)
    bcum = jnp.cumsum(logf.reshape(B, H, NC, L), axis=-1)
    gtot = bcum[..., -1]
    a_log = gtot[..., None] - bcum + ic
    a_max = jnp.max(a_log, axis=-1)

    def step(carry, inp):
        C, n, m = carry
        g_c, a_c, amax_c, k_c, v_c = inp
        m_new = jnp.maximum(g_c + m, amax_c)
        wgt = jnp.exp(a_c - m_new[..., None])
        dec = jnp.exp(g_c + m - m_new)
        C_new = dec[..., None, None] * C + jnp.einsum('bhl,bhlk,bhlv->bhkv', wgt, k_c, v_c)
        n_new = dec[..., None] * n + jnp.einsum('bhl,bhlk->bhk', wgt, k_c)
        return (C_new, n_new, m_new), (C, n, m)

    init = (jnp.zeros((B, H, dk, dv), F32), jnp.zeros((B, H, dk), F32), jnp.zeros((B, H), F32))
    xs = (jnp.moveaxis(gtot, 2, 0), jnp.moveaxis(a_log, 2, 0), jnp.moveaxis(a_max, 2, 0),
          jnp.moveaxis(k, 2, 0), jnp.moveaxis(v, 2, 0))
    _, (C_prev, n_prev, m_prev) = lax.scan(step, init, xs)
    C_prev = jnp.moveaxis(C_prev, 0, 2)
    n_prev = jnp.moveaxis(n_prev, 0, 2)
    m_prev = jnp.moveaxis(m_prev, 0, 2)

    causal = jnp.tril(jnp.ones((L, L), dtype=bool))
    logD = jnp.where(causal, bcum[..., :, None] - bcum[..., None, :] + ic[..., None, :], -jnp.inf)
    m_inter = bcum + m_prev[..., None]
    m_t = jnp.maximum(m_inter, jnp.max(logD, axis=-1))
    Dm = jnp.exp(logD - m_t[..., None])
    sqk = jnp.einsum('bhctd,bhcsd->bhcts', q, k) * Dm
    inter = jnp.exp(m_inter - m_t)
    num = jnp.einsum('bhcts,bhcsv->bhctv', sqk, v) + inter[..., None] * jnp.einsum('bhctk,bhckv->bhctv', q, C_prev)
    den = jnp.sum(sqk, axis=-1) + inter * jnp.einsum('bhctk,bhck->bhct', q, n_prev)
    h = num / jnp.maximum(jnp.abs(den), jnp.exp(-m_t))[..., None]
    return h.reshape(B, H, S, dv)


def _mlstm(p, p_gates, conv_w, conv_b, i_b, f_b, norm_g):
    H = MLSTM_HEADS
    qk, v, o = _split(p, (2 * MLSTM_QK_WIDTH, MLSTM_DIM, MLSTM_DIM))
    ig, fg = _split(p_gates, (H, H))
    S = qk.shape[1]
    qkp = jnp.pad(qk, ((0, 0), (MLSTM_CONV - 1, 0), (0, 0)))
    conv = conv_b
    for j in range(MLSTM_CONV):
        conv = conv + qkp[:, j:j + S] * conv_w[j]
    qk = jax.nn.silu(conv)
    q, k = jnp.split(qk, 2, axis=-1)
    q = _heads(q, H) * (MLSTM_QK_DIM ** -0.5)
    k = _heads(k, H)
    v = _heads(v, H)
    i_pre = jnp.moveaxis(ig + i_b, -1, 1)
    logf = jax.nn.log_sigmoid(jnp.moveaxis(fg + f_b, -1, 1))
    h = _mlstm_chunkwise(q, k, v, i_pre, logf)
    h = _rms(h, norm_g.reshape(H, 1, MLSTM_V_DIM))
    return jax.nn.sigmoid(o) * _merge_heads(h)


def _moba(q, k, v, k_mean):
    B, H, S, d = q.shape
    BLK = MOBA_BLOCK
    nb = S // BLK
    pos = jnp.arange(S)
    qblk = pos // BLK
    gate = jnp.einsum('bhsd,bnhd->bhsn', q.astype(F32), k_mean, precision=HIGHEST)
    gate = jnp.where(jnp.arange(nb)[None, :] < qblk[:, None], gate, NEG)
    n_top = min(MOBA_TOPK, nb)
    _, sel = lax.top_k(gate, n_top)
    slot_ok = jnp.arange(n_top)[None, :] < qblk[:, None]
    lanes = jnp.arange(LANES)
    hit = (sel[..., None] == lanes) & slot_ok[:, :, None]
    blk_mask = jnp.any(hit, axis=3) | (lanes[None, :] == qblk[:, None])
    return _attention(q, k, v, MOBA_BLOCK, d ** -0.5, block_mask=blk_mask.astype(BF16), block=BLK)


def _nsa(qb, k_slc, v_slc, k_win, v_win, kc_in, vc_in, gl, pos, k_norm_cmp, cmp_pe, cmp_w1, cmp_b1, cmp_w2):
    B, H, S, d = qb.shape
    scale = d ** -0.5

    ncmp = (S - NSA_CMP_BLOCK) // NSA_CMP_STRIDE + 1
    cstart = NSA_CMP_STRIDE * jnp.arange(ncmp)
    cend = cstart + NSA_CMP_BLOCK - 1
    cidx = cstart[:, None] + jnp.arange(NSA_CMP_BLOCK)[None, :]

    def compress(z, i):
        blocks = (z[:, cidx] + cmp_pe[i]).reshape(B, ncmp, NSA_CMP_BLOCK * d)
        return jax.nn.gelu(blocks @ cmp_w1[i] + cmp_b1[i]) @ cmp_w2[i]

    k_cmp = _rope(_rms(compress(kc_in, 0), k_norm_cmp), cend)
    v_cmp = compress(vc_in, 1)
    SB = NSA_SLC_BLOCK
    nsb = S // SB
    assert nsb <= LANES
    ncp = -(-ncmp // LANES) * LANES
    cs_np = NSA_CMP_STRIDE * np.arange(ncp)
    ss_np = SB * np.arange(LANES)
    overlap = ((cs_np[:, None] <= ss_np[None, :] + SB - 1) & (cs_np[:, None] + NSA_CMP_BLOCK - 1 >= ss_np[None, :])
               & (np.arange(ncp)[:, None] < ncmp) & (np.arange(LANES)[None, :] < nsb))
    cpad = ((0, 0), (0, ncp - ncmp), (0, 0))
    o_cmp, imp = _nsa_cmp_attention(qb, jnp.pad(k_cmp, cpad).astype(BF16), jnp.pad(v_cmp, cpad).astype(BF16),
                                    jnp.asarray(overlap, F32), ncmp, scale)
    imp = imp[:, :, :nsb]
    cur = pos // SB
    blk = jnp.arange(nsb)[None, :]
    forced = (blk == 0) | (blk == cur[:, None]) | (blk == cur[:, None] - 1)
    imp = jnp.where(forced, BIG, jnp.where(blk <= cur[:, None], imp, NEG))
    n_sel = min(NSA_SLC_TOPK, nsb)
    _, sel = lax.top_k(imp, n_sel)
    blk_mask = jnp.any(sel[..., None] == jnp.arange(LANES), axis=2)
    o_slc = _attention(qb, k_slc, v_slc, NSA_Q_TILE, scale, block_mask=blk_mask[:, None].astype(BF16), block=SB)
    o_win = _attention(qb, k_win, v_win, NSA_Q_TILE, scale, window=NSA_WINDOW)

    gates = jax.nn.sigmoid(gl).reshape(B, S, H, 3).transpose(0, 2, 1, 3)
    o = gates[..., 0:1] * o_cmp + gates[..., 1:2] * o_slc + gates[..., 2:3] * o_win
    return _merge_heads(o)


def _moe(xb, logits, x_packed, x_res, gate, seq_len, bias, w_gate, w_up, w_down, layer, s_gate, s_up, s_down):
    T, D = xb.shape
    A = T * TOP_K
    eidx, wts, rank, counts = _route(logits, bias)

    experts = jnp.arange(N_EXPERTS, dtype=jnp.int32)
    padded = (counts + MOE_BLOCK - 1) // MOE_BLOCK * MOE_BLOCK
    pstart = jnp.cumsum(padded) - padded
    slot = rank + jnp.sum(jnp.where(eidx[None] == experts[:, None, None], pstart[:, None, None], 0), axis=0)

    n_blk = -(-(A + N_EXPERTS * (MOE_BLOCK - 1)) // MOE_BLOCK)
    R = n_blk * MOE_BLOCK
    blk_e = jnp.minimum(jnp.sum((pstart + padded)[None, :] <= (jnp.arange(n_blk) * MOE_BLOCK)[:, None], axis=1),
                        N_EXPERTS - 1).astype(jnp.int32)
    n_used = (jnp.sum(padded) // MOE_BLOCK).astype(jnp.int32).reshape(1)
    tok_bits = max(int(T - 1).bit_length(), 1)
    tok_ids = jnp.arange(T, dtype=jnp.int32)[None, :]
    real_keys = (eidx << (tok_bits + 1)) | tok_ids
    n_fill = MOE_BLOCK - 1
    fill_j = jnp.arange(n_fill, dtype=jnp.int32)[None, :]
    unused = jnp.int32(N_EXPERTS << (tok_bits + 1))
    fill_keys = jnp.where(fill_j < (padded - counts)[:, None], (experts[:, None] << (tok_bits + 1)) | (1 << tok_bits), unused)
    tail = jnp.full((R - A - N_EXPERTS * n_fill,), unused, jnp.int32)
    keys = lax.sort(jnp.concatenate([real_keys.reshape(A), fill_keys.reshape(-1), tail]))
    is_fill = ((keys >> tok_bits) & 1 == 1) | (keys >= unused)
    tok_buf = jnp.where(is_fill, jnp.arange(R, dtype=jnp.int32) % T, keys & ((1 << tok_bits) - 1))

    packed = _moe_grouped(x_packed[tok_buf], blk_e, n_used, w_gate, w_up, w_down, layer)
    mine = packed[slot.reshape(A)].reshape(TOP_K, T, D // 2)

    gu = _matmul(xb, jnp.concatenate([s_gate, s_up], axis=1).astype(BF16))
    F = s_gate.shape[1]
    hmid = (jax.nn.silu(gu[:, :F]) * gu[:, F:]).astype(BF16)
    return _moe_combine(hmid, s_down.astype(BF16), x_res, gate, seq_len, mine, wts.T)


def _projection_layout():
    o_b = RWKV_IN_DIM
    o_c = o_b + MLSTM_IN_DIM
    o_d = o_c + MOBA_IN_DIM
    o_g = o_d + NSA_IN_DIM
    mlstm_main = MLSTM_IN_DIM - 2 * MLSTM_HEADS
    nsa_main = NSA_IN_DIM - 3 * NSA_HEADS
    spans = (('rwkv', 0, o_b), ('mlstm', o_b, o_b + mlstm_main), ('moba', o_c, o_d), ('nsa', o_d, o_d + nsa_main),
             ('mlstm_gates', o_b + mlstm_main, o_c), ('nsa_gates', o_d + nsa_main, o_g))
    pieces, cur = [], 0
    for name, lo, hi in spans:
        pieces.append((name, lo, hi, cur))
        cur += -(-(hi - lo) // LANES) * LANES
    return tuple(pieces), cur, o_g


def _relayout_kernel(w_ref, mix_ref, gates_ref, *, pieces, gate_lo):
    for _, lo, hi, off in pieces:
        width = hi - lo
        mix_ref[:, off:off + width] = w_ref[0, :, lo:hi].astype(BF16)
        pad = -width % LANES
        if pad:
            mix_ref[:, off + width:off + width + pad] = jnp.zeros((mix_ref.shape[0], pad), BF16)
    gates_ref[...] = w_ref[0, :, gate_lo:].astype(BF16)


def _projection_weights(w_in, layer, rows=128):
    _, D, n_in = w_in.shape
    pieces, mix_width, gate_lo = _projection_layout()
    w_mix, w_gates = pl.pallas_call(
        functools.partial(_relayout_kernel, pieces=pieces, gate_lo=gate_lo),
        grid=(D // rows,),
        in_specs=[pl.BlockSpec((1, rows, n_in), lambda i: (layer, i, 0))],
        out_specs=[pl.BlockSpec((rows, mix_width), lambda i: (i, 0)),
                   pl.BlockSpec((rows, n_in - gate_lo), lambda i: (i, 0))],
        out_shape=[jax.ShapeDtypeStruct((D, mix_width), BF16), jax.ShapeDtypeStruct((D, n_in - gate_lo), BF16)],
        compiler_params=pltpu.CompilerParams(dimension_semantics=("parallel",),
                                             vmem_limit_bytes=V7X_VMEM_LIMIT_BYTES),
        name="projection_weight_relayout",
    )(w_in)
    return w_mix, w_gates, {name: off for name, _, _, off in pieces}


def kernel(x, c, ada_w, ada_b, norm_mix_g, norm_ffn_g, w_in, rwkv_mu, rwkv_w0, rwkv_w_up, rwkv_a0, rwkv_a_up, rwkv_g_up, rwkv_k_k, rwkv_k_a, rwkv_r_k, rwkv_ln_w, rwkv_ln_b, rwkv_v0, rwkv_v_down, rwkv_v_up, mlstm_conv_w, mlstm_conv_b, mlstm_i_b, mlstm_f_b, mlstm_norm_g, moba_q_norm, moba_k_norm, nsa_q_norm, nsa_k_norm, nsa_cmp_pe, nsa_cmp_w1, nsa_cmp_b1, nsa_cmp_w2, w_branch, w_out, moe_router, moe_bias, moe_w_gate, moe_w_up, moe_w_down, shared_w_gate, shared_w_up, shared_w_down):
    B, S, D = x.shape
    T = B * S
    depth = ada_w.shape[0]
    pos = jnp.arange(S)
    cond = jax.nn.silu(c)
    v_first = None
    x2d = x.reshape(T, D)
    for l in range(depth):
        mod = jnp.dot(cond, ada_w[l], precision=HIGHEST) + ada_b[l]
        sh_mix, sc_mix, gate_mix, sh_ffn, sc_ffn, gate_ffn = [z[:, None, :] for z in jnp.split(mod, 6, axis=-1)]

        hb = _normmod(x2d, norm_mix_g[l], sc_mix, sh_mix, S)
        w_mix, w_gates, offs = _projection_weights(w_in, l)
        p_gate = _matmul(hb, w_gates, out_dtype=BF16)

        def proj(name, width, dtype=F32, n_pieces=1):
            padded = -(-width // LANES) * LANES * n_pieces
            return _matmul(hb, w_mix, out_dtype=dtype, cols=(offs[name], padded)).reshape(B, S, padded)

        small = proj('mlstm_gates', 2 * MLSTM_HEADS, n_pieces=2)
        nsa_gate_off = offs['nsa_gates'] - offs['mlstm_gates']
        v_mix = None if l == 0 else (rwkv_v0[l - 1], rwkv_v_down[l - 1], rwkv_v_up[l - 1])
        y_a, v_first = _rwkv7(proj('rwkv', RWKV_IN_DIM)[:, :, :RWKV_IN_DIM], v_first, v_mix, rwkv_mu[l], rwkv_w0[l],
                              rwkv_w_up[l], rwkv_a0[l], rwkv_a_up[l], rwkv_g_up[l], rwkv_k_k[l], rwkv_k_a[l],
                              rwkv_r_k[l], rwkv_ln_w[l], rwkv_ln_b[l])
        y_b = _mlstm(proj('mlstm', MLSTM_IN_DIM - 2 * MLSTM_HEADS), small[:, :, :2 * MLSTM_HEADS],
                     mlstm_conv_w[l], mlstm_conv_b[l], mlstm_i_b[l], mlstm_f_b[l], mlstm_norm_g[l])
        p_moba = proj('moba', MOBA_IN_DIM, BF16).reshape(T, MOBA_IN_DIM)
        hm = MOBA_HEADS
        q_c, k_c, v_c, k_mean = _head_prep(p_moba, jnp.stack([moba_q_norm[l], moba_k_norm[l]]),
                                           ((hm, 0, 0), (hm, hm, 1), (hm, 2 * hm, None)), S, with_mean=True)
        y_c = _merge_heads(_moba(q_c, k_c, v_c, k_mean))
        nsa_main = NSA_IN_DIM - 3 * NSA_HEADS
        p_nsa = proj('nsa', nsa_main, BF16)
        hn = NSA_HEADS
        q_d, ks_d, vs_d, kw_d, vw_d = _head_prep(
            p_nsa.reshape(T, nsa_main), jnp.stack([nsa_q_norm[l], nsa_k_norm[l, 1], nsa_k_norm[l, 2]]),
            ((hn, 0, 0), (1, hn + 2, 1), (1, hn + 3, None), (1, hn + 4, 2), (1, hn + 5, None)), S)
        cmp_in = p_nsa[:, :, NSA_DIM:NSA_DIM + 2 * ATTN_HEAD_DIM].astype(F32)
        y_d = _nsa(q_d, ks_d, vs_d, kw_d, vw_d, cmp_in[:, :, :ATTN_HEAD_DIM], cmp_in[:, :, ATTN_HEAD_DIM:],
                   small[:, :, nsa_gate_off:nsa_gate_off + 3 * NSA_HEADS], pos, nsa_k_norm[l, 0],
                   nsa_cmp_pe[l], nsa_cmp_w1[l], nsa_cmp_b1[l], nsa_cmp_w2[l])
        ys = [y.reshape(T, BRANCH_DIM).astype(BF16) for y in (y_a, y_b, y_c, y_d)]
        merged = _merge_branches(p_gate, ys, w_branch[l].astype(BF16))
        x2d = _matmul_residual(merged, w_out[l].astype(BF16), x2d, gate_mix, S)

        hb, logits, hb_packed = _normmod(x2d, norm_ffn_g[l], sc_ffn, sh_ffn, S, router=moe_router[l])
        x2d = _moe(hb, logits, hb_packed, x2d, gate_ffn, S, moe_bias[l], moe_w_gate, moe_w_up, moe_w_down, l,
                   shared_w_gate[l], shared_w_up[l], shared_w_down[l])
    return x2d.reshape(B, S, D)
```

```python
import functools

import numpy as np
import jax
import jax.numpy as jnp
from jax import lax
from jax.experimental import pallas as pl
from jax.experimental.pallas import tpu as pltpu

F32 = jnp.float32
BF16 = jnp.bfloat16
HIGHEST = lax.Precision.HIGHEST

D_MODEL = 2048
N_MIXERS = 4
BRANCH_DIM = 512
NORM_EPS = 1e-6
NEG = -1e30
BIG = 1e9

RWKV_HEADS = 8
RWKV_HEAD_DIM = 64
RWKV_DIM = RWKV_HEADS * RWKV_HEAD_DIM
RWKV_DECAY_LORA = 96
RWKV_ICLR_LORA = 96
RWKV_GATE_LORA = 256
RWKV_GN_EPS = 64e-5
RWKV_IN_DIM = 3 * RWKV_DIM + RWKV_DECAY_LORA + RWKV_ICLR_LORA + RWKV_GATE_LORA

MLSTM_HEADS = 4
MLSTM_QK_DIM = 128
MLSTM_V_DIM = 128
MLSTM_QK_WIDTH = MLSTM_HEADS * MLSTM_QK_DIM
MLSTM_DIM = MLSTM_HEADS * MLSTM_V_DIM
MLSTM_CONV = 4
MLSTM_CHUNK = 64
MLSTM_IN_DIM = 2 * MLSTM_QK_WIDTH + 2 * MLSTM_DIM + 2 * MLSTM_HEADS

ATTN_HEAD_DIM = 128
ROPE_DIM = ATTN_HEAD_DIM // 4
ROPE_THETA = 500000.0

MOBA_HEADS = 4
MOBA_DIM = MOBA_HEADS * ATTN_HEAD_DIM
MOBA_BLOCK = 256
MOBA_TOPK = 3
MOBA_IN_DIM = 3 * MOBA_DIM

NSA_HEADS = 4
NSA_DIM = NSA_HEADS * ATTN_HEAD_DIM
NSA_CMP_BLOCK = 32
NSA_CMP_STRIDE = 16
NSA_SLC_BLOCK = 64
NSA_SLC_TOPK = 16
NSA_WINDOW = 512
NSA_IN_DIM = NSA_DIM + 6 * ATTN_HEAD_DIM + 3 * NSA_HEADS

MERGE_IN_DIM = N_MIXERS * D_MODEL
IN_WIDTHS = (RWKV_IN_DIM, MLSTM_IN_DIM, MOBA_IN_DIM, NSA_IN_DIM, MERGE_IN_DIM)

N_EXPERTS = 64
TOP_K = 8
N_GROUPS = 8
TOPK_GROUPS = 4
EXPERT_DIM = 512
ROUTED_SCALE = 2.5
MOE_BLOCK = 512

V7X_VMEM_LIMIT_BYTES = 48 * 1024 * 1024
LANES = 128
SUBLANES = 8


def _mm_kernel(a_ref, b_ref, o_ref):
    o_ref[...] = jnp.dot(a_ref[...], b_ref[...], preferred_element_type=F32).astype(o_ref.dtype)


def _pick_tile(n, pref):
    t = min(pref, n)
    while n % t:
        t //= 2
    return t


def _pick_lane_tile(n, pref):
    assert n % LANES == 0
    units = n // LANES
    best = 1
    for u in range(1, units + 1):
        if units % u == 0 and u * LANES <= pref:
            best = u
    return best * LANES


def _matmul(a, b, out_dtype=F32, tm=1024, tn=1024, cols=None):
    M, K = a.shape
    off, N = (0, b.shape[1]) if cols is None else cols
    tm = _pick_tile(M, tm)
    tn = _pick_lane_tile(int(np.gcd(N, off)) if off else N, tn)
    first = off // tn
    return pl.pallas_call(
        _mm_kernel,
        grid=(N // tn, M // tm),
        in_specs=[pl.BlockSpec((tm, K), lambda j, i: (i, 0)),
                  pl.BlockSpec((K, tn), lambda j, i: (0, first + j))],
        out_specs=pl.BlockSpec((tm, tn), lambda j, i: (i, j)),
        out_shape=jax.ShapeDtypeStruct((M, N), out_dtype),
        compiler_params=pltpu.CompilerParams(dimension_semantics=("parallel", "parallel"),
                                             vmem_limit_bytes=V7X_VMEM_LIMIT_BYTES),
        name="dense_matmul",
    )(a, b)


def _mm_res_kernel(*refs, has_extra):
    if has_extra:
        a_ref, b_ref, res_ref, gate_ref, extra_ref, o_ref = refs
    else:
        a_ref, b_ref, res_ref, gate_ref, o_ref = refs
    y = jnp.dot(a_ref[...], b_ref[...], preferred_element_type=F32)
    if has_extra:
        y = y + extra_ref[...]
    o_ref[...] = res_ref[...] + gate_ref[0] * y


def _matmul_residual(a, b, res, gate, seq_len, extra=None, tm=1024, tn=1024):
    M, K = a.shape
    _, N = b.shape
    tm = _pick_tile(seq_len, tm)
    tn = _pick_lane_tile(N, tn)
    row_tile = pl.BlockSpec((tm, tn), lambda j, i: (i, j))
    in_specs = [pl.BlockSpec((tm, K), lambda j, i: (i, 0)),
                pl.BlockSpec((K, tn), lambda j, i: (0, j)),
                row_tile,
                pl.BlockSpec((1, 1, tn), lambda j, i: ((i * tm) // seq_len, 0, j))]
    args = [a, b, res, gate]
    if extra is not None:
        in_specs.append(row_tile)
        args.append(extra)
    return pl.pallas_call(
        functools.partial(_mm_res_kernel, has_extra=extra is not None),
        grid=(N // tn, M // tm),
        in_specs=in_specs,
        out_specs=row_tile,
        out_shape=jax.ShapeDtypeStruct((M, N), F32),
        compiler_params=pltpu.CompilerParams(dimension_semantics=("parallel", "parallel"),
                                             vmem_limit_bytes=V7X_VMEM_LIMIT_BYTES),
        name="matmul_gated_residual",
    )(*args)


def _normmod_kernel(*refs, with_router):
    if with_router:
        x_ref, g_ref, sc_ref, sh_ref, rt_ref, o_ref, lg_ref, pk_ref = refs
    else:
        x_ref, g_ref, sc_ref, sh_ref, o_ref = refs
    x = x_ref[...]
    y = x * lax.rsqrt(jnp.mean(x * x, axis=-1, keepdims=True) + NORM_EPS) * g_ref[...]
    h = y * (1.0 + sc_ref[0]) + sh_ref[0]
    o_ref[...] = h.astype(o_ref.dtype)
    if with_router:
        lg_ref[...] = lax.dot_general(rt_ref[...], h, (((1,), (1,)), ((), ())),
                                      preferred_element_type=F32, precision=HIGHEST)
        pk_ref[...] = _pack_bf16_halves(h)


def _pack_bf16_halves(y):
    half = y.shape[1] // 2
    lo = pltpu.bitcast(y[:, :half].astype(BF16).astype(F32), jnp.uint32) >> 16
    hi = pltpu.bitcast(y[:, half:].astype(BF16).astype(F32), jnp.uint32) & jnp.uint32(0xFFFF0000)
    return hi | lo


def _unpack_bf16_halves(word):
    return pltpu.bitcast(word << 16, F32), pltpu.bitcast(word & jnp.uint32(0xFFFF0000), F32)


def _normmod(x2d, g, scale, shift, seq_len, router=None, tm=512):
    M, D = x2d.shape
    tm = _pick_tile(seq_len, tm)
    mod_spec = pl.BlockSpec((1, 1, D), lambda i: ((i * tm) // seq_len, 0, 0))
    row_spec = pl.BlockSpec((tm, D), lambda i: (i, 0))
    in_specs = [row_spec, pl.BlockSpec((1, D), lambda i: (0, 0)), mod_spec, mod_spec]
    args = [x2d, g.reshape(1, D), scale, shift]
    out_shape = [jax.ShapeDtypeStruct((M, D), BF16)]
    out_specs = [row_spec]
    if router is not None:
        n_exp = router.shape[1]
        in_specs.append(pl.BlockSpec((n_exp, D), lambda i: (0, 0)))
        args.append(router.T)
        out_shape.append(jax.ShapeDtypeStruct((n_exp, M), F32))
        out_specs.append(pl.BlockSpec((n_exp, tm), lambda i: (0, i)))
        out_shape.append(jax.ShapeDtypeStruct((M, D // 2), jnp.uint32))
        out_specs.append(pl.BlockSpec((tm, D // 2), lambda i: (i, 0)))
    out = pl.pallas_call(
        functools.partial(_normmod_kernel, with_router=router is not None),
        grid=(M // tm,),
        in_specs=in_specs,
        out_specs=out_specs,
        out_shape=out_shape,
        compiler_params=pltpu.CompilerParams(dimension_semantics=("parallel",),
                                             vmem_limit_bytes=V7X_VMEM_LIMIT_BYTES),
        name="rmsnorm_modulate",
    )(*args)
    return tuple(out) if router is not None else out[0]


def _merge_kernel(h_ref, wg0, wg1, wg2, wg3, y0, y1, y2, y3, wb_ref, o_ref):
    h = h_ref[...]
    acc = None
    for m, (wg_ref, y_ref) in enumerate(((wg0, y0), (wg1, y1), (wg2, y2), (wg3, y3))):
        gate = jax.nn.sigmoid(jnp.dot(h, wg_ref[...], preferred_element_type=F32))
        t = jnp.dot(y_ref[...], wb_ref[m], preferred_element_type=F32) * gate
        acc = t if acc is None else acc + t
    o_ref[...] = acc.astype(o_ref.dtype)


def _merge_branches(hb, w_gates, ys, w_branch, tm=1024, tn=512):
    M, K = hb.shape
    n_mix, kb, D = w_branch.shape
    tm = _pick_tile(M, tm)
    tn = _pick_lane_tile(D, tn)
    nj = D // tn
    gate_specs = [pl.BlockSpec((K, tn), functools.partial(lambda j, i, m: (0, m * nj + j), m=m)) for m in range(n_mix)]
    y_specs = [pl.BlockSpec((tm, kb), lambda j, i: (i, 0)) for _ in range(n_mix)]
    return pl.pallas_call(
        _merge_kernel,
        grid=(nj, M // tm),
        in_specs=[pl.BlockSpec((tm, K), lambda j, i: (i, 0))] + gate_specs + y_specs
                 + [pl.BlockSpec((n_mix, kb, tn), lambda j, i: (0, 0, j))],
        out_specs=pl.BlockSpec((tm, tn), lambda j, i: (i, j)),
        out_shape=jax.ShapeDtypeStruct((M, D), BF16),
        compiler_params=pltpu.CompilerParams(dimension_semantics=("parallel", "parallel"),
                                             vmem_limit_bytes=V7X_VMEM_LIMIT_BYTES),
        name="merge_gated_branches",
    )(hb, *([w_gates] * n_mix), *ys, w_branch)


HEAD_PREP_ROWS = 256


def _rope_tables(seq_len):
    half = ROPE_DIM // 2
    inv_freq = ROPE_THETA ** (-jnp.arange(half, dtype=F32) / half)
    ang = jnp.arange(seq_len, dtype=F32)[:, None] * inv_freq[None, :]
    cos, sin = jnp.cos(ang), jnp.sin(ang)
    rest = LANES - ROPE_DIM
    cosf = jnp.concatenate([cos, cos, jnp.ones((seq_len, rest), F32)], axis=1)
    sinf = jnp.concatenate([-sin, sin, jnp.zeros((seq_len, rest), F32)], axis=1)
    return cosf, sinf


def _head_prep_kernel(*refs, groups, with_mean):
    p_ref, g_ref, cos_ref, sin_ref = refs[:4]
    out_refs = refs[4:]
    half = ROPE_DIM // 2
    lane = lax.broadcasted_iota(jnp.int32, (p_ref.shape[0], LANES), 1)
    cosf = cos_ref[...]
    sinf = sin_ref[...]
    for gi, (n_heads, first_tile, gain_row) in enumerate(groups):
        for h in range(n_heads):
            c0 = (first_tile + h) * LANES
            z = p_ref[:, c0:c0 + LANES]
            if gain_row is not None:
                z = z.astype(F32)
                z = z * lax.rsqrt(jnp.mean(z * z, axis=-1, keepdims=True) + NORM_EPS) * g_ref[gain_row:gain_row + 1, :]
                partner = jnp.where(lane < half, pltpu.roll(z, LANES - half, axis=1), pltpu.roll(z, half, axis=1))
                z = z * cosf + partner * sinf
                if with_mean and gi == 1:
                    out_refs[-1][0, 0, h:h + 1, :] = jnp.mean(z, axis=0, keepdims=True)
            out_refs[gi][0, h] = z.astype(BF16)


def _head_prep(p, gains, groups, seq_len, with_mean=False):
    M, W = p.shape
    B = M // seq_len
    tm = _pick_tile(seq_len, HEAD_PREP_ROWS)
    per_seq = seq_len // tm
    cosf, sinf = _rope_tables(seq_len)
    rope_spec = pl.BlockSpec((tm, LANES), lambda i: (i % per_seq, 0))
    out_shape = [jax.ShapeDtypeStruct((B, n, seq_len, LANES), BF16) for n, _, _ in groups]
    out_specs = [pl.BlockSpec((1, n, tm, LANES), lambda i: (i // per_seq, 0, i % per_seq, 0)) for n, _, _ in groups]
    if with_mean:
        n = groups[1][0]
        out_shape.append(jax.ShapeDtypeStruct((B, per_seq, n, LANES), F32))
        out_specs.append(pl.BlockSpec((1, 1, n, LANES), lambda i: (i // per_seq, i % per_seq, 0, 0)))
    return pl.pallas_call(
        functools.partial(_head_prep_kernel, groups=tuple(groups), with_mean=with_mean),
        grid=(M // tm,),
        in_specs=[pl.BlockSpec((tm, W), lambda i: (i, 0)),
                  pl.BlockSpec(gains.shape, lambda i: (0, 0)),
                  rope_spec, rope_spec],
        out_specs=out_specs,
        out_shape=out_shape,
        compiler_params=pltpu.CompilerParams(dimension_semantics=("parallel",),
                                             vmem_limit_bytes=V7X_VMEM_LIMIT_BYTES),
        name="attention_head_prep",
    )(p, gains, cosf, sinf)


ATTN_KEY_CHUNK = 512
NSA_Q_TILE = 128
MASKED_SCORE = -1e30
RUNNING_MAX_FLOOR = -1e20


def _attn_kernel(*refs, tqt, n_rep, kc, window, use_bm, scale):
    if use_bm:
        q_ref, k_ref, v_ref, bm_ref, e_ref, o_ref = refs
    else:
        q_ref, k_ref, v_ref, o_ref = refs
    t0 = pl.program_id(2) * tqt
    d = q_ref.shape[-1]
    rows = n_rep * tqt
    q = q_ref[0].reshape(rows, d)
    dmat = lax.broadcasted_iota(jnp.int32, (tqt, kc), 1) - lax.broadcasted_iota(jnp.int32, (tqt, kc), 0)

    def body(c, carry):
        m, l, acc = carry
        k0 = pl.multiple_of(c * kc, kc)
        kch = k_ref[0, 0, pl.ds(k0, kc), :]
        vch = v_ref[0, 0, pl.ds(k0, kc), :]
        s = lax.dot_general(q, kch, (((1,), (1,)), ((), ())), preferred_element_type=F32) * scale
        off = t0 - k0
        ok = dmat <= off
        if window is not None:
            ok = ok & (dmat > off - window)
        if use_bm:
            ok = ok & (jnp.dot(bm_ref[0, 0], e_ref[c], preferred_element_type=F32) > 0.5)
        bias = jnp.where(ok, 0.0, MASKED_SCORE)
        s = (s.reshape(n_rep, tqt, kc) + bias[None]).reshape(rows, kc)
        m_new = jnp.maximum(m, jnp.max(s, axis=-1, keepdims=True))
        alpha = jnp.exp(m - m_new)
        p = jnp.exp(s - m_new)
        l = alpha * l + jnp.sum(p, axis=-1, keepdims=True)
        acc = alpha * acc + jnp.dot(p.astype(BF16), vch, preferred_element_type=F32)
        return m_new, l, acc

    lo = 0 if window is None else jnp.maximum(t0 - window + 1, 0) // kc
    hi = (t0 + tqt + kc - 1) // kc
    init = (jnp.full((rows, 1), RUNNING_MAX_FLOOR, F32), jnp.zeros((rows, 1), F32), jnp.zeros((rows, d), F32))
    _, l, acc = lax.fori_loop(lo, hi, body, init)
    o_ref[0] = (acc / l).reshape(n_rep, tqt, d).astype(o_ref.dtype)


def _block_expander(seq_len, kc, block, n_lanes=LANES):
    key_blk = (np.arange(seq_len) // block).reshape(seq_len // kc, 1, kc)
    return jnp.asarray(key_blk == np.arange(n_lanes).reshape(1, n_lanes, 1), dtype=BF16)


def _attention(q, k, v, tqt, scale, block_mask=None, block=None, window=None):
    B, HQ, S, d = q.shape
    G = k.shape[1]
    n_rep = HQ // G
    kc = _pick_tile(S, ATTN_KEY_CHUNK)
    tqt = _pick_tile(S, tqt)
    use_bm = block_mask is not None
    qspec = pl.BlockSpec((1, n_rep, tqt, d), lambda b, g, i: (b, g, i, 0))
    kvspec = pl.BlockSpec((1, 1, S, d), lambda b, g, i: (b, g, 0, 0))
    in_specs = [qspec, kvspec, kvspec]
    args = [q, k, v]
    if use_bm:
        in_specs += [pl.BlockSpec((1, 1, tqt, LANES), lambda b, g, i: (b, g, i, 0)),
                     pl.BlockSpec((S // kc, LANES, kc), lambda b, g, i: (0, 0, 0))]
        args += [block_mask, _block_expander(S, kc, block)]
    return pl.pallas_call(
        functools.partial(_attn_kernel, tqt=tqt, n_rep=n_rep, kc=kc, window=window, use_bm=use_bm, scale=scale),
        grid=(B, G, S // tqt),
        in_specs=in_specs,
        out_specs=qspec,
        out_shape=jax.ShapeDtypeStruct((B, HQ, S, d), F32),
        compiler_params=pltpu.CompilerParams(dimension_semantics=("parallel", "parallel", "parallel"),
                                             vmem_limit_bytes=V7X_VMEM_LIMIT_BYTES),
        name="block_masked_attention",
    )(*args)


def _nsa_cmp_kernel(q_ref, kc_ref, vc_ref, ov_ref, o_ref, imp_ref, *, tqt, n_rep, ncmp, scale):
    t0 = pl.program_id(1) * tqt
    d = q_ref.shape[-1]
    ncp = kc_ref.shape[1]
    rows = n_rep * tqt
    q = q_ref[0].reshape(rows, d)
    s = lax.dot_general(q, kc_ref[0], (((1,), (1,)), ((), ())), preferred_element_type=F32) * scale
    n_io = lax.broadcasted_iota(jnp.int32, (tqt, ncp), 1)
    t_io = t0 + lax.broadcasted_iota(jnp.int32, (tqt, ncp), 0)
    valid = (n_io * NSA_CMP_STRIDE + (NSA_CMP_BLOCK - 1) <= t_io) & (n_io < ncmp)
    bias = jnp.where(valid, 0.0, MASKED_SCORE)
    s = s.reshape(n_rep, tqt, ncp) + bias[None]
    m = jnp.maximum(jnp.max(s, axis=-1, keepdims=True), RUNNING_MAX_FLOOR)
    e = jnp.exp(s - m)
    l = jnp.sum(e, axis=-1, keepdims=True)
    p = e / jnp.where(l > 0.0, l, 1.0)
    o = jnp.dot(p.reshape(rows, ncp).astype(BF16), vc_ref[0], preferred_element_type=F32)
    o_ref[0] = o.reshape(n_rep, tqt, d)
    imp_ref[0] = jnp.dot(jnp.sum(p, axis=0), ov_ref[...], preferred_element_type=F32, precision=HIGHEST)


def _nsa_cmp_attention(q, k_cmp, v_cmp, overlap, ncmp, scale, tqt=128):
    B, H, S, d = q.shape
    ncp = k_cmp.shape[1]
    tqt = _pick_tile(S, tqt)
    qspec = pl.BlockSpec((1, H, tqt, d), lambda b, i: (b, 0, i, 0))
    cspec = pl.BlockSpec((1, ncp, d), lambda b, i: (b, 0, 0))
    return pl.pallas_call(
        functools.partial(_nsa_cmp_kernel, tqt=tqt, n_rep=H, ncmp=ncmp, scale=scale),
        grid=(B, S // tqt),
        in_specs=[qspec, cspec, cspec, pl.BlockSpec((ncp, LANES), lambda b, i: (0, 0))],
        out_specs=[qspec, pl.BlockSpec((1, tqt, LANES), lambda b, i: (b, i, 0))],
        out_shape=[jax.ShapeDtypeStruct((B, H, S, d), F32), jax.ShapeDtypeStruct((B, S, LANES), F32)],
        compiler_params=pltpu.CompilerParams(dimension_semantics=("parallel", "parallel"),
                                             vmem_limit_bytes=V7X_VMEM_LIMIT_BYTES),
        name="nsa_compressed_attention",
    )(q, k_cmp, v_cmp, overlap)


RWKV_TIME_CHUNK = 32
RWKV_ROWS = RWKV_HEAD_DIM // 2
RWKV_ROW_GROUPS = RWKV_ROWS // SUBLANES
RWKV_KEY_TILES = RWKV_HEAD_DIM // SUBLANES


def _rwkv_scan_kernel(keys_ref, v_ref, y_ref, st_ref):
    @pl.when(pl.program_id(0) == 0)
    def _():
        st_ref[...] = jnp.zeros_like(st_ref)

    n_t = keys_ref.shape[0]
    sub_iota = lax.broadcasted_iota(jnp.int32, (SUBLANES, st_ref.shape[-1]), 0)

    def step(t, carry):
        r = keys_ref[t, 0]
        w = keys_ref[t, 1]
        k = keys_ref[t, 2]
        kk = keys_ref[t, 3]
        bb = keys_ref[t, 4]

        def group(g, c2):
            y_tile = jnp.zeros(sub_iota.shape, F32)
            for ii in range(SUBLANES):
                i = g * SUBLANES + ii
                s = st_ref[i]
                sa = jnp.sum(jnp.sum(s * kk, axis=0), axis=0, keepdims=True)
                vrow = v_ref[t, g, pl.ds(ii, 1), :]
                s = s * w - sa[None] * bb + vrow[None] * k
                st_ref[i] = s
                yrow = jnp.sum(jnp.sum(s * r, axis=0), axis=0, keepdims=True)
                y_tile = jnp.where(sub_iota == ii, jnp.broadcast_to(yrow, sub_iota.shape), y_tile)
            y_ref[t, g] = y_tile
            return c2

        lax.fori_loop(0, RWKV_ROW_GROUPS, group, 0, unroll=True)
        return carry

    lax.fori_loop(0, n_t, step, 0)


def _rwkv_scan(r, w, k, v, kk, bvec):
    B, S, H, N = r.shape
    chains = B * H
    lanes = 2 * chains

    keys = jnp.transpose(jnp.stack([r, w, k, kk, bvec]), (2, 0, 4, 1, 3)).reshape(S, 5, N, chains)
    keys = jnp.concatenate([keys, keys], axis=-1).reshape(S, 5, RWKV_KEY_TILES, SUBLANES, lanes)

    def val_layout(z):
        zt = jnp.transpose(z.reshape(B, S, H, 2, RWKV_ROWS), (1, 4, 3, 0, 2))
        return zt.reshape(S, RWKV_ROW_GROUPS, SUBLANES, lanes)

    tc = _pick_tile(S, RWKV_TIME_CHUNK)
    kspec = pl.BlockSpec((tc, 5, RWKV_KEY_TILES, SUBLANES, lanes), lambda c: (c, 0, 0, 0, 0))
    vspec = pl.BlockSpec((tc, RWKV_ROW_GROUPS, SUBLANES, lanes), lambda c: (c, 0, 0, 0))
    y = pl.pallas_call(
        _rwkv_scan_kernel,
        grid=(S // tc,),
        in_specs=[kspec, vspec],
        out_specs=vspec,
        out_shape=jax.ShapeDtypeStruct((S, RWKV_ROW_GROUPS, SUBLANES, lanes), F32),
        scratch_shapes=[pltpu.VMEM((RWKV_ROWS, RWKV_KEY_TILES, SUBLANES, lanes), F32)],
        compiler_params=pltpu.CompilerParams(dimension_semantics=("arbitrary",),
                                             vmem_limit_bytes=V7X_VMEM_LIMIT_BYTES),
        name="rwkv7_scan",
    )(keys, val_layout(v))
    y = y.reshape(S, RWKV_ROWS, 2, B, H)
    return jnp.transpose(y, (3, 0, 4, 2, 1)).reshape(B, S, H, N)


ROUTE_TOKENS = 512


def _first_index_of_max(vals, index, n, axes):
    m = vals
    for ax in axes:
        m = jnp.max(m, axis=ax, keepdims=True)
    idx = jnp.where(vals == m, index, float(n))
    for ax in axes:
        idx = jnp.min(idx, axis=ax, keepdims=True)
    return m, idx


def _route_kernel(lg_ref, bias_ref, upper_ref, eidx_ref, wts_ref, rank_ref, cnt_ref, carry_ref):
    @pl.when(pl.program_id(0) == 0)
    def _():
        carry_ref[...] = jnp.zeros_like(carry_ref)

    E, tn = lg_ref.shape
    G = N_GROUPS
    J = E // G
    s3 = jax.nn.sigmoid(lg_ref[...]).reshape(G, J, tn)
    b3 = s3 + bias_ref[...].reshape(G, J, 1)
    j_io = lax.broadcasted_iota(jnp.int32, (G, J, tn), 1).astype(F32)
    g_io = lax.broadcasted_iota(jnp.int32, (G, 1, tn), 0).astype(F32)
    e_io = lax.broadcasted_iota(jnp.int32, (G, J, tn), 0).astype(F32) * J + j_io

    m1, first = _first_index_of_max(b3, j_io, J, (1,))
    m2 = jnp.max(jnp.where(j_io == first, -jnp.inf, b3), axis=1, keepdims=True)
    cur = m1 + m2
    keep = jnp.zeros((G, 1, tn), F32)
    for _ in range(TOPK_GROUPS):
        _, idx = _first_index_of_max(cur, g_io, G, (0,))
        hit = g_io == idx
        keep = jnp.where(hit, 1.0, keep)
        cur = jnp.where(hit, -jnp.inf, cur)

    cur = jnp.where(keep > 0.5, b3, NEG)
    hot = jnp.zeros((G, J, tn), F32)
    picks, weights = [], []
    for _ in range(TOP_K):
        _, idx = _first_index_of_max(cur, e_io, E, (0, 1))
        hit = e_io == idx
        w = jnp.sum(jnp.sum(jnp.where(hit, s3, 0.0), axis=0, keepdims=True), axis=1, keepdims=True)
        cur = jnp.where(hit, -jnp.inf, cur)
        hot = jnp.where(hit, 1.0, hot)
        picks.append(idx)
        weights.append(w)
    wsum = weights[0]
    for w in weights[1:]:
        wsum = wsum + w

    hot2 = hot.reshape(E, tn)
    before = jnp.dot(hot2.astype(BF16), upper_ref[...], preferred_element_type=F32) + carry_ref[:, 0:1]
    before3 = before.reshape(G, J, tn)
    for k in range(TOP_K):
        eidx_ref[k:k + 1, :] = picks[k].reshape(1, tn).astype(jnp.int32)
        wts_ref[k:k + 1, :] = (weights[k] / wsum * ROUTED_SCALE).reshape(1, tn)
        r = jnp.sum(jnp.sum(jnp.where(e_io == picks[k], before3, 0.0), axis=0, keepdims=True), axis=1, keepdims=True)
        rank_ref[k:k + 1, :] = r.reshape(1, tn).astype(jnp.int32)
    carry_ref[...] = carry_ref[...] + jnp.sum(hot2, axis=1, keepdims=True)
    cnt_ref[...] = carry_ref[...]


def _route(logits_t, bias):
    E, T = logits_t.shape
    tn = _pick_tile(T, ROUTE_TOKENS)
    upper = jnp.asarray(np.triu(np.ones((tn, tn), np.float32), 1), BF16)
    tok_spec = pl.BlockSpec((TOP_K, tn), lambda i: (0, i))
    eidx, wts, rank, cnt = pl.pallas_call(
        _route_kernel,
        grid=(T // tn,),
        in_specs=[pl.BlockSpec((E, tn), lambda i: (0, i)),
                  pl.BlockSpec((E, 1), lambda i: (0, 0)),
                  pl.BlockSpec((tn, tn), lambda i: (0, 0))],
        out_specs=[tok_spec, tok_spec, tok_spec, pl.BlockSpec((E, LANES), lambda i: (0, 0))],
        out_shape=[jax.ShapeDtypeStruct((TOP_K, T), jnp.int32), jax.ShapeDtypeStruct((TOP_K, T), F32),
                   jax.ShapeDtypeStruct((TOP_K, T), jnp.int32), jax.ShapeDtypeStruct((E, LANES), F32)],
        scratch_shapes=[pltpu.VMEM((E, LANES), F32)],
        compiler_params=pltpu.CompilerParams(dimension_semantics=("arbitrary",),
                                             vmem_limit_bytes=V7X_VMEM_LIMIT_BYTES),
        name="moe_route",
    )(logits_t, bias.reshape(E, 1), upper)
    return eidx, wts, rank, cnt[:, 0].astype(jnp.int32)


def _moe_kernel(be_ref, nused_ref, x_ref, wg_ref, wu_ref, wd_ref, o_ref, wg_s, wu_s, wd_s):
    i = pl.program_id(0)
    active = i < nused_ref[0]

    @pl.when(active & ((i == 0) | (be_ref[i] != be_ref[jnp.maximum(i - 1, 0)])))
    def _():
        wg_s[...] = wg_ref[0].astype(BF16)
        wu_s[...] = wu_ref[0].astype(BF16)
        wd_s[...] = wd_ref[0].astype(BF16)

    @pl.when(active)
    def _():
        x_lo, x_hi = (z.astype(BF16) for z in _unpack_bf16_halves(x_ref[...]))
        half = x_lo.shape[1]
        g = (jnp.dot(x_lo, wg_s[:half, :], preferred_element_type=F32)
             + jnp.dot(x_hi, wg_s[half:, :], preferred_element_type=F32))
        u = (jnp.dot(x_lo, wu_s[:half, :], preferred_element_type=F32)
             + jnp.dot(x_hi, wu_s[half:, :], preferred_element_type=F32))
        hmid = (g * jax.nn.sigmoid(g) * u).astype(BF16)
        o_ref[...] = _pack_bf16_halves(jnp.dot(hmid, wd_s[...], preferred_element_type=F32))

    @pl.when(jnp.logical_not(active))
    def _():
        o_ref[...] = jnp.zeros_like(o_ref)


def _moe_grouped(xg, blk_e, n_used, w_gate, w_up, w_down, layer):
    R = xg.shape[0]
    D = 2 * xg.shape[1]
    n_blk = R // MOE_BLOCK
    n_layers, E, _, F = w_gate.shape
    w_gate, w_up, w_down = (w.reshape((n_layers * E,) + w.shape[2:]) for w in (w_gate, w_up, w_down))

    def xmap(i, be, nu):
        return (jnp.minimum(i, jnp.maximum(nu[0] - 1, 0)), 0)

    def wmap(i, be, nu):
        return (layer * E + be[i], 0, 0)

    grid_spec = pltpu.PrefetchScalarGridSpec(
        num_scalar_prefetch=2,
        grid=(n_blk,),
        in_specs=[pl.BlockSpec((MOE_BLOCK, D // 2), xmap),
                  pl.BlockSpec((1, D, F), wmap),
                  pl.BlockSpec((1, D, F), wmap),
                  pl.BlockSpec((1, F, D), wmap)],
        out_specs=pl.BlockSpec((MOE_BLOCK, D // 2), lambda i, be, nu: (i, 0)),
        scratch_shapes=[pltpu.VMEM((D, F), BF16), pltpu.VMEM((D, F), BF16), pltpu.VMEM((F, D), BF16)],
    )
    return pl.pallas_call(
        _moe_kernel,
        grid_spec=grid_spec,
        out_shape=jax.ShapeDtypeStruct((R, D // 2), jnp.uint32),
        compiler_params=pltpu.CompilerParams(dimension_semantics=("arbitrary",),
                                             vmem_limit_bytes=V7X_VMEM_LIMIT_BYTES),
        name="moe_grouped_swiglu",
    )(blk_e, n_used, xg, w_gate, w_up, w_down)


MOE_COMBINE_ROWS = 256


def _moe_combine_kernel(a_ref, b_ref, res_ref, gate_ref, mine_ref, w_ref, o_ref):
    half = o_ref.shape[1] // 2
    y = jnp.dot(a_ref[...], b_ref[...], preferred_element_type=F32)
    lo_acc = y[:, :half]
    hi_acc = y[:, half:]
    for k in range(mine_ref.shape[0]):
        lo, hi = _unpack_bf16_halves(mine_ref[k])
        wk = w_ref[:, k:k + 1]
        lo_acc = lo_acc + wk * lo
        hi_acc = hi_acc + wk * hi
    g = gate_ref[0]
    o_ref[:, :half] = res_ref[:, :half] + g[:, :half] * lo_acc
    o_ref[:, half:] = res_ref[:, half:] + g[:, half:] * hi_acc


def _moe_combine(hmid, s_down, x_res, gate, seq_len, mine, wts):
    M, F = hmid.shape
    N = s_down.shape[1]
    n_k = mine.shape[0]
    tm = _pick_tile(seq_len, MOE_COMBINE_ROWS)
    row_tile = pl.BlockSpec((tm, N), lambda i: (i, 0))
    return pl.pallas_call(
        _moe_combine_kernel,
        grid=(M // tm,),
        in_specs=[pl.BlockSpec((tm, F), lambda i: (i, 0)),
                  pl.BlockSpec((F, N), lambda i: (0, 0)),
                  row_tile,
                  pl.BlockSpec((1, 1, N), lambda i: ((i * tm) // seq_len, 0, 0)),
                  pl.BlockSpec((n_k, tm, N // 2), lambda i: (0, i, 0)),
                  pl.BlockSpec((tm, n_k), lambda i: (i, 0))],
        out_specs=row_tile,
        out_shape=jax.ShapeDtypeStruct((M, N), F32),
        compiler_params=pltpu.CompilerParams(dimension_semantics=("parallel",),
                                             vmem_limit_bytes=V7X_VMEM_LIMIT_BYTES),
        name="moe_combine_residual",
    )(hmid, s_down, x_res, gate, mine, wts)


def _split(z, widths):
    cuts = [int(i) for i in np.cumsum(widths)[:-1]]
    return jnp.split(z, cuts, axis=-1)


def _rms(z, g, eps=NORM_EPS):
    return z * lax.rsqrt(jnp.mean(z * z, axis=-1, keepdims=True) + eps) * g


def _heads(z, n_heads):
    B, S, C = z.shape
    return z.reshape(B, S, n_heads, C // n_heads).transpose(0, 2, 1, 3)


def _merge_heads(z):
    B, H, S, d = z.shape
    return z.transpose(0, 2, 1, 3).reshape(B, S, H * d)


def _rope(z, pos):
    half = ROPE_DIM // 2
    inv_freq = ROPE_THETA ** (-jnp.arange(half, dtype=F32) / half)
    ang = pos.astype(F32)[:, None] * inv_freq[None, :]
    cos = jnp.cos(ang)
    sin = jnp.sin(ang)
    z1, z2, zr = z[..., :half], z[..., half:ROPE_DIM], z[..., ROPE_DIM:]
    return jnp.concatenate([z1 * cos - z2 * sin, z2 * cos + z1 * sin, zr], axis=-1)


def _token_shift(z):
    return jnp.pad(z, ((0, 0), (1, 0), (0, 0)))[:, :-1]


def _rwkv7(p, v_first, v_mix, mu, w0, w_up, a0, a_up, g_up, k_k, k_a, r_k, ln_w, ln_b):
    B, S, _ = p.shape
    H, N = RWKV_HEADS, RWKV_HEAD_DIM
    xs = p + (_token_shift(p) - p) * mu
    r, k, v, xw, xa, xg = _split(xs, (RWKV_DIM, RWKV_DIM, RWKV_DIM, RWKV_DECAY_LORA, RWKV_ICLR_LORA, RWKV_GATE_LORA))
    w = -jax.nn.softplus(-(w0 + jnp.tanh(xw) @ w_up)) - 0.5
    decay = jnp.exp(-jnp.exp(w))
    a = jax.nn.sigmoid(a0 + xa @ a_up)
    g = jax.nn.sigmoid(xg) @ g_up
    kk = (k * k_k).reshape(B, S, H, N)
    kk = kk / jnp.maximum(jnp.sqrt(jnp.sum(kk * kk, axis=-1, keepdims=True)), 1e-12)
    k = k * (1.0 + (a - 1.0) * k_a)
    if v_mix is None:
        v_first = v
    else:
        v0, v_down, v_up = v_mix
        v = v + (v_first - v) * jax.nn.sigmoid(v0 + (v @ v_down) @ v_up)

    def hd(z):
        return z.reshape(B, S, H, N)

    y = _rwkv_scan(hd(r), hd(decay), hd(k), hd(v), kk, kk * hd(a))
    mean = jnp.mean(y, axis=-1, keepdims=True)
    var = jnp.mean(jnp.square(y - mean), axis=-1, keepdims=True)
    y = ((y - mean) * lax.rsqrt(var + RWKV_GN_EPS)).reshape(B, S, RWKV_DIM) * ln_w + ln_b
    bonus = jnp.sum((r * k * r_k).reshape(B, S, H, N), axis=-1, keepdims=True) * v.reshape(B, S, H, N)
    y = (y + bonus.reshape(B, S, RWKV_DIM)) * g
    return y, v_first


def _mlstm_chunkwise(q, k, v, i_pre, logf):
    B, H, S, dk = q.shape
    dv = v.shape[-1]
    L = MLSTM_CHUNK
    NC = S // L
    q = q.reshape(B, H, NC, L, dk)
    k = k.reshape(B, H, NC, L, dk)
    v = v.reshape(B, H, NC, L, dv)
    ic = i_pre.reshape(B, H, NC, L)
    bcum = jnp.cumsum(logf.reshape(B, H, NC, L), axis=-1)
    gtot = bcum[..., -1]
    a_log = gtot[..., None] - bcum + ic
    a_max = jnp.max(a_log, axis=-1)

    def step(carry, inp):
        C, n, m = carry
        g_c, a_c, amax_c, k_c, v_c = inp
        m_new = jnp.maximum(g_c + m, amax_c)
        wgt = jnp.exp(a_c - m_new[..., None])
        dec = jnp.exp(g_c + m - m_new)
        C_new = dec[..., None, None] * C + jnp.einsum('bhl,bhlk,bhlv->bhkv', wgt, k_c, v_c)
        n_new = dec[..., None] * n + jnp.einsum('bhl,bhlk->bhk', wgt, k_c)
        return (C_new, n_new, m_new), (C, n, m)

    init = (jnp.zeros((B, H, dk, dv), F32), jnp.zeros((B, H, dk), F32), jnp.zeros((B, H), F32))
    xs = (jnp.moveaxis(gtot, 2, 0), jnp.moveaxis(a_log, 2, 0), jnp.moveaxis(a_max, 2, 0),
          jnp.moveaxis(k, 2, 0), jnp.moveaxis(v, 2, 0))
    _, (C_prev, n_prev, m_prev) = lax.scan(step, init, xs)
    C_prev = jnp.moveaxis(C_prev, 0, 2)
    n_prev = jnp.moveaxis(n_prev, 0, 2)
    m_prev = jnp.moveaxis(m_prev, 0, 2)

    causal = jnp.tril(jnp.ones((L, L), dtype=bool))
    logD = jnp.where(causal, bcum[..., :, None] - bcum[..., None, :] + ic[..., None, :], -jnp.inf)
    m_inter = bcum + m_prev[..., None]
    m_t = jnp.maximum(m_inter, jnp.max(logD, axis=-1))
    Dm = jnp.exp(logD - m_t[..., None])
    sqk = jnp.einsum('bhctd,bhcsd->bhcts', q, k) * Dm
    inter = jnp.exp(m_inter - m_t)
    num = jnp.einsum('bhcts,bhcsv->bhctv', sqk, v) + inter[..., None] * jnp.einsum('bhctk,bhckv->bhctv', q, C_prev)
    den = jnp.sum(sqk, axis=-1) + inter * jnp.einsum('bhctk,bhck->bhct', q, n_prev)
    h = num / jnp.maximum(jnp.abs(den), jnp.exp(-m_t))[..., None]
    return h.reshape(B, H, S, dv)


def _mlstm(p, p_gates, conv_w, conv_b, i_b, f_b, norm_g):
    H = MLSTM_HEADS
    qk, v, o = _split(p, (2 * MLSTM_QK_WIDTH, MLSTM_DIM, MLSTM_DIM))
    ig, fg = _split(p_gates, (H, H))
    S = qk.shape[1]
    qkp = jnp.pad(qk, ((0, 0), (MLSTM_CONV - 1, 0), (0, 0)))
    conv = conv_b
    for j in range(MLSTM_CONV):
        conv = conv + qkp[:, j:j + S] * conv_w[j]
    qk = jax.nn.silu(conv)
    q, k = jnp.split(qk, 2, axis=-1)
    q = _heads(q, H) * (MLSTM_QK_DIM ** -0.5)
    k = _heads(k, H)
    v = _heads(v, H)
    i_pre = jnp.moveaxis(ig + i_b, -1, 1)
    logf = jax.nn.log_sigmoid(jnp.moveaxis(fg + f_b, -1, 1))
    h = _mlstm_chunkwise(q, k, v, i_pre, logf)
    h = _rms(h, norm_g.reshape(H, 1, MLSTM_V_DIM))
    return jax.nn.sigmoid(o) * _merge_heads(h)


def _moba(q, k, v, k_mean):
    B, H, S, d = q.shape
    BLK = MOBA_BLOCK
    nb = S // BLK
    pos = jnp.arange(S)
    qblk = pos // BLK
    gate = jnp.einsum('bhsd,bnhd->bhsn', q.astype(F32), k_mean, precision=HIGHEST)
    gate = jnp.where(jnp.arange(nb)[None, :] < qblk[:, None], gate, NEG)
    n_top = min(MOBA_TOPK, nb)
    _, sel = lax.top_k(gate, n_top)
    slot_ok = jnp.arange(n_top)[None, :] < qblk[:, None]
    lanes = jnp.arange(LANES)
    hit = (sel[..., None] == lanes) & slot_ok[:, :, None]
    blk_mask = jnp.any(hit, axis=3) | (lanes[None, :] == qblk[:, None])
    return _attention(q, k, v, MOBA_BLOCK, d ** -0.5, block_mask=blk_mask.astype(BF16), block=BLK)


def _nsa(qb, k_slc, v_slc, k_win, v_win, kc_in, vc_in, gl, pos, k_norm_cmp, cmp_pe, cmp_w1, cmp_b1, cmp_w2):
    B, H, S, d = qb.shape
    scale = d ** -0.5

    ncmp = (S - NSA_CMP_BLOCK) // NSA_CMP_STRIDE + 1
    cstart = NSA_CMP_STRIDE * jnp.arange(ncmp)
    cend = cstart + NSA_CMP_BLOCK - 1
    cidx = cstart[:, None] + jnp.arange(NSA_CMP_BLOCK)[None, :]

    def compress(z, i):
        blocks = (z[:, cidx] + cmp_pe[i]).reshape(B, ncmp, NSA_CMP_BLOCK * d)
        return jax.nn.gelu(blocks @ cmp_w1[i] + cmp_b1[i]) @ cmp_w2[i]

    k_cmp = _rope(_rms(compress(kc_in, 0), k_norm_cmp), cend)
    v_cmp = compress(vc_in, 1)
    SB = NSA_SLC_BLOCK
    nsb = S // SB
    assert nsb <= LANES
    ncp = -(-ncmp // LANES) * LANES
    cs_np = NSA_CMP_STRIDE * np.arange(ncp)
    ss_np = SB * np.arange(LANES)
    overlap = ((cs_np[:, None] <= ss_np[None, :] + SB - 1) & (cs_np[:, None] + NSA_CMP_BLOCK - 1 >= ss_np[None, :])
               & (np.arange(ncp)[:, None] < ncmp) & (np.arange(LANES)[None, :] < nsb))
    cpad = ((0, 0), (0, ncp - ncmp), (0, 0))
    o_cmp, imp = _nsa_cmp_attention(qb, jnp.pad(k_cmp, cpad).astype(BF16), jnp.pad(v_cmp, cpad).astype(BF16),
                                    jnp.asarray(overlap, F32), ncmp, scale)
    imp = imp[:, :, :nsb]
    cur = pos // SB
    blk = jnp.arange(nsb)[None, :]
    forced = (blk == 0) | (blk == cur[:, None]) | (blk == cur[:, None] - 1)
    imp = jnp.where(forced, BIG, jnp.where(blk <= cur[:, None], imp, NEG))
    n_sel = min(NSA_SLC_TOPK, nsb)
    _, sel = lax.top_k(imp, n_sel)
    blk_mask = jnp.any(sel[..., None] == jnp.arange(LANES), axis=2)
    o_slc = _attention(qb, k_slc, v_slc, NSA_Q_TILE, scale, block_mask=blk_mask[:, None].astype(BF16), block=SB)
    o_win = _attention(qb, k_win, v_win, NSA_Q_TILE, scale, window=NSA_WINDOW)

    gates = jax.nn.sigmoid(gl).reshape(B, S, H, 3).transpose(0, 2, 1, 3)
    o = gates[..., 0:1] * o_cmp + gates[..., 1:2] * o_slc + gates[..., 2:3] * o_win
    return _merge_heads(o)


def _moe(xb, logits, x_packed, x_res, gate, seq_len, bias, w_gate, w_up, w_down, layer, s_gate, s_up, s_down):
    T, D = xb.shape
    A = T * TOP_K
    eidx, wts, rank, counts = _route(logits, bias)

    experts = jnp.arange(N_EXPERTS, dtype=jnp.int32)
    padded = (counts + MOE_BLOCK - 1) // MOE_BLOCK * MOE_BLOCK
    pstart = jnp.cumsum(padded) - padded
    slot = rank + jnp.sum(jnp.where(eidx[None] == experts[:, None, None], pstart[:, None, None], 0), axis=0)

    n_blk = -(-(A + N_EXPERTS * (MOE_BLOCK - 1)) // MOE_BLOCK)
    R = n_blk * MOE_BLOCK
    blk_e = jnp.minimum(jnp.sum((pstart + padded)[None, :] <= (jnp.arange(n_blk) * MOE_BLOCK)[:, None], axis=1),
                        N_EXPERTS - 1).astype(jnp.int32)
    n_used = (jnp.sum(padded) // MOE_BLOCK).astype(jnp.int32).reshape(1)
    tok_bits = max(int(T - 1).bit_length(), 1)
    tok_ids = jnp.arange(T, dtype=jnp.int32)[None, :]
    real_keys = (eidx << (tok_bits + 1)) | tok_ids
    n_fill = MOE_BLOCK - 1
    fill_j = jnp.arange(n_fill, dtype=jnp.int32)[None, :]
    unused = jnp.int32(N_EXPERTS << (tok_bits + 1))
    fill_keys = jnp.where(fill_j < (padded - counts)[:, None], (experts[:, None] << (tok_bits + 1)) | (1 << tok_bits), unused)
    tail = jnp.full((R - A - N_EXPERTS * n_fill,), unused, jnp.int32)
    keys = lax.sort(jnp.concatenate([real_keys.reshape(A), fill_keys.reshape(-1), tail]))
    is_fill = ((keys >> tok_bits) & 1 == 1) | (keys >= unused)
    tok_buf = jnp.where(is_fill, jnp.arange(R, dtype=jnp.int32) % T, keys & ((1 << tok_bits) - 1))

    packed = _moe_grouped(x_packed[tok_buf], blk_e, n_used, w_gate, w_up, w_down, layer)
    mine = packed[slot.reshape(A)].reshape(TOP_K, T, D // 2)

    gu = _matmul(xb, jnp.concatenate([s_gate, s_up], axis=1).astype(BF16))
    F = s_gate.shape[1]
    hmid = (jax.nn.silu(gu[:, :F]) * gu[:, F:]).astype(BF16)
    return _moe_combine(hmid, s_down.astype(BF16), x_res, gate, seq_len, mine, wts.T)


def _projection_layout():
    o_b = RWKV_IN_DIM
    o_c = o_b + MLSTM_IN_DIM
    o_d = o_c + MOBA_IN_DIM
    o_g = o_d + NSA_IN_DIM
    mlstm_main = MLSTM_IN_DIM - 2 * MLSTM_HEADS
    nsa_main = NSA_IN_DIM - 3 * NSA_HEADS
    spans = (('rwkv', 0, o_b), ('mlstm', o_b, o_b + mlstm_main), ('moba', o_c, o_d), ('nsa', o_d, o_d + nsa_main),
             ('mlstm_gates', o_b + mlstm_main, o_c), ('nsa_gates', o_d + nsa_main, o_g))
    pieces, cur = [], 0
    for name, lo, hi in spans:
        pieces.append((name, lo, hi, cur))
        cur += -(-(hi - lo) // LANES) * LANES
    return tuple(pieces), cur, o_g


def _relayout_kernel(w_ref, mix_ref, gates_ref, *, pieces, gate_lo):
    for _, lo, hi, off in pieces:
        width = hi - lo
        mix_ref[:, off:off + width] = w_ref[0, :, lo:hi].astype(BF16)
        pad = -width % LANES
        if pad:
            mix_ref[:, off + width:off + width + pad] = jnp.zeros((mix_ref.shape[0], pad), BF16)
    gates_ref[...] = w_ref[0, :, gate_lo:].astype(BF16)


def _projection_weights(w_in, layer, rows=128):
    _, D, n_in = w_in.shape
    pieces, mix_width, gate_lo = _projection_layout()
    w_mix, w_gates = pl.pallas_call(
        functools.partial(_relayout_kernel, pieces=pieces, gate_lo=gate_lo),
        grid=(D // rows,),
        in_specs=[pl.BlockSpec((1, rows, n_in), lambda i: (layer, i, 0))],
        out_specs=[pl.BlockSpec((rows, mix_width), lambda i: (i, 0)),
                   pl.BlockSpec((rows, n_in - gate_lo), lambda i: (i, 0))],
        out_shape=[jax.ShapeDtypeStruct((D, mix_width), BF16), jax.ShapeDtypeStruct((D, n_in - gate_lo), BF16)],
        compiler_params=pltpu.CompilerParams(dimension_semantics=("parallel",),
                                             vmem_limit_bytes=V7X_VMEM_LIMIT_BYTES),
        name="projection_weight_relayout",
    )(w_in)
    return w_mix, w_gates, {name: off for name, _, _, off in pieces}


def kernel(x, c, ada_w, ada_b, norm_mix_g, norm_ffn_g, w_in, rwkv_mu, rwkv_w0, rwkv_w_up, rwkv_a0, rwkv_a_up, rwkv_g_up, rwkv_k_k, rwkv_k_a, rwkv_r_k, rwkv_ln_w, rwkv_ln_b, rwkv_v0, rwkv_v_down, rwkv_v_up, mlstm_conv_w, mlstm_conv_b, mlstm_i_b, mlstm_f_b, mlstm_norm_g, moba_q_norm, moba_k_norm, nsa_q_norm, nsa_k_norm, nsa_cmp_pe, nsa_cmp_w1, nsa_cmp_b1, nsa_cmp_w2, w_branch, w_out, moe_router, moe_bias, moe_w_gate, moe_w_up, moe_w_down, shared_w_gate, shared_w_up, shared_w_down):
    B, S, D = x.shape
    T = B * S
    depth = ada_w.shape[0]
    pos = jnp.arange(S)
    cond = jax.nn.silu(c)
    v_first = None
    x2d = x.reshape(T, D)
    for l in range(depth):
        mod = jnp.dot(cond, ada_w[l], precision=HIGHEST) + ada_b[l]
        sh_mix, sc_mix, gate_mix, sh_ffn, sc_ffn, gate_ffn = [z[:, None, :] for z in jnp.split(mod, 6, axis=-1)]

        hb = _normmod(x2d, norm_mix_g[l], sc_mix, sh_mix, S)
        w_mix, w_gates, offs = _projection_weights(w_in, l)

        def proj(name, width, dtype=F32, n_pieces=1):
            padded = -(-width // LANES) * LANES * n_pieces
            return _matmul(hb, w_mix, out_dtype=dtype, cols=(offs[name], padded)).reshape(B, S, padded)

        small = proj('mlstm_gates', 2 * MLSTM_HEADS, n_pieces=2)
        nsa_gate_off = offs['nsa_gates'] - offs['mlstm_gates']
        v_mix = None if l == 0 else (rwkv_v0[l - 1], rwkv_v_down[l - 1], rwkv_v_up[l - 1])
        y_a, v_first = _rwkv7(proj('rwkv', RWKV_IN_DIM)[:, :, :RWKV_IN_DIM], v_first, v_mix, rwkv_mu[l], rwkv_w0[l],
                              rwkv_w_up[l], rwkv_a0[l], rwkv_a_up[l], rwkv_g_up[l], rwkv_k_k[l], rwkv_k_a[l],
                              rwkv_r_k[l], rwkv_ln_w[l], rwkv_ln_b[l])
        y_b = _mlstm(proj('mlstm', MLSTM_IN_DIM - 2 * MLSTM_HEADS), small[:, :, :2 * MLSTM_HEADS],
                     mlstm_conv_w[l], mlstm_conv_b[l], mlstm_i_b[l], mlstm_f_b[l], mlstm_norm_g[l])
        p_moba = proj('moba', MOBA_IN_DIM, BF16).reshape(T, MOBA_IN_DIM)
        hm = MOBA_HEADS
        q_c, k_c, v_c, k_mean = _head_prep(p_moba, jnp.stack([moba_q_norm[l], moba_k_norm[l]]),
                                           ((hm, 0, 0), (hm, hm, 1), (hm, 2 * hm, None)), S, with_mean=True)
        y_c = _merge_heads(_moba(q_c, k_c, v_c, k_mean))
        nsa_main = NSA_IN_DIM - 3 * NSA_HEADS
        p_nsa = proj('nsa', nsa_main, BF16)
        hn = NSA_HEADS
        q_d, ks_d, vs_d, kw_d, vw_d = _head_prep(
            p_nsa.reshape(T, nsa_main), jnp.stack([nsa_q_norm[l], nsa_k_norm[l, 1], nsa_k_norm[l, 2]]),
            ((hn, 0, 0), (1, hn + 2, 1), (1, hn + 3, None), (1, hn + 4, 2), (1, hn + 5, None)), S)
        cmp_in = p_nsa[:, :, NSA_DIM:NSA_DIM + 2 * ATTN_HEAD_DIM].astype(F32)
        y_d = _nsa(q_d, ks_d, vs_d, kw_d, vw_d, cmp_in[:, :, :ATTN_HEAD_DIM], cmp_in[:, :, ATTN_HEAD_DIM:],
                   small[:, :, nsa_gate_off:nsa_gate_off + 3 * NSA_HEADS], pos, nsa_k_norm[l, 0],
                   nsa_cmp_pe[l], nsa_cmp_w1[l], nsa_cmp_b1[l], nsa_cmp_w2[l])
        ys = [y.reshape(T, BRANCH_DIM).astype(BF16) for y in (y_a, y_b, y_c, y_d)]
        merged = _merge_branches(hb, w_gates, ys, w_branch[l].astype(BF16))
        x2d = _matmul_residual(merged, w_out[l].astype(BF16), x2d, gate_mix, S)

        hb, logits, hb_packed = _normmod(x2d, norm_ffn_g[l], sc_ffn, sh_ffn, S, router=moe_router[l])
        x2d = _moe(hb, logits, hb_packed, x2d, gate_ffn, S, moe_bias[l], moe_w_gate, moe_w_up, moe_w_down, l,
                   shared_w_gate[l], shared_w_up[l], shared_w_down[l])
    return x2d.reshape(B, S, D)
```

```python
import functools

import numpy as np
import jax
import jax.numpy as jnp
from jax import lax
from jax.experimental import pallas as pl
from jax.experimental.pallas import tpu as pltpu

F32 = jnp.float32
BF16 = jnp.bfloat16
HIGHEST = lax.Precision.HIGHEST

D_MODEL = 2048
N_MIXERS = 4
BRANCH_DIM = 512
NORM_EPS = 1e-6
NEG = -1e30
BIG = 1e9

RWKV_HEADS = 8
RWKV_HEAD_DIM = 64
RWKV_DIM = RWKV_HEADS * RWKV_HEAD_DIM
RWKV_DECAY_LORA = 96
RWKV_ICLR_LORA = 96
RWKV_GATE_LORA = 256
RWKV_GN_EPS = 64e-5
RWKV_IN_DIM = 3 * RWKV_DIM + RWKV_DECAY_LORA + RWKV_ICLR_LORA + RWKV_GATE_LORA

MLSTM_HEADS = 4
MLSTM_QK_DIM = 128
MLSTM_V_DIM = 128
MLSTM_QK_WIDTH = MLSTM_HEADS * MLSTM_QK_DIM
MLSTM_DIM = MLSTM_HEADS * MLSTM_V_DIM
MLSTM_CONV = 4
MLSTM_CHUNK = 64
MLSTM_IN_DIM = 2 * MLSTM_QK_WIDTH + 2 * MLSTM_DIM + 2 * MLSTM_HEADS

ATTN_HEAD_DIM = 128
ROPE_DIM = ATTN_HEAD_DIM // 4
ROPE_THETA = 500000.0

MOBA_HEADS = 4
MOBA_DIM = MOBA_HEADS * ATTN_HEAD_DIM
MOBA_BLOCK = 256
MOBA_TOPK = 3
MOBA_IN_DIM = 3 * MOBA_DIM

NSA_HEADS = 4
NSA_DIM = NSA_HEADS * ATTN_HEAD_DIM
NSA_CMP_BLOCK = 32
NSA_CMP_STRIDE = 16
NSA_SLC_BLOCK = 64
NSA_SLC_TOPK = 16
NSA_WINDOW = 512
NSA_IN_DIM = NSA_DIM + 6 * ATTN_HEAD_DIM + 3 * NSA_HEADS


N_EXPERTS = 64
TOP_K = 8
N_GROUPS = 8
TOPK_GROUPS = 4
EXPERT_DIM = 512
ROUTED_SCALE = 2.5
MOE_BLOCK = 512

V7X_VMEM_LIMIT_BYTES = 48 * 1024 * 1024
LANES = 128
SUBLANES = 8


def _mm_kernel(a_ref, b_ref, o_ref):
    o_ref[...] = jnp.dot(a_ref[...], b_ref[...], preferred_element_type=F32).astype(o_ref.dtype)


def _pick_tile(n, pref):
    t = min(pref, n)
    while n % t:
        t //= 2
    return t


def _pick_lane_tile(n, pref):
    assert n % LANES == 0
    units = n // LANES
    best = 1
    for u in range(1, units + 1):
        if units % u == 0 and u * LANES <= pref:
            best = u
    return best * LANES


def _matmul(a, b, out_dtype=F32, tm=1024, tn=1024, cols=None):
    M, K = a.shape
    off, N = (0, b.shape[1]) if cols is None else cols
    tm = _pick_tile(M, tm)
    tn = _pick_lane_tile(int(np.gcd(N, off)) if off else N, tn)
    first = off // tn
    return pl.pallas_call(
        _mm_kernel,
        grid=(N // tn, M // tm),
        in_specs=[pl.BlockSpec((tm, K), lambda j, i: (i, 0)),
                  pl.BlockSpec((K, tn), lambda j, i: (0, first + j))],
        out_specs=pl.BlockSpec((tm, tn), lambda j, i: (i, j)),
        out_shape=jax.ShapeDtypeStruct((M, N), out_dtype),
        compiler_params=pltpu.CompilerParams(dimension_semantics=("parallel", "parallel"),
                                             vmem_limit_bytes=V7X_VMEM_LIMIT_BYTES),
        name="dense_matmul",
    )(a, b)


def _mm_res_kernel(a_ref, b_ref, res_ref, gate_ref, o_ref):
    y = jnp.dot(a_ref[...], b_ref[...], preferred_element_type=F32)
    o_ref[...] = res_ref[...] + gate_ref[0] * y


def _matmul_residual(a, b, res, gate, seq_len, tm=1024, tn=1024):
    M, K = a.shape
    _, N = b.shape
    tm = _pick_tile(seq_len, tm)
    tn = _pick_lane_tile(N, tn)
    row_tile = pl.BlockSpec((tm, tn), lambda j, i: (i, j))
    return pl.pallas_call(
        _mm_res_kernel,
        grid=(N // tn, M // tm),
        in_specs=[pl.BlockSpec((tm, K), lambda j, i: (i, 0)),
                  pl.BlockSpec((K, tn), lambda j, i: (0, j)),
                  row_tile,
                  pl.BlockSpec((1, 1, tn), lambda j, i: ((i * tm) // seq_len, 0, j))],
        out_specs=row_tile,
        out_shape=jax.ShapeDtypeStruct((M, N), F32),
        compiler_params=pltpu.CompilerParams(dimension_semantics=("parallel", "parallel"),
                                             vmem_limit_bytes=V7X_VMEM_LIMIT_BYTES),
        name="matmul_gated_residual",
    )(a, b, res, gate)


def _normmod_kernel(*refs, with_router):
    if with_router:
        x_ref, g_ref, sc_ref, sh_ref, rt_ref, o_ref, lg_ref, pk_ref = refs
    else:
        x_ref, g_ref, sc_ref, sh_ref, o_ref = refs
    x = x_ref[...]
    y = x * lax.rsqrt(jnp.mean(x * x, axis=-1, keepdims=True) + NORM_EPS) * g_ref[...]
    h = y * (1.0 + sc_ref[0]) + sh_ref[0]
    o_ref[...] = h.astype(o_ref.dtype)
    if with_router:
        lg_ref[...] = lax.dot_general(rt_ref[...], h, (((1,), (1,)), ((), ())),
                                      preferred_element_type=F32, precision=HIGHEST)
        pk_ref[...] = _pack_bf16_halves(h)


def _pack_bf16_halves(y):
    half = y.shape[1] // 2
    lo = pltpu.bitcast(y[:, :half].astype(BF16).astype(F32), jnp.uint32) >> 16
    hi = pltpu.bitcast(y[:, half:].astype(BF16).astype(F32), jnp.uint32) & jnp.uint32(0xFFFF0000)
    return hi | lo


def _unpack_bf16_halves(word):
    return pltpu.bitcast(word << 16, F32), pltpu.bitcast(word & jnp.uint32(0xFFFF0000), F32)


def _normmod(x2d, g, scale, shift, seq_len, router=None, tm=512):
    M, D = x2d.shape
    tm = _pick_tile(seq_len, tm)
    mod_spec = pl.BlockSpec((1, 1, D), lambda i: ((i * tm) // seq_len, 0, 0))
    row_spec = pl.BlockSpec((tm, D), lambda i: (i, 0))
    in_specs = [row_spec, pl.BlockSpec((1, D), lambda i: (0, 0)), mod_spec, mod_spec]
    args = [x2d, g.reshape(1, D), scale, shift]
    out_shape = [jax.ShapeDtypeStruct((M, D), BF16)]
    out_specs = [row_spec]
    if router is not None:
        n_exp = router.shape[1]
        in_specs.append(pl.BlockSpec((n_exp, D), lambda i: (0, 0)))
        args.append(router.T)
        out_shape.append(jax.ShapeDtypeStruct((n_exp, M), F32))
        out_specs.append(pl.BlockSpec((n_exp, tm), lambda i: (0, i)))
        out_shape.append(jax.ShapeDtypeStruct((M, D // 2), jnp.uint32))
        out_specs.append(pl.BlockSpec((tm, D // 2), lambda i: (i, 0)))
    out = pl.pallas_call(
        functools.partial(_normmod_kernel, with_router=router is not None),
        grid=(M // tm,),
        in_specs=in_specs,
        out_specs=out_specs,
        out_shape=out_shape,
        compiler_params=pltpu.CompilerParams(dimension_semantics=("parallel",),
                                             vmem_limit_bytes=V7X_VMEM_LIMIT_BYTES),
        name="rmsnorm_modulate",
    )(*args)
    return tuple(out) if router is not None else out[0]


def _merge_kernel(h_ref, wg0, wg1, wg2, wg3, y0, y1, y2, y3, wb_ref, o_ref):
    h = h_ref[...]
    acc = None
    for m, (wg_ref, y_ref) in enumerate(((wg0, y0), (wg1, y1), (wg2, y2), (wg3, y3))):
        gate = jax.nn.sigmoid(jnp.dot(h, wg_ref[...], preferred_element_type=F32))
        t = jnp.dot(y_ref[...], wb_ref[m], preferred_element_type=F32) * gate
        acc = t if acc is None else acc + t
    o_ref[...] = acc.astype(o_ref.dtype)


def _merge_branches(hb, w_gates, ys, w_branch, tm=1024, tn=512):
    M, K = hb.shape
    n_mix, kb, D = w_branch.shape
    tm = _pick_tile(M, tm)
    tn = _pick_lane_tile(D, tn)
    nj = D // tn
    gate_specs = [pl.BlockSpec((K, tn), functools.partial(lambda j, i, m: (0, m * nj + j), m=m)) for m in range(n_mix)]
    y_specs = [pl.BlockSpec((tm, kb), lambda j, i: (i, 0)) for _ in range(n_mix)]
    return pl.pallas_call(
        _merge_kernel,
        grid=(nj, M // tm),
        in_specs=[pl.BlockSpec((tm, K), lambda j, i: (i, 0))] + gate_specs + y_specs
                 + [pl.BlockSpec((n_mix, kb, tn), lambda j, i: (0, 0, j))],
        out_specs=pl.BlockSpec((tm, tn), lambda j, i: (i, j)),
        out_shape=jax.ShapeDtypeStruct((M, D), BF16),
        compiler_params=pltpu.CompilerParams(dimension_semantics=("parallel", "parallel"),
                                             vmem_limit_bytes=V7X_VMEM_LIMIT_BYTES),
        name="merge_gated_branches",
    )(hb, *([w_gates] * n_mix), *ys, w_branch)


HEAD_PREP_ROWS = 256


def _rope_tables(seq_len):
    half = ROPE_DIM // 2
    inv_freq = ROPE_THETA ** (-jnp.arange(half, dtype=F32) / half)
    ang = jnp.arange(seq_len, dtype=F32)[:, None] * inv_freq[None, :]
    cos, sin = jnp.cos(ang), jnp.sin(ang)
    rest = LANES - ROPE_DIM
    cosf = jnp.concatenate([cos, cos, jnp.ones((seq_len, rest), F32)], axis=1)
    sinf = jnp.concatenate([-sin, sin, jnp.zeros((seq_len, rest), F32)], axis=1)
    return cosf, sinf


def _head_prep_kernel(*refs, groups, with_mean):
    p_ref, g_ref, cos_ref, sin_ref = refs[:4]
    out_refs = refs[4:]
    half = ROPE_DIM // 2
    lane = lax.broadcasted_iota(jnp.int32, (p_ref.shape[0], LANES), 1)
    cosf = cos_ref[...]
    sinf = sin_ref[...]
    for gi, (n_heads, first_tile, gain_row) in enumerate(groups):
        for h in range(n_heads):
            c0 = (first_tile + h) * LANES
            z = p_ref[:, c0:c0 + LANES]
            if gain_row is not None:
                z = z.astype(F32)
                z = z * lax.rsqrt(jnp.mean(z * z, axis=-1, keepdims=True) + NORM_EPS) * g_ref[gain_row:gain_row + 1, :]
                partner = jnp.where(lane < half, pltpu.roll(z, LANES - half, axis=1), pltpu.roll(z, half, axis=1))
                z = z * cosf + partner * sinf
                if with_mean and gi == 1:
                    out_refs[-1][0, 0, h:h + 1, :] = jnp.mean(z, axis=0, keepdims=True)
            out_refs[gi][0, h] = z.astype(BF16)


def _head_prep(p, gains, groups, seq_len, with_mean=False):
    M, W = p.shape
    B = M // seq_len
    tm = _pick_tile(seq_len, HEAD_PREP_ROWS)
    per_seq = seq_len // tm
    cosf, sinf = _rope_tables(seq_len)
    rope_spec = pl.BlockSpec((tm, LANES), lambda i: (i % per_seq, 0))
    out_shape = [jax.ShapeDtypeStruct((B, n, seq_len, LANES), BF16) for n, _, _ in groups]
    out_specs = [pl.BlockSpec((1, n, tm, LANES), lambda i: (i // per_seq, 0, i % per_seq, 0)) for n, _, _ in groups]
    if with_mean:
        n = groups[1][0]
        out_shape.append(jax.ShapeDtypeStruct((B, per_seq, n, LANES), F32))
        out_specs.append(pl.BlockSpec((1, 1, n, LANES), lambda i: (i // per_seq, i % per_seq, 0, 0)))
    return pl.pallas_call(
        functools.partial(_head_prep_kernel, groups=tuple(groups), with_mean=with_mean),
        grid=(M // tm,),
        in_specs=[pl.BlockSpec((tm, W), lambda i: (i, 0)),
                  pl.BlockSpec(gains.shape, lambda i: (0, 0)),
                  rope_spec, rope_spec],
        out_specs=out_specs,
        out_shape=out_shape,
        compiler_params=pltpu.CompilerParams(dimension_semantics=("parallel",),
                                             vmem_limit_bytes=V7X_VMEM_LIMIT_BYTES),
        name="attention_head_prep",
    )(p, gains, cosf, sinf)


ATTN_KEY_CHUNK = 512
NSA_Q_TILE = 128
MASKED_SCORE = -1e30
RUNNING_MAX_FLOOR = -1e20


def _attn_kernel(*refs, tqt, n_rep, kc, window, use_bm, scale):
    if use_bm:
        q_ref, k_ref, v_ref, bm_ref, e_ref, o_ref = refs
    else:
        q_ref, k_ref, v_ref, o_ref = refs
    t0 = pl.program_id(2) * tqt
    d = q_ref.shape[-1]
    rows = n_rep * tqt
    q = q_ref[0].reshape(rows, d)
    dmat = lax.broadcasted_iota(jnp.int32, (tqt, kc), 1) - lax.broadcasted_iota(jnp.int32, (tqt, kc), 0)

    def body(c, carry):
        m, l, acc = carry
        k0 = pl.multiple_of(c * kc, kc)
        kch = k_ref[0, 0, pl.ds(k0, kc), :]
        vch = v_ref[0, 0, pl.ds(k0, kc), :]
        s = lax.dot_general(q, kch, (((1,), (1,)), ((), ())), preferred_element_type=F32) * scale
        off = t0 - k0
        ok = dmat <= off
        if window is not None:
            ok = ok & (dmat > off - window)
        if use_bm:
            ok = ok & (jnp.dot(bm_ref[0, 0], e_ref[c], preferred_element_type=F32) > 0.5)
        bias = jnp.where(ok, 0.0, MASKED_SCORE)
        s = (s.reshape(n_rep, tqt, kc) + bias[None]).reshape(rows, kc)
        m_new = jnp.maximum(m, jnp.max(s, axis=-1, keepdims=True))
        alpha = jnp.exp(m - m_new)
        p = jnp.exp(s - m_new)
        l = alpha * l + jnp.sum(p, axis=-1, keepdims=True)
        acc = alpha * acc + jnp.dot(p.astype(BF16), vch, preferred_element_type=F32)
        return m_new, l, acc

    lo = 0 if window is None else jnp.maximum(t0 - window + 1, 0) // kc
    hi = (t0 + tqt + kc - 1) // kc
    init = (jnp.full((rows, 1), RUNNING_MAX_FLOOR, F32), jnp.zeros((rows, 1), F32), jnp.zeros((rows, d), F32))
    _, l, acc = lax.fori_loop(lo, hi, body, init)
    o = (acc / l).astype(o_ref.dtype)
    for h in range(n_rep):
        o_ref[0, :, h * d:(h + 1) * d] = o[h * tqt:(h + 1) * tqt, :]


def _block_expander(seq_len, kc, block, n_lanes=LANES):
    key_blk = (np.arange(seq_len) // block).reshape(seq_len // kc, 1, kc)
    return jnp.asarray(key_blk == np.arange(n_lanes).reshape(1, n_lanes, 1), dtype=BF16)


def _attention(q, k, v, tqt, scale, block_mask=None, block=None, window=None, out_dtype=F32):
    B, HQ, S, d = q.shape
    G = k.shape[1]
    n_rep = HQ // G
    kc = _pick_tile(S, ATTN_KEY_CHUNK)
    tqt = _pick_tile(S, tqt)
    use_bm = block_mask is not None
    qspec = pl.BlockSpec((1, n_rep, tqt, d), lambda b, g, i: (b, g, i, 0))
    kvspec = pl.BlockSpec((1, 1, S, d), lambda b, g, i: (b, g, 0, 0))
    in_specs = [qspec, kvspec, kvspec]
    args = [q, k, v]
    if use_bm:
        in_specs += [pl.BlockSpec((1, 1, tqt, LANES), lambda b, g, i: (b, g, i, 0)),
                     pl.BlockSpec((S // kc, LANES, kc), lambda b, g, i: (0, 0, 0))]
        args += [block_mask, _block_expander(S, kc, block)]
    return pl.pallas_call(
        functools.partial(_attn_kernel, tqt=tqt, n_rep=n_rep, kc=kc, window=window, use_bm=use_bm, scale=scale),
        grid=(B, G, S // tqt),
        in_specs=in_specs,
        out_specs=pl.BlockSpec((1, tqt, n_rep * d), lambda b, g, i: (b, i, g)),
        out_shape=jax.ShapeDtypeStruct((B, S, HQ * d), out_dtype),
        compiler_params=pltpu.CompilerParams(dimension_semantics=("parallel", "parallel", "parallel"),
                                             vmem_limit_bytes=V7X_VMEM_LIMIT_BYTES),
        name="block_masked_attention",
    )(*args)


def _nsa_cmp_kernel(q_ref, kc_ref, vc_ref, ov_ref, o_ref, imp_ref, *, tqt, n_rep, ncmp, scale):
    t0 = pl.program_id(1) * tqt
    d = q_ref.shape[-1]
    ncp = kc_ref.shape[1]
    rows = n_rep * tqt
    q = q_ref[0].reshape(rows, d)
    s = lax.dot_general(q, kc_ref[0], (((1,), (1,)), ((), ())), preferred_element_type=F32) * scale
    n_io = lax.broadcasted_iota(jnp.int32, (tqt, ncp), 1)
    t_io = t0 + lax.broadcasted_iota(jnp.int32, (tqt, ncp), 0)
    valid = (n_io * NSA_CMP_STRIDE + (NSA_CMP_BLOCK - 1) <= t_io) & (n_io < ncmp)
    bias = jnp.where(valid, 0.0, MASKED_SCORE)
    s = s.reshape(n_rep, tqt, ncp) + bias[None]
    m = jnp.maximum(jnp.max(s, axis=-1, keepdims=True), RUNNING_MAX_FLOOR)
    e = jnp.exp(s - m)
    l = jnp.sum(e, axis=-1, keepdims=True)
    p = e / jnp.where(l > 0.0, l, 1.0)
    o = jnp.dot(p.reshape(rows, ncp).astype(BF16), vc_ref[0], preferred_element_type=F32)
    for h in range(n_rep):
        o_ref[0, :, h * d:(h + 1) * d] = o[h * tqt:(h + 1) * tqt, :]
    imp_ref[0] = jnp.dot(jnp.sum(p, axis=0), ov_ref[...], preferred_element_type=F32, precision=HIGHEST)


def _nsa_cmp_attention(q, k_cmp, v_cmp, overlap, ncmp, scale, tqt=128):
    B, H, S, d = q.shape
    ncp = k_cmp.shape[1]
    tqt = _pick_tile(S, tqt)
    qspec = pl.BlockSpec((1, H, tqt, d), lambda b, i: (b, 0, i, 0))
    cspec = pl.BlockSpec((1, ncp, d), lambda b, i: (b, 0, 0))
    return pl.pallas_call(
        functools.partial(_nsa_cmp_kernel, tqt=tqt, n_rep=H, ncmp=ncmp, scale=scale),
        grid=(B, S // tqt),
        in_specs=[qspec, cspec, cspec, pl.BlockSpec((ncp, LANES), lambda b, i: (0, 0))],
        out_specs=[pl.BlockSpec((1, tqt, H * d), lambda b, i: (b, i, 0)),
                   pl.BlockSpec((1, tqt, LANES), lambda b, i: (b, i, 0))],
        out_shape=[jax.ShapeDtypeStruct((B, S, H * d), F32), jax.ShapeDtypeStruct((B, S, LANES), F32)],
        compiler_params=pltpu.CompilerParams(dimension_semantics=("parallel", "parallel"),
                                             vmem_limit_bytes=V7X_VMEM_LIMIT_BYTES),
        name="nsa_compressed_attention",
    )(q, k_cmp, v_cmp, overlap)


RWKV_TIME_CHUNK = 32
RWKV_ROWS = RWKV_HEAD_DIM // 2
RWKV_ROW_GROUPS = RWKV_ROWS // SUBLANES
RWKV_KEY_TILES = RWKV_HEAD_DIM // SUBLANES


def _rwkv_scan_kernel(keys_ref, v_ref, y_ref, st_ref):
    @pl.when(pl.program_id(0) == 0)
    def _():
        st_ref[...] = jnp.zeros_like(st_ref)

    n_t = keys_ref.shape[0]
    sub_iota = lax.broadcasted_iota(jnp.int32, (SUBLANES, st_ref.shape[-1]), 0)

    def step(t, carry):
        r = keys_ref[t, 0]
        w = keys_ref[t, 1]
        k = keys_ref[t, 2]
        kk = keys_ref[t, 3]
        bb = keys_ref[t, 4]

        def group(g, c2):
            y_tile = jnp.zeros(sub_iota.shape, F32)
            for ii in range(SUBLANES):
                i = g * SUBLANES + ii
                s = st_ref[i]
                sa = jnp.sum(jnp.sum(s * kk, axis=0), axis=0, keepdims=True)
                vrow = v_ref[t, g, pl.ds(ii, 1), :]
                s = s * w - sa[None] * bb + vrow[None] * k
                st_ref[i] = s
                yrow = jnp.sum(jnp.sum(s * r, axis=0), axis=0, keepdims=True)
                y_tile = jnp.where(sub_iota == ii, jnp.broadcast_to(yrow, sub_iota.shape), y_tile)
            y_ref[t, g] = y_tile
            return c2

        lax.fori_loop(0, RWKV_ROW_GROUPS, group, 0, unroll=True)
        return carry

    lax.fori_loop(0, n_t, step, 0)


def _rwkv_scan(r, w, k, v, kk, bvec):
    B, S, H, N = r.shape
    chains = B * H
    lanes = 2 * chains

    keys = jnp.transpose(jnp.stack([r, w, k, kk, bvec]), (2, 0, 4, 1, 3)).reshape(S, 5, N, chains)
    keys = jnp.concatenate([keys, keys], axis=-1).reshape(S, 5, RWKV_KEY_TILES, SUBLANES, lanes)

    def val_layout(z):
        zt = jnp.transpose(z.reshape(B, S, H, 2, RWKV_ROWS), (1, 4, 3, 0, 2))
        return zt.reshape(S, RWKV_ROW_GROUPS, SUBLANES, lanes)

    tc = _pick_tile(S, RWKV_TIME_CHUNK)
    kspec = pl.BlockSpec((tc, 5, RWKV_KEY_TILES, SUBLANES, lanes), lambda c: (c, 0, 0, 0, 0))
    vspec = pl.BlockSpec((tc, RWKV_ROW_GROUPS, SUBLANES, lanes), lambda c: (c, 0, 0, 0))
    y = pl.pallas_call(
        _rwkv_scan_kernel,
        grid=(S // tc,),
        in_specs=[kspec, vspec],
        out_specs=vspec,
        out_shape=jax.ShapeDtypeStruct((S, RWKV_ROW_GROUPS, SUBLANES, lanes), F32),
        scratch_shapes=[pltpu.VMEM((RWKV_ROWS, RWKV_KEY_TILES, SUBLANES, lanes), F32)],
        compiler_params=pltpu.CompilerParams(dimension_semantics=("arbitrary",),
                                             vmem_limit_bytes=V7X_VMEM_LIMIT_BYTES),
        name="rwkv7_scan",
    )(keys, val_layout(v))
    y = y.reshape(S, RWKV_ROWS, 2, B, H)
    return jnp.transpose(y, (3, 0, 4, 2, 1)).reshape(B, S, H, N)


ROUTE_TOKENS = 512


def _first_index_of_max(vals, index, n, axes):
    m = vals
    for ax in axes:
        m = jnp.max(m, axis=ax, keepdims=True)
    idx = jnp.where(vals == m, index, float(n))
    for ax in axes:
        idx = jnp.min(idx, axis=ax, keepdims=True)
    return m, idx


def _route_kernel(lg_ref, bias_ref, upper_ref, eidx_ref, wts_ref, rank_ref, cnt_ref, carry_ref):
    @pl.when(pl.program_id(0) == 0)
    def _():
        carry_ref[...] = jnp.zeros_like(carry_ref)

    E, tn = lg_ref.shape
    G = N_GROUPS
    J = E // G
    s3 = jax.nn.sigmoid(lg_ref[...]).reshape(G, J, tn)
    b3 = s3 + bias_ref[...].reshape(G, J, 1)
    j_io = lax.broadcasted_iota(jnp.int32, (G, J, tn), 1).astype(F32)
    g_io = lax.broadcasted_iota(jnp.int32, (G, 1, tn), 0).astype(F32)
    e_io = lax.broadcasted_iota(jnp.int32, (G, J, tn), 0).astype(F32) * J + j_io

    m1, first = _first_index_of_max(b3, j_io, J, (1,))
    m2 = jnp.max(jnp.where(j_io == first, -jnp.inf, b3), axis=1, keepdims=True)
    cur = m1 + m2
    keep = jnp.zeros((G, 1, tn), F32)
    for _ in range(TOPK_GROUPS):
        _, idx = _first_index_of_max(cur, g_io, G, (0,))
        hit = g_io == idx
        keep = jnp.where(hit, 1.0, keep)
        cur = jnp.where(hit, -jnp.inf, cur)

    cur = jnp.where(keep > 0.5, b3, NEG)
    hot = jnp.zeros((G, J, tn), F32)
    picks, weights = [], []
    for _ in range(TOP_K):
        _, idx = _first_index_of_max(cur, e_io, E, (0, 1))
        hit = e_io == idx
        w = jnp.sum(jnp.sum(jnp.where(hit, s3, 0.0), axis=0, keepdims=True), axis=1, keepdims=True)
        cur = jnp.where(hit, -jnp.inf, cur)
        hot = jnp.where(hit, 1.0, hot)
        picks.append(idx)
        weights.append(w)
    wsum = weights[0]
    for w in weights[1:]:
        wsum = wsum + w

    hot2 = hot.reshape(E, tn)
    before = jnp.dot(hot2.astype(BF16), upper_ref[...], preferred_element_type=F32) + carry_ref[:, 0:1]
    before3 = before.reshape(G, J, tn)
    for k in range(TOP_K):
        eidx_ref[k:k + 1, :] = picks[k].reshape(1, tn).astype(jnp.int32)
        wts_ref[k:k + 1, :] = (weights[k] / wsum * ROUTED_SCALE).reshape(1, tn)
        r = jnp.sum(jnp.sum(jnp.where(e_io == picks[k], before3, 0.0), axis=0, keepdims=True), axis=1, keepdims=True)
        rank_ref[k:k + 1, :] = r.reshape(1, tn).astype(jnp.int32)
    carry_ref[...] = carry_ref[...] + jnp.sum(hot2, axis=1, keepdims=True)
    cnt_ref[...] = carry_ref[...]


def _route(logits_t, bias):
    E, T = logits_t.shape
    tn = _pick_tile(T, ROUTE_TOKENS)
    upper = jnp.asarray(np.triu(np.ones((tn, tn), np.float32), 1), BF16)
    tok_spec = pl.BlockSpec((TOP_K, tn), lambda i: (0, i))
    eidx, wts, rank, cnt = pl.pallas_call(
        _route_kernel,
        grid=(T // tn,),
        in_specs=[pl.BlockSpec((E, tn), lambda i: (0, i)),
                  pl.BlockSpec((E, 1), lambda i: (0, 0)),
                  pl.BlockSpec((tn, tn), lambda i: (0, 0))],
        out_specs=[tok_spec, tok_spec, tok_spec, pl.BlockSpec((E, LANES), lambda i: (0, 0))],
        out_shape=[jax.ShapeDtypeStruct((TOP_K, T), jnp.int32), jax.ShapeDtypeStruct((TOP_K, T), F32),
                   jax.ShapeDtypeStruct((TOP_K, T), jnp.int32), jax.ShapeDtypeStruct((E, LANES), F32)],
        scratch_shapes=[pltpu.VMEM((E, LANES), F32)],
        compiler_params=pltpu.CompilerParams(dimension_semantics=("arbitrary",),
                                             vmem_limit_bytes=V7X_VMEM_LIMIT_BYTES),
        name="moe_route",
    )(logits_t, bias.reshape(E, 1), upper)
    return eidx, wts, rank, cnt[:, 0].astype(jnp.int32)


def _moe_kernel(be_ref, nused_ref, x_ref, wg_ref, wu_ref, wd_ref, o_ref, wg_s, wu_s, wd_s):
    i = pl.program_id(0)
    active = i < nused_ref[0]

    @pl.when(active & ((i == 0) | (be_ref[i] != be_ref[jnp.maximum(i - 1, 0)])))
    def _():
        wg_s[...] = wg_ref[0].astype(BF16)
        wu_s[...] = wu_ref[0].astype(BF16)
        wd_s[...] = wd_ref[0].astype(BF16)

    @pl.when(active)
    def _():
        x_lo, x_hi = (z.astype(BF16) for z in _unpack_bf16_halves(x_ref[...]))
        half = x_lo.shape[1]
        g = (jnp.dot(x_lo, wg_s[:half, :], preferred_element_type=F32)
             + jnp.dot(x_hi, wg_s[half:, :], preferred_element_type=F32))
        u = (jnp.dot(x_lo, wu_s[:half, :], preferred_element_type=F32)
             + jnp.dot(x_hi, wu_s[half:, :], preferred_element_type=F32))
        hmid = (g * jax.nn.sigmoid(g) * u).astype(BF16)
        o_ref[...] = _pack_bf16_halves(jnp.dot(hmid, wd_s[...], preferred_element_type=F32))

    @pl.when(jnp.logical_not(active))
    def _():
        o_ref[...] = jnp.zeros_like(o_ref)


def _moe_grouped(xg, blk_e, n_used, w_gate, w_up, w_down, layer):
    R = xg.shape[0]
    D = 2 * xg.shape[1]
    n_blk = R // MOE_BLOCK
    n_layers, E, _, F = w_gate.shape
    w_gate, w_up, w_down = (w.reshape((n_layers * E,) + w.shape[2:]) for w in (w_gate, w_up, w_down))

    def xmap(i, be, nu):
        return (jnp.minimum(i, jnp.maximum(nu[0] - 1, 0)), 0)

    def wmap(i, be, nu):
        return (layer * E + be[i], 0, 0)

    grid_spec = pltpu.PrefetchScalarGridSpec(
        num_scalar_prefetch=2,
        grid=(n_blk,),
        in_specs=[pl.BlockSpec((MOE_BLOCK, D // 2), xmap),
                  pl.BlockSpec((1, D, F), wmap),
                  pl.BlockSpec((1, D, F), wmap),
                  pl.BlockSpec((1, F, D), wmap)],
        out_specs=pl.BlockSpec((MOE_BLOCK, D // 2), lambda i, be, nu: (i, 0)),
        scratch_shapes=[pltpu.VMEM((D, F), BF16), pltpu.VMEM((D, F), BF16), pltpu.VMEM((F, D), BF16)],
    )
    return pl.pallas_call(
        _moe_kernel,
        grid_spec=grid_spec,
        out_shape=jax.ShapeDtypeStruct((R, D // 2), jnp.uint32),
        compiler_params=pltpu.CompilerParams(dimension_semantics=("arbitrary",),
                                             vmem_limit_bytes=V7X_VMEM_LIMIT_BYTES),
        name="moe_grouped_swiglu",
    )(blk_e, n_used, xg, w_gate, w_up, w_down)


MOE_COMBINE_ROWS = 256


def _moe_combine_kernel(a_ref, b_ref, res_ref, gate_ref, mine_ref, w_ref, o_ref):
    half = o_ref.shape[1] // 2
    y = jnp.dot(a_ref[...], b_ref[...], preferred_element_type=F32)
    lo_acc = y[:, :half]
    hi_acc = y[:, half:]
    for k in range(mine_ref.shape[0]):
        lo, hi = _unpack_bf16_halves(mine_ref[k])
        wk = w_ref[:, k:k + 1]
        lo_acc = lo_acc + wk * lo
        hi_acc = hi_acc + wk * hi
    g = gate_ref[0]
    o_ref[:, :half] = res_ref[:, :half] + g[:, :half] * lo_acc
    o_ref[:, half:] = res_ref[:, half:] + g[:, half:] * hi_acc


def _moe_combine(hmid, s_down, x_res, gate, seq_len, mine, wts):
    M, F = hmid.shape
    N = s_down.shape[1]
    n_k = mine.shape[0]
    tm = _pick_tile(seq_len, MOE_COMBINE_ROWS)
    row_tile = pl.BlockSpec((tm, N), lambda i: (i, 0))
    return pl.pallas_call(
        _moe_combine_kernel,
        grid=(M // tm,),
        in_specs=[pl.BlockSpec((tm, F), lambda i: (i, 0)),
                  pl.BlockSpec((F, N), lambda i: (0, 0)),
                  row_tile,
                  pl.BlockSpec((1, 1, N), lambda i: ((i * tm) // seq_len, 0, 0)),
                  pl.BlockSpec((n_k, tm, N // 2), lambda i: (0, i, 0)),
                  pl.BlockSpec((tm, n_k), lambda i: (i, 0))],
        out_specs=row_tile,
        out_shape=jax.ShapeDtypeStruct((M, N), F32),
        compiler_params=pltpu.CompilerParams(dimension_semantics=("parallel",),
                                             vmem_limit_bytes=V7X_VMEM_LIMIT_BYTES),
        name="moe_combine_residual",
    )(hmid, s_down, x_res, gate, mine, wts)


def _split(z, widths):
    cuts = [int(i) for i in np.cumsum(widths)[:-1]]
    return jnp.split(z, cuts, axis=-1)


def _rms(z, g, eps=NORM_EPS):
    return z * lax.rsqrt(jnp.mean(z * z, axis=-1, keepdims=True) + eps) * g


def _heads(z, n_heads):
    B, S, C = z.shape
    return z.reshape(B, S, n_heads, C // n_heads).transpose(0, 2, 1, 3)


def _merge_heads(z):
    B, H, S, d = z.shape
    return z.transpose(0, 2, 1, 3).reshape(B, S, H * d)


def _rope(z, pos):
    half = ROPE_DIM // 2
    inv_freq = ROPE_THETA ** (-jnp.arange(half, dtype=F32) / half)
    ang = pos.astype(F32)[:, None] * inv_freq[None, :]
    cos = jnp.cos(ang)
    sin = jnp.sin(ang)
    z1, z2, zr = z[..., :half], z[..., half:ROPE_DIM], z[..., ROPE_DIM:]
    return jnp.concatenate([z1 * cos - z2 * sin, z2 * cos + z1 * sin, zr], axis=-1)


def _token_shift(z):
    return jnp.pad(z, ((0, 0), (1, 0), (0, 0)))[:, :-1]


def _rwkv7(p, v_first, v_mix, mu, w0, w_up, a0, a_up, g_up, k_k, k_a, r_k, ln_w, ln_b):
    B, S, _ = p.shape
    H, N = RWKV_HEADS, RWKV_HEAD_DIM
    xs = p + (_token_shift(p) - p) * mu
    r, k, v, xw, xa, xg = _split(xs, (RWKV_DIM, RWKV_DIM, RWKV_DIM, RWKV_DECAY_LORA, RWKV_ICLR_LORA, RWKV_GATE_LORA))
    w = -jax.nn.softplus(-(w0 + jnp.tanh(xw) @ w_up)) - 0.5
    decay = jnp.exp(-jnp.exp(w))
    a = jax.nn.sigmoid(a0 + xa @ a_up)
    g = jax.nn.sigmoid(xg) @ g_up
    kk = (k * k_k).reshape(B, S, H, N)
    kk = kk / jnp.maximum(jnp.sqrt(jnp.sum(kk * kk, axis=-1, keepdims=True)), 1e-12)
    k = k * (1.0 + (a - 1.0) * k_a)
    if v_mix is None:
        v_first = v
    else:
        v0, v_down, v_up = v_mix
        v = v + (v_first - v) * jax.nn.sigmoid(v0 + (v @ v_down) @ v_up)

    def hd(z):
        return z.reshape(B, S, H, N)

    y = _rwkv_scan(hd(r), hd(decay), hd(k), hd(v), kk, kk * hd(a))
    mean = jnp.mean(y, axis=-1, keepdims=True)
    var = jnp.mean(jnp.square(y - mean), axis=-1, keepdims=True)
    y = ((y - mean) * lax.rsqrt(var + RWKV_GN_EPS)).reshape(B, S, RWKV_DIM) * ln_w + ln_b
    bonus = jnp.sum((r * k * r_k).reshape(B, S, H, N), axis=-1, keepdims=True) * v.reshape(B, S, H, N)
    y = (y + bonus.reshape(B, S, RWKV_DIM)) * g
    return y, v_first


def _mlstm_chunkwise(q, k, v, i_pre, logf):
    B, H, S, dk = q.shape
    dv = v.shape[-1]
    L = MLSTM_CHUNK
    NC = S // L
    q = q.reshape(B, H, NC, L, dk)
    k = k.reshape(B, H, NC, L, dk)
    v = v.reshape(B, H, NC, L, dv)
    ic = i_pre.reshape(B, H, NC, L)
    bcum = jnp.cumsum(logf.reshape(B, H, NC, L), axis=-1)
    gtot = bcum[..., -1]
    a_log = gtot[..., None] - bcum + ic
    a_max = jnp.max(a_log, axis=-1)

    def step(carry, inp):
        C, n, m = carry
        g_c, a_c, amax_c, k_c, v_c = inp
        m_new = jnp.maximum(g_c + m, amax_c)
        wgt = jnp.exp(a_c - m_new[..., None])
        dec = jnp.exp(g_c + m - m_new)
        C_new = dec[..., None, None] * C + jnp.einsum('bhl,bhlk,bhlv->bhkv', wgt, k_c, v_c)
        n_new = dec[..., None] * n + jnp.einsum('bhl,bhlk->bhk', wgt, k_c)
        return (C_new, n_new, m_new), (C, n, m)

    init = (jnp.zeros((B, H, dk, dv), F32), jnp.zeros((B, H, dk), F32), jnp.zeros((B, H), F32))
    xs = (jnp.moveaxis(gtot, 2, 0), jnp.moveaxis(a_log, 2, 0), jnp.moveaxis(a_max, 2, 0),
          jnp.moveaxis(k, 2, 0), jnp.moveaxis(v, 2, 0))
    _, (C_prev, n_prev, m_prev) = lax.scan(step, init, xs)
    C_prev = jnp.moveaxis(C_prev, 0, 2)
    n_prev = jnp.moveaxis(n_prev, 0, 2)
    m_prev = jnp.moveaxis(m_prev, 0, 2)

    causal = jnp.tril(jnp.ones((L, L), dtype=bool))
    logD = jnp.where(causal, bcum[..., :, None] - bcum[..., None, :] + ic[..., None, :], -jnp.inf)
    m_inter = bcum + m_prev[..., None]
    m_t = jnp.maximum(m_inter, jnp.max(logD, axis=-1))
    Dm = jnp.exp(logD - m_t[..., None])
    sqk = jnp.einsum('bhctd,bhcsd->bhcts', q, k) * Dm
    inter = jnp.exp(m_inter - m_t)
    num = jnp.einsum('bhcts,bhcsv->bhctv', sqk, v) + inter[..., None] * jnp.einsum('bhctk,bhckv->bhctv', q, C_prev)
    den = jnp.sum(sqk, axis=-1) + inter * jnp.einsum('bhctk,bhck->bhct', q, n_prev)
    h = num / jnp.maximum(jnp.abs(den), jnp.exp(-m_t))[..., None]
    return h.reshape(B, H, S, dv)


def _mlstm(p, p_gates, conv_w, conv_b, i_b, f_b, norm_g):
    H = MLSTM_HEADS
    qk, v, o = _split(p, (2 * MLSTM_QK_WIDTH, MLSTM_DIM, MLSTM_DIM))
    ig, fg = _split(p_gates, (H, H))
    S = qk.shape[1]
    qkp = jnp.pad(qk, ((0, 0), (MLSTM_CONV - 1, 0), (0, 0)))
    conv = conv_b
    for j in range(MLSTM_CONV):
        conv = conv + qkp[:, j:j + S] * conv_w[j]
    qk = jax.nn.silu(conv)
    q, k = jnp.split(qk, 2, axis=-1)
    q = _heads(q, H) * (MLSTM_QK_DIM ** -0.5)
    k = _heads(k, H)
    v = _heads(v, H)
    i_pre = jnp.moveaxis(ig + i_b, -1, 1)
    logf = jax.nn.log_sigmoid(jnp.moveaxis(fg + f_b, -1, 1))
    h = _mlstm_chunkwise(q, k, v, i_pre, logf)
    h = _rms(h, norm_g.reshape(H, 1, MLSTM_V_DIM))
    return jax.nn.sigmoid(o) * _merge_heads(h)


def _moba(q, k, v, k_mean):
    B, H, S, d = q.shape
    BLK = MOBA_BLOCK
    nb = S // BLK
    pos = jnp.arange(S)
    qblk = pos // BLK
    gate = jnp.einsum('bhsd,bnhd->bhsn', q.astype(F32), k_mean, precision=HIGHEST)
    gate = jnp.where(jnp.arange(nb)[None, :] < qblk[:, None], gate, NEG)
    n_top = min(MOBA_TOPK, nb)
    _, sel = lax.top_k(gate, n_top)
    slot_ok = jnp.arange(n_top)[None, :] < qblk[:, None]
    lanes = jnp.arange(LANES)
    hit = (sel[..., None] == lanes) & slot_ok[:, :, None]
    blk_mask = jnp.any(hit, axis=3) | (lanes[None, :] == qblk[:, None])
    return _attention(q, k, v, MOBA_BLOCK, d ** -0.5, block_mask=blk_mask.astype(BF16), block=BLK, out_dtype=BF16)


def _nsa(qb, k_slc, v_slc, k_win, v_win, kc_in, vc_in, gl, pos, k_norm_cmp, cmp_pe, cmp_w1, cmp_b1, cmp_w2):
    B, H, S, d = qb.shape
    scale = d ** -0.5

    ncmp = (S - NSA_CMP_BLOCK) // NSA_CMP_STRIDE + 1
    cstart = NSA_CMP_STRIDE * jnp.arange(ncmp)
    cend = cstart + NSA_CMP_BLOCK - 1
    cidx = cstart[:, None] + jnp.arange(NSA_CMP_BLOCK)[None, :]

    def compress(z, i):
        blocks = (z[:, cidx] + cmp_pe[i]).reshape(B, ncmp, NSA_CMP_BLOCK * d)
        return jax.nn.gelu(blocks @ cmp_w1[i] + cmp_b1[i]) @ cmp_w2[i]

    k_cmp = _rope(_rms(compress(kc_in, 0), k_norm_cmp), cend)
    v_cmp = compress(vc_in, 1)
    SB = NSA_SLC_BLOCK
    nsb = S // SB
    assert nsb <= LANES
    ncp = -(-ncmp // LANES) * LANES
    cs_np = NSA_CMP_STRIDE * np.arange(ncp)
    ss_np = SB * np.arange(LANES)
    overlap = ((cs_np[:, None] <= ss_np[None, :] + SB - 1) & (cs_np[:, None] + NSA_CMP_BLOCK - 1 >= ss_np[None, :])
               & (np.arange(ncp)[:, None] < ncmp) & (np.arange(LANES)[None, :] < nsb))
    cpad = ((0, 0), (0, ncp - ncmp), (0, 0))
    o_cmp, imp = _nsa_cmp_attention(qb, jnp.pad(k_cmp, cpad).astype(BF16), jnp.pad(v_cmp, cpad).astype(BF16),
                                    jnp.asarray(overlap, F32), ncmp, scale)
    imp = imp[:, :, :nsb]
    cur = pos // SB
    blk = jnp.arange(nsb)[None, :]
    forced = (blk == 0) | (blk == cur[:, None]) | (blk == cur[:, None] - 1)
    imp = jnp.where(forced, BIG, jnp.where(blk <= cur[:, None], imp, NEG))
    n_sel = min(NSA_SLC_TOPK, nsb)
    _, sel = lax.top_k(imp, n_sel)
    blk_mask = jnp.any(sel[..., None] == jnp.arange(LANES), axis=2)
    o_slc = _attention(qb, k_slc, v_slc, NSA_Q_TILE, scale, block_mask=blk_mask[:, None].astype(BF16), block=SB)
    o_win = _attention(qb, k_win, v_win, NSA_Q_TILE, scale, window=NSA_WINDOW)

    gates = jax.nn.sigmoid(gl).reshape(B, S, H, 3)

    def per_head(z):
        return z.reshape(B, S, H, d)

    o = (gates[..., 0:1] * per_head(o_cmp) + gates[..., 1:2] * per_head(o_slc) + gates[..., 2:3] * per_head(o_win))
    return o.reshape(B, S, H * d)


def _moe(xb, logits, x_packed, x_res, gate, seq_len, bias, w_gate, w_up, w_down, layer, s_gate, s_up, s_down):
    T, D = xb.shape
    A = T * TOP_K
    eidx, wts, rank, counts = _route(logits, bias)

    experts = jnp.arange(N_EXPERTS, dtype=jnp.int32)
    padded = (counts + MOE_BLOCK - 1) // MOE_BLOCK * MOE_BLOCK
    pstart = jnp.cumsum(padded) - padded
    slot = rank + jnp.sum(jnp.where(eidx[None] == experts[:, None, None], pstart[:, None, None], 0), axis=0)

    n_blk = -(-(A + N_EXPERTS * (MOE_BLOCK - 1)) // MOE_BLOCK)
    R = n_blk * MOE_BLOCK
    blk_e = jnp.minimum(jnp.sum((pstart + padded)[None, :] <= (jnp.arange(n_blk) * MOE_BLOCK)[:, None], axis=1),
                        N_EXPERTS - 1).astype(jnp.int32)
    n_used = (jnp.sum(padded) // MOE_BLOCK).astype(jnp.int32).reshape(1)
    tok_bits = max(int(T - 1).bit_length(), 1)
    tok_ids = jnp.arange(T, dtype=jnp.int32)[None, :]
    real_keys = (eidx << (tok_bits + 1)) | tok_ids
    n_fill = MOE_BLOCK - 1
    fill_j = jnp.arange(n_fill, dtype=jnp.int32)[None, :]
    unused = jnp.int32(N_EXPERTS << (tok_bits + 1))
    fill_keys = jnp.where(fill_j < (padded - counts)[:, None], (experts[:, None] << (tok_bits + 1)) | (1 << tok_bits), unused)
    tail = jnp.full((R - A - N_EXPERTS * n_fill,), unused, jnp.int32)
    keys = lax.sort(jnp.concatenate([real_keys.reshape(A), fill_keys.reshape(-1), tail]))
    is_fill = ((keys >> tok_bits) & 1 == 1) | (keys >= unused)
    tok_buf = jnp.where(is_fill, jnp.arange(R, dtype=jnp.int32) % T, keys & ((1 << tok_bits) - 1))

    packed = _moe_grouped(x_packed[tok_buf], blk_e, n_used, w_gate, w_up, w_down, layer)
    mine = packed[slot.reshape(A)].reshape(TOP_K, T, D // 2)

    gu = _matmul(xb, jnp.concatenate([s_gate, s_up], axis=1).astype(BF16))
    F = s_gate.shape[1]
    hmid = (jax.nn.silu(gu[:, :F]) * gu[:, F:]).astype(BF16)
    return _moe_combine(hmid, s_down.astype(BF16), x_res, gate, seq_len, mine, wts.T)


def _projection_layout():
    o_b = RWKV_IN_DIM
    o_c = o_b + MLSTM_IN_DIM
    o_d = o_c + MOBA_IN_DIM
    o_g = o_d + NSA_IN_DIM
    mlstm_main = MLSTM_IN_DIM - 2 * MLSTM_HEADS
    nsa_main = NSA_IN_DIM - 3 * NSA_HEADS
    spans = (('rwkv', 0, o_b), ('mlstm', o_b, o_b + mlstm_main), ('moba', o_c, o_d), ('nsa', o_d, o_d + nsa_main),
             ('mlstm_gates', o_b + mlstm_main, o_c), ('nsa_gates', o_d + nsa_main, o_g))
    pieces, cur = [], 0
    for name, lo, hi in spans:
        pieces.append((name, lo, hi, cur))
        cur += -(-(hi - lo) // LANES) * LANES
    return tuple(pieces), cur, o_g


def _relayout_kernel(w_ref, mix_ref, gates_ref, *, pieces, gate_lo):
    for _, lo, hi, off in pieces:
        width = hi - lo
        mix_ref[:, off:off + width] = w_ref[0, :, lo:hi].astype(BF16)
        pad = -width % LANES
        if pad:
            mix_ref[:, off + width:off + width + pad] = jnp.zeros((mix_ref.shape[0], pad), BF16)
    gates_ref[...] = w_ref[0, :, gate_lo:].astype(BF16)


def _projection_weights(w_in, layer, rows=128):
    _, D, n_in = w_in.shape
    pieces, mix_width, gate_lo = _projection_layout()
    w_mix, w_gates = pl.pallas_call(
        functools.partial(_relayout_kernel, pieces=pieces, gate_lo=gate_lo),
        grid=(D // rows,),
        in_specs=[pl.BlockSpec((1, rows, n_in), lambda i: (layer, i, 0))],
        out_specs=[pl.BlockSpec((rows, mix_width), lambda i: (i, 0)),
                   pl.BlockSpec((rows, n_in - gate_lo), lambda i: (i, 0))],
        out_shape=[jax.ShapeDtypeStruct((D, mix_width), BF16), jax.ShapeDtypeStruct((D, n_in - gate_lo), BF16)],
        compiler_params=pltpu.CompilerParams(dimension_semantics=("parallel",),
                                             vmem_limit_bytes=V7X_VMEM_LIMIT_BYTES),
        name="projection_weight_relayout",
    )(w_in)
    return w_mix, w_gates, {name: off for name, _, _, off in pieces}


def kernel(x, c, ada_w, ada_b, norm_mix_g, norm_ffn_g, w_in, rwkv_mu, rwkv_w0, rwkv_w_up, rwkv_a0, rwkv_a_up, rwkv_g_up, rwkv_k_k, rwkv_k_a, rwkv_r_k, rwkv_ln_w, rwkv_ln_b, rwkv_v0, rwkv_v_down, rwkv_v_up, mlstm_conv_w, mlstm_conv_b, mlstm_i_b, mlstm_f_b, mlstm_norm_g, moba_q_norm, moba_k_norm, nsa_q_norm, nsa_k_norm, nsa_cmp_pe, nsa_cmp_w1, nsa_cmp_b1, nsa_cmp_w2, w_branch, w_out, moe_router, moe_bias, moe_w_gate, moe_w_up, moe_w_down, shared_w_gate, shared_w_up, shared_w_down):
    B, S, D = x.shape
    T = B * S
    depth = ada_w.shape[0]
    pos = jnp.arange(S)
    cond = jax.nn.silu(c)
    v_first = None
    x2d = x.reshape(T, D)
    for l in range(depth):
        mod = jnp.dot(cond, ada_w[l], precision=HIGHEST) + ada_b[l]
        sh_mix, sc_mix, gate_mix, sh_ffn, sc_ffn, gate_ffn = [z[:, None, :] for z in jnp.split(mod, 6, axis=-1)]

        hb = _normmod(x2d, norm_mix_g[l], sc_mix, sh_mix, S)
        w_mix, w_gates, offs = _projection_weights(w_in, l)

        def proj(name, width, dtype=F32, n_pieces=1):
            padded = -(-width // LANES) * LANES * n_pieces
            return _matmul(hb, w_mix, out_dtype=dtype, cols=(offs[name], padded)).reshape(B, S, padded)

        small = proj('mlstm_gates', 2 * MLSTM_HEADS, n_pieces=2)
        nsa_gate_off = offs['nsa_gates'] - offs['mlstm_gates']
        v_mix = None if l == 0 else (rwkv_v0[l - 1], rwkv_v_down[l - 1], rwkv_v_up[l - 1])
        y_a, v_first = _rwkv7(proj('rwkv', RWKV_IN_DIM)[:, :, :RWKV_IN_DIM], v_first, v_mix, rwkv_mu[l], rwkv_w0[l],
                              rwkv_w_up[l], rwkv_a0[l], rwkv_a_up[l], rwkv_g_up[l], rwkv_k_k[l], rwkv_k_a[l],
                              rwkv_r_k[l], rwkv_ln_w[l], rwkv_ln_b[l])
        y_b = _mlstm(proj('mlstm', MLSTM_IN_DIM - 2 * MLSTM_HEADS), small[:, :, :2 * MLSTM_HEADS],
                     mlstm_conv_w[l], mlstm_conv_b[l], mlstm_i_b[l], mlstm_f_b[l], mlstm_norm_g[l])
        p_moba = proj('moba', MOBA_IN_DIM, BF16).reshape(T, MOBA_IN_DIM)
        hm = MOBA_HEADS
        q_c, k_c, v_c, k_mean = _head_prep(p_moba, jnp.stack([moba_q_norm[l], moba_k_norm[l]]),
                                           ((hm, 0, 0), (hm, hm, 1), (hm, 2 * hm, None)), S, with_mean=True)
        y_c = _moba(q_c, k_c, v_c, k_mean)
        nsa_main = NSA_IN_DIM - 3 * NSA_HEADS
        p_nsa = proj('nsa', nsa_main, BF16)
        hn = NSA_HEADS
        q_d, ks_d, vs_d, kw_d, vw_d = _head_prep(
            p_nsa.reshape(T, nsa_main), jnp.stack([nsa_q_norm[l], nsa_k_norm[l, 1], nsa_k_norm[l, 2]]),
            ((hn, 0, 0), (1, hn + 2, 1), (1, hn + 3, None), (1, hn + 4, 2), (1, hn + 5, None)), S)
        cmp_in = p_nsa[:, :, NSA_DIM:NSA_DIM + 2 * ATTN_HEAD_DIM].astype(F32)
        y_d = _nsa(q_d, ks_d, vs_d, kw_d, vw_d, cmp_in[:, :, :ATTN_HEAD_DIM], cmp_in[:, :, ATTN_HEAD_DIM:],
                   small[:, :, nsa_gate_off:nsa_gate_off + 3 * NSA_HEADS], pos, nsa_k_norm[l, 0],
                   nsa_cmp_pe[l], nsa_cmp_w1[l], nsa_cmp_b1[l], nsa_cmp_w2[l])
        ys = [y.reshape(T, BRANCH_DIM).astype(BF16) for y in (y_a, y_b, y_c, y_d)]
        merged = _merge_branches(hb, w_gates, ys, w_branch[l].astype(BF16))
        x2d = _matmul_residual(merged, w_out[l].astype(BF16), x2d, gate_mix, S)

        hb, logits, hb_packed = _normmod(x2d, norm_ffn_g[l], sc_ffn, sh_ffn, S, router=moe_router[l])
        x2d = _moe(hb, logits, hb_packed, x2d, gate_ffn, S, moe_bias[l], moe_w_gate, moe_w_up, moe_w_down, l,
                   shared_w_gate[l], shared_w_up[l], shared_w_down[l])
    return x2d.reshape(B, S, D)
```

```python
import functools

import numpy as np
import jax
import jax.numpy as jnp
from jax import lax
from jax.experimental import pallas as pl
from jax.experimental.pallas import tpu as pltpu

F32 = jnp.float32
BF16 = jnp.bfloat16
HIGHEST = lax.Precision.HIGHEST

D_MODEL = 2048
N_MIXERS = 4
BRANCH_DIM = 512
NORM_EPS = 1e-6
NEG = -1e30
BIG = 1e9

RWKV_HEADS = 8
RWKV_HEAD_DIM = 64
RWKV_DIM = RWKV_HEADS * RWKV_HEAD_DIM
RWKV_DECAY_LORA = 96
RWKV_ICLR_LORA = 96
RWKV_GATE_LORA = 256
RWKV_GN_EPS = 64e-5
RWKV_IN_DIM = 3 * RWKV_DIM + RWKV_DECAY_LORA + RWKV_ICLR_LORA + RWKV_GATE_LORA

MLSTM_HEADS = 4
MLSTM_QK_DIM = 128
MLSTM_V_DIM = 128
MLSTM_QK_WIDTH = MLSTM_HEADS * MLSTM_QK_DIM
MLSTM_DIM = MLSTM_HEADS * MLSTM_V_DIM
MLSTM_CONV = 4
MLSTM_CHUNK = 64
MLSTM_IN_DIM = 2 * MLSTM_QK_WIDTH + 2 * MLSTM_DIM + 2 * MLSTM_HEADS

ATTN_HEAD_DIM = 128
ROPE_DIM = ATTN_HEAD_DIM // 4
ROPE_THETA = 500000.0

MOBA_HEADS = 4
MOBA_DIM = MOBA_HEADS * ATTN_HEAD_DIM
MOBA_BLOCK = 256
MOBA_TOPK = 3
MOBA_IN_DIM = 3 * MOBA_DIM

NSA_HEADS = 4
NSA_DIM = NSA_HEADS * ATTN_HEAD_DIM
NSA_CMP_BLOCK = 32
NSA_CMP_STRIDE = 16
NSA_SLC_BLOCK = 64
NSA_SLC_TOPK = 16
NSA_WINDOW = 512
NSA_IN_DIM = NSA_DIM + 6 * ATTN_HEAD_DIM + 3 * NSA_HEADS


N_EXPERTS = 64
TOP_K = 8
N_GROUPS = 8
TOPK_GROUPS = 4
EXPERT_DIM = 512
ROUTED_SCALE = 2.5
MOE_BLOCK = 512

V7X_VMEM_LIMIT_BYTES = 48 * 1024 * 1024
LANES = 128
SUBLANES = 8


def _mm_kernel(a_ref, b_ref, o_ref):
    o_ref[...] = jnp.dot(a_ref[...], b_ref[...], preferred_element_type=F32).astype(o_ref.dtype)


def _pick_tile(n, pref):
    t = min(pref, n)
    while n % t:
        t //= 2
    return t


def _pick_lane_tile(n, pref):
    assert n % LANES == 0
    units = n // LANES
    best = 1
    for u in range(1, units + 1):
        if units % u == 0 and u * LANES <= pref:
            best = u
    return best * LANES


def _matmul(a, b, out_dtype=F32, tm=1024, tn=1024, cols=None):
    M, K = a.shape
    off, N = (0, b.shape[1]) if cols is None else cols
    tm = _pick_tile(M, tm)
    tn = _pick_lane_tile(int(np.gcd(N, off)) if off else N, tn)
    first = off // tn
    return pl.pallas_call(
        _mm_kernel,
        grid=(N // tn, M // tm),
        in_specs=[pl.BlockSpec((tm, K), lambda j, i: (i, 0)),
                  pl.BlockSpec((K, tn), lambda j, i: (0, first + j))],
        out_specs=pl.BlockSpec((tm, tn), lambda j, i: (i, j)),
        out_shape=jax.ShapeDtypeStruct((M, N), out_dtype),
        compiler_params=pltpu.CompilerParams(dimension_semantics=("parallel", "parallel"),
                                             vmem_limit_bytes=V7X_VMEM_LIMIT_BYTES),
        name="dense_matmul",
    )(a, b)


def _mm_res_kernel(a_ref, b_ref, res_ref, gate_ref, o_ref):
    y = jnp.dot(a_ref[...], b_ref[...], preferred_element_type=F32)
    o_ref[...] = res_ref[...] + gate_ref[0] * y


def _matmul_residual(a, b, res, gate, seq_len, tm=1024, tn=1024):
    M, K = a.shape
    _, N = b.shape
    tm = _pick_tile(seq_len, tm)
    tn = _pick_lane_tile(N, tn)
    row_tile = pl.BlockSpec((tm, tn), lambda j, i: (i, j))
    return pl.pallas_call(
        _mm_res_kernel,
        grid=(N // tn, M // tm),
        in_specs=[pl.BlockSpec((tm, K), lambda j, i: (i, 0)),
                  pl.BlockSpec((K, tn), lambda j, i: (0, j)),
                  row_tile,
                  pl.BlockSpec((1, 1, tn), lambda j, i: ((i * tm) // seq_len, 0, j))],
        out_specs=row_tile,
        out_shape=jax.ShapeDtypeStruct((M, N), F32),
        compiler_params=pltpu.CompilerParams(dimension_semantics=("parallel", "parallel"),
                                             vmem_limit_bytes=V7X_VMEM_LIMIT_BYTES),
        name="matmul_gated_residual",
    )(a, b, res, gate)


def _normmod_kernel(*refs, with_router):
    if with_router:
        x_ref, g_ref, sc_ref, sh_ref, rt_ref, o_ref, lg_ref, pk_ref = refs
    else:
        x_ref, g_ref, sc_ref, sh_ref, o_ref = refs
    x = x_ref[...]
    y = x * lax.rsqrt(jnp.mean(x * x, axis=-1, keepdims=True) + NORM_EPS) * g_ref[...]
    h = y * (1.0 + sc_ref[0]) + sh_ref[0]
    o_ref[...] = h.astype(o_ref.dtype)
    if with_router:
        lg_ref[...] = lax.dot_general(rt_ref[...], h, (((1,), (1,)), ((), ())),
                                      preferred_element_type=F32, precision=HIGHEST)
        pk_ref[...] = _pack_bf16_halves(h)


def _pack_bf16_halves(y):
    half = y.shape[1] // 2
    lo = pltpu.bitcast(y[:, :half].astype(BF16).astype(F32), jnp.uint32) >> 16
    hi = pltpu.bitcast(y[:, half:].astype(BF16).astype(F32), jnp.uint32) & jnp.uint32(0xFFFF0000)
    return hi | lo


def _unpack_bf16_halves(word):
    return pltpu.bitcast(word << 16, F32), pltpu.bitcast(word & jnp.uint32(0xFFFF0000), F32)


def _normmod(x2d, g, scale, shift, seq_len, router=None, tm=512):
    M, D = x2d.shape
    tm = _pick_tile(seq_len, tm)
    mod_spec = pl.BlockSpec((1, 1, D), lambda i: ((i * tm) // seq_len, 0, 0))
    row_spec = pl.BlockSpec((tm, D), lambda i: (i, 0))
    in_specs = [row_spec, pl.BlockSpec((1, D), lambda i: (0, 0)), mod_spec, mod_spec]
    args = [x2d, g.reshape(1, D), scale, shift]
    out_shape = [jax.ShapeDtypeStruct((M, D), BF16)]
    out_specs = [row_spec]
    if router is not None:
        n_exp = router.shape[1]
        in_specs.append(pl.BlockSpec((n_exp, D), lambda i: (0, 0)))
        args.append(router.T)
        out_shape.append(jax.ShapeDtypeStruct((n_exp, M), F32))
        out_specs.append(pl.BlockSpec((n_exp, tm), lambda i: (0, i)))
        out_shape.append(jax.ShapeDtypeStruct((M, D // 2), jnp.uint32))
        out_specs.append(pl.BlockSpec((tm, D // 2), lambda i: (i, 0)))
    out = pl.pallas_call(
        functools.partial(_normmod_kernel, with_router=router is not None),
        grid=(M // tm,),
        in_specs=in_specs,
        out_specs=out_specs,
        out_shape=out_shape,
        compiler_params=pltpu.CompilerParams(dimension_semantics=("parallel",),
                                             vmem_limit_bytes=V7X_VMEM_LIMIT_BYTES),
        name="rmsnorm_modulate",
    )(*args)
    return tuple(out) if router is not None else out[0]


def _merge_kernel(h_ref, wg0, wg1, wg2, wg3, y0, y1, y2, y3, wb_ref, o_ref):
    h = h_ref[...]
    acc = None
    for m, (wg_ref, y_ref) in enumerate(((wg0, y0), (wg1, y1), (wg2, y2), (wg3, y3))):
        gate = jax.nn.sigmoid(jnp.dot(h, wg_ref[...], preferred_element_type=F32))
        t = jnp.dot(y_ref[...], wb_ref[m], preferred_element_type=F32) * gate
        acc = t if acc is None else acc + t
    o_ref[...] = acc.astype(o_ref.dtype)


def _merge_branches(hb, w_gates, ys, w_branch, tm=1024, tn=512):
    M, K = hb.shape
    n_mix, kb, D = w_branch.shape
    tm = _pick_tile(M, tm)
    tn = _pick_lane_tile(D, tn)
    nj = D // tn
    gate_specs = [pl.BlockSpec((K, tn), functools.partial(lambda j, i, m: (0, m * nj + j), m=m)) for m in range(n_mix)]
    y_specs = [pl.BlockSpec((tm, kb), lambda j, i: (i, 0)) for _ in range(n_mix)]
    return pl.pallas_call(
        _merge_kernel,
        grid=(nj, M // tm),
        in_specs=[pl.BlockSpec((tm, K), lambda j, i: (i, 0))] + gate_specs + y_specs
                 + [pl.BlockSpec((n_mix, kb, tn), lambda j, i: (0, 0, j))],
        out_specs=pl.BlockSpec((tm, tn), lambda j, i: (i, j)),
        out_shape=jax.ShapeDtypeStruct((M, D), BF16),
        compiler_params=pltpu.CompilerParams(dimension_semantics=("parallel", "parallel"),
                                             vmem_limit_bytes=V7X_VMEM_LIMIT_BYTES),
        name="merge_gated_branches",
    )(hb, *([w_gates] * n_mix), *ys, w_branch)


HEAD_PREP_ROWS = 256


def _rope_tables(seq_len):
    half = ROPE_DIM // 2
    inv_freq = ROPE_THETA ** (-jnp.arange(half, dtype=F32) / half)
    ang = jnp.arange(seq_len, dtype=F32)[:, None] * inv_freq[None, :]
    cos, sin = jnp.cos(ang), jnp.sin(ang)
    rest = LANES - ROPE_DIM
    cosf = jnp.concatenate([cos, cos, jnp.ones((seq_len, rest), F32)], axis=1)
    sinf = jnp.concatenate([-sin, sin, jnp.zeros((seq_len, rest), F32)], axis=1)
    return cosf, sinf


def _head_prep_kernel(*refs, groups, with_mean):
    p_ref, g_ref, cos_ref, sin_ref = refs[:4]
    out_refs = refs[4:]
    half = ROPE_DIM // 2
    lane = lax.broadcasted_iota(jnp.int32, (p_ref.shape[0], LANES), 1)
    cosf = cos_ref[...]
    sinf = sin_ref[...]
    for gi, (n_heads, first_tile, gain_row) in enumerate(groups):
        for h in range(n_heads):
            c0 = (first_tile + h) * LANES
            z = p_ref[:, c0:c0 + LANES]
            if gain_row is not None:
                z = z.astype(F32)
                z = z * lax.rsqrt(jnp.mean(z * z, axis=-1, keepdims=True) + NORM_EPS) * g_ref[gain_row:gain_row + 1, :]
                partner = jnp.where(lane < half, pltpu.roll(z, LANES - half, axis=1), pltpu.roll(z, half, axis=1))
                z = z * cosf + partner * sinf
                if with_mean and gi == 1:
                    out_refs[-1][0, 0, h:h + 1, :] = jnp.mean(z, axis=0, keepdims=True)
            out_refs[gi][0, h] = z.astype(BF16)


def _head_prep(p, gains, groups, seq_len, with_mean=False):
    M, W = p.shape
    B = M // seq_len
    tm = _pick_tile(seq_len, HEAD_PREP_ROWS)
    per_seq = seq_len // tm
    cosf, sinf = _rope_tables(seq_len)
    rope_spec = pl.BlockSpec((tm, LANES), lambda i: (i % per_seq, 0))
    out_shape = [jax.ShapeDtypeStruct((B, n, seq_len, LANES), BF16) for n, _, _ in groups]
    out_specs = [pl.BlockSpec((1, n, tm, LANES), lambda i: (i // per_seq, 0, i % per_seq, 0)) for n, _, _ in groups]
    if with_mean:
        n = groups[1][0]
        out_shape.append(jax.ShapeDtypeStruct((B, per_seq, n, LANES), F32))
        out_specs.append(pl.BlockSpec((1, 1, n, LANES), lambda i: (i // per_seq, i % per_seq, 0, 0)))
    return pl.pallas_call(
        functools.partial(_head_prep_kernel, groups=tuple(groups), with_mean=with_mean),
        grid=(M // tm,),
        in_specs=[pl.BlockSpec((tm, W), lambda i: (i, 0)),
                  pl.BlockSpec(gains.shape, lambda i: (0, 0)),
                  rope_spec, rope_spec],
        out_specs=out_specs,
        out_shape=out_shape,
        compiler_params=pltpu.CompilerParams(dimension_semantics=("parallel",),
                                             vmem_limit_bytes=V7X_VMEM_LIMIT_BYTES),
        name="attention_head_prep",
    )(p, gains, cosf, sinf)


ATTN_KEY_CHUNK = 512
NSA_Q_TILE = 128
MASKED_SCORE = -1e30
RUNNING_MAX_FLOOR = -1e20


def _attn_kernel(*refs, tqt, n_rep, kc, window, use_bm, scale):
    if use_bm:
        q_ref, k_ref, v_ref, bm_ref, e_ref, o_ref = refs
    else:
        q_ref, k_ref, v_ref, o_ref = refs
    t0 = pl.program_id(2) * tqt
    d = q_ref.shape[-1]
    rows = n_rep * tqt
    q = q_ref[0].reshape(rows, d)
    dmat = lax.broadcasted_iota(jnp.int32, (tqt, kc), 1) - lax.broadcasted_iota(jnp.int32, (tqt, kc), 0)

    def body(c, carry):
        m, l, acc = carry
        k0 = pl.multiple_of(c * kc, kc)
        kch = k_ref[0, 0, pl.ds(k0, kc), :]
        vch = v_ref[0, 0, pl.ds(k0, kc), :]
        s = lax.dot_general(q, kch, (((1,), (1,)), ((), ())), preferred_element_type=F32) * scale
        off = t0 - k0
        ok = dmat <= off
        if window is not None:
            ok = ok & (dmat > off - window)
        if use_bm:
            ok = ok & (jnp.dot(bm_ref[0, 0], e_ref[c], preferred_element_type=F32) > 0.5)
        bias = jnp.where(ok, 0.0, MASKED_SCORE)
        s = (s.reshape(n_rep, tqt, kc) + bias[None]).reshape(rows, kc)
        m_new = jnp.maximum(m, jnp.max(s, axis=-1, keepdims=True))
        alpha = jnp.exp(m - m_new)
        p = jnp.exp(s - m_new)
        l = alpha * l + jnp.sum(p, axis=-1, keepdims=True)
        acc = alpha * acc + jnp.dot(p.astype(BF16), vch, preferred_element_type=F32)
        return m_new, l, acc

    lo = 0 if window is None else jnp.maximum(t0 - window + 1, 0) // kc
    hi = (t0 + tqt + kc - 1) // kc
    init = (jnp.full((rows, 1), RUNNING_MAX_FLOOR, F32), jnp.zeros((rows, 1), F32), jnp.zeros((rows, d), F32))
    _, l, acc = lax.fori_loop(lo, hi, body, init)
    o = (acc / l).astype(o_ref.dtype)
    for h in range(n_rep):
        o_ref[0, :, h * d:(h + 1) * d] = o[h * tqt:(h + 1) * tqt, :]


def _block_expander(seq_len, kc, block, n_lanes=LANES):
    key_blk = (np.arange(seq_len) // block).reshape(seq_len // kc, 1, kc)
    return jnp.asarray(key_blk == np.arange(n_lanes).reshape(1, n_lanes, 1), dtype=BF16)


def _attention(q, k, v, tqt, scale, block_mask=None, block=None, window=None, out_dtype=F32):
    B, HQ, S, d = q.shape
    G = k.shape[1]
    n_rep = HQ // G
    kc = _pick_tile(S, ATTN_KEY_CHUNK)
    tqt = _pick_tile(S, tqt)
    use_bm = block_mask is not None
    qspec = pl.BlockSpec((1, n_rep, tqt, d), lambda b, g, i: (b, g, i, 0))
    kvspec = pl.BlockSpec((1, 1, S, d), lambda b, g, i: (b, g, 0, 0))
    in_specs = [qspec, kvspec, kvspec]
    args = [q, k, v]
    if use_bm:
        in_specs += [pl.BlockSpec((1, 1, tqt, LANES), lambda b, g, i: (b, g, i, 0)),
                     pl.BlockSpec((S // kc, LANES, kc), lambda b, g, i: (0, 0, 0))]
        args += [block_mask, _block_expander(S, kc, block)]
    return pl.pallas_call(
        functools.partial(_attn_kernel, tqt=tqt, n_rep=n_rep, kc=kc, window=window, use_bm=use_bm, scale=scale),
        grid=(B, G, S // tqt),
        in_specs=in_specs,
        out_specs=pl.BlockSpec((1, tqt, n_rep * d), lambda b, g, i: (b, i, g)),
        out_shape=jax.ShapeDtypeStruct((B, S, HQ * d), out_dtype),
        compiler_params=pltpu.CompilerParams(dimension_semantics=("parallel", "parallel", "parallel"),
                                             vmem_limit_bytes=V7X_VMEM_LIMIT_BYTES),
        name="block_masked_attention",
    )(*args)


def _nsa_cmp_kernel(q_ref, kc_ref, vc_ref, ov_ref, o_ref, imp_ref, *, tqt, n_rep, ncmp, scale):
    t0 = pl.program_id(1) * tqt
    d = q_ref.shape[-1]
    ncp = kc_ref.shape[1]
    rows = n_rep * tqt
    q = q_ref[0].reshape(rows, d)
    s = lax.dot_general(q, kc_ref[0], (((1,), (1,)), ((), ())), preferred_element_type=F32) * scale
    n_io = lax.broadcasted_iota(jnp.int32, (tqt, ncp), 1)
    t_io = t0 + lax.broadcasted_iota(jnp.int32, (tqt, ncp), 0)
    valid = (n_io * NSA_CMP_STRIDE + (NSA_CMP_BLOCK - 1) <= t_io) & (n_io < ncmp)
    bias = jnp.where(valid, 0.0, MASKED_SCORE)
    s = s.reshape(n_rep, tqt, ncp) + bias[None]
    m = jnp.maximum(jnp.max(s, axis=-1, keepdims=True), RUNNING_MAX_FLOOR)
    e = jnp.exp(s - m)
    l = jnp.sum(e, axis=-1, keepdims=True)
    p = e / jnp.where(l > 0.0, l, 1.0)
    o = jnp.dot(p.reshape(rows, ncp).astype(BF16), vc_ref[0], preferred_element_type=F32)
    for h in range(n_rep):
        o_ref[0, :, h * d:(h + 1) * d] = o[h * tqt:(h + 1) * tqt, :]
    imp_ref[0] = jnp.dot(jnp.sum(p, axis=0), ov_ref[...], preferred_element_type=F32, precision=HIGHEST)


def _nsa_cmp_attention(q, k_cmp, v_cmp, overlap, ncmp, scale, tqt=128):
    B, H, S, d = q.shape
    ncp = k_cmp.shape[1]
    tqt = _pick_tile(S, tqt)
    qspec = pl.BlockSpec((1, H, tqt, d), lambda b, i: (b, 0, i, 0))
    cspec = pl.BlockSpec((1, ncp, d), lambda b, i: (b, 0, 0))
    return pl.pallas_call(
        functools.partial(_nsa_cmp_kernel, tqt=tqt, n_rep=H, ncmp=ncmp, scale=scale),
        grid=(B, S // tqt),
        in_specs=[qspec, cspec, cspec, pl.BlockSpec((ncp, LANES), lambda b, i: (0, 0))],
        out_specs=[pl.BlockSpec((1, tqt, H * d), lambda b, i: (b, i, 0)),
                   pl.BlockSpec((1, tqt, LANES), lambda b, i: (b, i, 0))],
        out_shape=[jax.ShapeDtypeStruct((B, S, H * d), F32), jax.ShapeDtypeStruct((B, S, LANES), F32)],
        compiler_params=pltpu.CompilerParams(dimension_semantics=("parallel", "parallel"),
                                             vmem_limit_bytes=V7X_VMEM_LIMIT_BYTES),
        name="nsa_compressed_attention",
    )(q, k_cmp, v_cmp, overlap)


def _nsa_gate_kernel(gl_ref, oc_ref, os_ref, ow_ref, o_ref, *, n_heads, d):
    g = jax.nn.sigmoid(gl_ref[...])
    for h in range(n_heads):
        cols = slice(h * d, (h + 1) * d)
        acc = None
        for j, b_ref in enumerate((oc_ref, os_ref, ow_ref)):
            t = g[:, 3 * h + j:3 * h + j + 1] * b_ref[:, cols]
            acc = t if acc is None else acc + t
        o_ref[:, cols] = acc.astype(o_ref.dtype)


def _nsa_gate(gate_logits, lane_block, o_cmp, o_slc, o_win, n_heads, tm=512):
    M, W = o_cmp.shape
    tm = _pick_tile(M, tm)
    row = pl.BlockSpec((tm, W), lambda i: (i, 0))
    return pl.pallas_call(
        functools.partial(_nsa_gate_kernel, n_heads=n_heads, d=W // n_heads),
        grid=(M // tm,),
        in_specs=[pl.BlockSpec((tm, LANES), lambda i: (i, lane_block)), row, row, row],
        out_specs=row,
        out_shape=jax.ShapeDtypeStruct((M, W), BF16),
        compiler_params=pltpu.CompilerParams(dimension_semantics=("parallel",),
                                             vmem_limit_bytes=V7X_VMEM_LIMIT_BYTES),
        name="nsa_branch_gate",
    )(gate_logits, o_cmp, o_slc, o_win)


RWKV_TIME_CHUNK = 32
RWKV_ROWS = RWKV_HEAD_DIM // 2
RWKV_ROW_GROUPS = RWKV_ROWS // SUBLANES
RWKV_KEY_TILES = RWKV_HEAD_DIM // SUBLANES


def _rwkv_scan_kernel(keys_ref, v_ref, y_ref, st_ref):
    @pl.when(pl.program_id(0) == 0)
    def _():
        st_ref[...] = jnp.zeros_like(st_ref)

    n_t = keys_ref.shape[0]
    sub_iota = lax.broadcasted_iota(jnp.int32, (SUBLANES, st_ref.shape[-1]), 0)

    def step(t, carry):
        r = keys_ref[t, 0]
        w = keys_ref[t, 1]
        k = keys_ref[t, 2]
        kk = keys_ref[t, 3]
        bb = keys_ref[t, 4]

        def group(g, c2):
            y_tile = jnp.zeros(sub_iota.shape, F32)
            for ii in range(SUBLANES):
                i = g * SUBLANES + ii
                s = st_ref[i]
                sa = jnp.sum(jnp.sum(s * kk, axis=0), axis=0, keepdims=True)
                vrow = v_ref[t, g, pl.ds(ii, 1), :]
                s = s * w - sa[None] * bb + vrow[None] * k
                st_ref[i] = s
                yrow = jnp.sum(jnp.sum(s * r, axis=0), axis=0, keepdims=True)
                y_tile = jnp.where(sub_iota == ii, jnp.broadcast_to(yrow, sub_iota.shape), y_tile)
            y_ref[t, g] = y_tile
            return c2

        lax.fori_loop(0, RWKV_ROW_GROUPS, group, 0, unroll=True)
        return carry

    lax.fori_loop(0, n_t, step, 0)


def _rwkv_scan(r, w, k, v, kk, bvec):
    B, S, H, N = r.shape
    chains = B * H
    lanes = 2 * chains

    keys = jnp.transpose(jnp.stack([r, w, k, kk, bvec]), (2, 0, 4, 1, 3)).reshape(S, 5, N, chains)
    keys = jnp.concatenate([keys, keys], axis=-1).reshape(S, 5, RWKV_KEY_TILES, SUBLANES, lanes)

    def val_layout(z):
        zt = jnp.transpose(z.reshape(B, S, H, 2, RWKV_ROWS), (1, 4, 3, 0, 2))
        return zt.reshape(S, RWKV_ROW_GROUPS, SUBLANES, lanes)

    tc = _pick_tile(S, RWKV_TIME_CHUNK)
    kspec = pl.BlockSpec((tc, 5, RWKV_KEY_TILES, SUBLANES, lanes), lambda c: (c, 0, 0, 0, 0))
    vspec = pl.BlockSpec((tc, RWKV_ROW_GROUPS, SUBLANES, lanes), lambda c: (c, 0, 0, 0))
    y = pl.pallas_call(
        _rwkv_scan_kernel,
        grid=(S // tc,),
        in_specs=[kspec, vspec],
        out_specs=vspec,
        out_shape=jax.ShapeDtypeStruct((S, RWKV_ROW_GROUPS, SUBLANES, lanes), F32),
        scratch_shapes=[pltpu.VMEM((RWKV_ROWS, RWKV_KEY_TILES, SUBLANES, lanes), F32)],
        compiler_params=pltpu.CompilerParams(dimension_semantics=("arbitrary",),
                                             vmem_limit_bytes=V7X_VMEM_LIMIT_BYTES),
        name="rwkv7_scan",
    )(keys, val_layout(v))
    y = y.reshape(S, RWKV_ROWS, 2, B, H)
    return jnp.transpose(y, (3, 0, 4, 2, 1)).reshape(B, S, H, N)


ROUTE_TOKENS = 512


def _first_index_of_max(vals, index, n, axes):
    m = vals
    for ax in axes:
        m = jnp.max(m, axis=ax, keepdims=True)
    idx = jnp.where(vals == m, index, float(n))
    for ax in axes:
        idx = jnp.min(idx, axis=ax, keepdims=True)
    return m, idx


def _route_kernel(lg_ref, bias_ref, upper_ref, eidx_ref, wts_ref, rank_ref, cnt_ref, carry_ref):
    @pl.when(pl.program_id(0) == 0)
    def _():
        carry_ref[...] = jnp.zeros_like(carry_ref)

    E, tn = lg_ref.shape
    G = N_GROUPS
    J = E // G
    s3 = jax.nn.sigmoid(lg_ref[...]).reshape(G, J, tn)
    b3 = s3 + bias_ref[...].reshape(G, J, 1)
    j_io = lax.broadcasted_iota(jnp.int32, (G, J, tn), 1).astype(F32)
    g_io = lax.broadcasted_iota(jnp.int32, (G, 1, tn), 0).astype(F32)
    e_io = lax.broadcasted_iota(jnp.int32, (G, J, tn), 0).astype(F32) * J + j_io

    m1, first = _first_index_of_max(b3, j_io, J, (1,))
    m2 = jnp.max(jnp.where(j_io == first, -jnp.inf, b3), axis=1, keepdims=True)
    cur = m1 + m2
    keep = jnp.zeros((G, 1, tn), F32)
    for _ in range(TOPK_GROUPS):
        _, idx = _first_index_of_max(cur, g_io, G, (0,))
        hit = g_io == idx
        keep = jnp.where(hit, 1.0, keep)
        cur = jnp.where(hit, -jnp.inf, cur)

    cur = jnp.where(keep > 0.5, b3, NEG)
    hot = jnp.zeros((G, J, tn), F32)
    picks, weights = [], []
    for _ in range(TOP_K):
        _, idx = _first_index_of_max(cur, e_io, E, (0, 1))
        hit = e_io == idx
        w = jnp.sum(jnp.sum(jnp.where(hit, s3, 0.0), axis=0, keepdims=True), axis=1, keepdims=True)
        cur = jnp.where(hit, -jnp.inf, cur)
        hot = jnp.where(hit, 1.0, hot)
        picks.append(idx)
        weights.append(w)
    wsum = weights[0]
    for w in weights[1:]:
        wsum = wsum + w

    hot2 = hot.reshape(E, tn)
    before = jnp.dot(hot2.astype(BF16), upper_ref[...], preferred_element_type=F32) + carry_ref[:, 0:1]
    before3 = before.reshape(G, J, tn)
    for k in range(TOP_K):
        eidx_ref[k:k + 1, :] = picks[k].reshape(1, tn).astype(jnp.int32)
        wts_ref[k:k + 1, :] = (weights[k] / wsum * ROUTED_SCALE).reshape(1, tn)
        r = jnp.sum(jnp.sum(jnp.where(e_io == picks[k], before3, 0.0), axis=0, keepdims=True), axis=1, keepdims=True)
        rank_ref[k:k + 1, :] = r.reshape(1, tn).astype(jnp.int32)
    carry_ref[...] = carry_ref[...] + jnp.sum(hot2, axis=1, keepdims=True)
    cnt_ref[...] = carry_ref[...]


def _route(logits_t, bias):
    E, T = logits_t.shape
    tn = _pick_tile(T, ROUTE_TOKENS)
    upper = jnp.asarray(np.triu(np.ones((tn, tn), np.float32), 1), BF16)
    tok_spec = pl.BlockSpec((TOP_K, tn), lambda i: (0, i))
    eidx, wts, rank, cnt = pl.pallas_call(
        _route_kernel,
        grid=(T // tn,),
        in_specs=[pl.BlockSpec((E, tn), lambda i: (0, i)),
                  pl.BlockSpec((E, 1), lambda i: (0, 0)),
                  pl.BlockSpec((tn, tn), lambda i: (0, 0))],
        out_specs=[tok_spec, tok_spec, tok_spec, pl.BlockSpec((E, LANES), lambda i: (0, 0))],
        out_shape=[jax.ShapeDtypeStruct((TOP_K, T), jnp.int32), jax.ShapeDtypeStruct((TOP_K, T), F32),
                   jax.ShapeDtypeStruct((TOP_K, T), jnp.int32), jax.ShapeDtypeStruct((E, LANES), F32)],
        scratch_shapes=[pltpu.VMEM((E, LANES), F32)],
        compiler_params=pltpu.CompilerParams(dimension_semantics=("arbitrary",),
                                             vmem_limit_bytes=V7X_VMEM_LIMIT_BYTES),
        name="moe_route",
    )(logits_t, bias.reshape(E, 1), upper)
    return eidx, wts, rank, cnt[:, 0].astype(jnp.int32)


def _moe_kernel(be_ref, nused_ref, x_ref, wg_ref, wu_ref, wd_ref, o_ref, wg_s, wu_s, wd_s):
    i = pl.program_id(0)
    active = i < nused_ref[0]

    @pl.when(active & ((i == 0) | (be_ref[i] != be_ref[jnp.maximum(i - 1, 0)])))
    def _():
        wg_s[...] = wg_ref[0].astype(BF16)
        wu_s[...] = wu_ref[0].astype(BF16)
        wd_s[...] = wd_ref[0].astype(BF16)

    @pl.when(active)
    def _():
        x_lo, x_hi = (z.astype(BF16) for z in _unpack_bf16_halves(x_ref[...]))
        half = x_lo.shape[1]
        g = (jnp.dot(x_lo, wg_s[:half, :], preferred_element_type=F32)
             + jnp.dot(x_hi, wg_s[half:, :], preferred_element_type=F32))
        u = (jnp.dot(x_lo, wu_s[:half, :], preferred_element_type=F32)
             + jnp.dot(x_hi, wu_s[half:, :], preferred_element_type=F32))
        hmid = (g * jax.nn.sigmoid(g) * u).astype(BF16)
        o_ref[...] = _pack_bf16_halves(jnp.dot(hmid, wd_s[...], preferred_element_type=F32))

    @pl.when(jnp.logical_not(active))
    def _():
        o_ref[...] = jnp.zeros_like(o_ref)


def _moe_grouped(xg, blk_e, n_used, w_gate, w_up, w_down, layer):
    R = xg.shape[0]
    D = 2 * xg.shape[1]
    n_blk = R // MOE_BLOCK
    n_layers, E, _, F = w_gate.shape
    w_gate, w_up, w_down = (w.reshape((n_layers * E,) + w.shape[2:]) for w in (w_gate, w_up, w_down))

    def xmap(i, be, nu):
        return (jnp.minimum(i, jnp.maximum(nu[0] - 1, 0)), 0)

    def wmap(i, be, nu):
        return (layer * E + be[i], 0, 0)

    grid_spec = pltpu.PrefetchScalarGridSpec(
        num_scalar_prefetch=2,
        grid=(n_blk,),
        in_specs=[pl.BlockSpec((MOE_BLOCK, D // 2), xmap),
                  pl.BlockSpec((1, D, F), wmap),
                  pl.BlockSpec((1, D, F), wmap),
                  pl.BlockSpec((1, F, D), wmap)],
        out_specs=pl.BlockSpec((MOE_BLOCK, D // 2), lambda i, be, nu: (i, 0)),
        scratch_shapes=[pltpu.VMEM((D, F), BF16), pltpu.VMEM((D, F), BF16), pltpu.VMEM((F, D), BF16)],
    )
    return pl.pallas_call(
        _moe_kernel,
        grid_spec=grid_spec,
        out_shape=jax.ShapeDtypeStruct((R, D // 2), jnp.uint32),
        compiler_params=pltpu.CompilerParams(dimension_semantics=("arbitrary",),
                                             vmem_limit_bytes=V7X_VMEM_LIMIT_BYTES),
        name="moe_grouped_swiglu",
    )(blk_e, n_used, xg, w_gate, w_up, w_down)


MOE_COMBINE_ROWS = 256


def _moe_combine_kernel(a_ref, b_ref, res_ref, gate_ref, mine_ref, w_ref, o_ref):
    half = o_ref.shape[1] // 2
    y = jnp.dot(a_ref[...], b_ref[...], preferred_element_type=F32)
    lo_acc = y[:, :half]
    hi_acc = y[:, half:]
    for k in range(mine_ref.shape[0]):
        lo, hi = _unpack_bf16_halves(mine_ref[k])
        wk = w_ref[:, k:k + 1]
        lo_acc = lo_acc + wk * lo
        hi_acc = hi_acc + wk * hi
    g = gate_ref[0]
    o_ref[:, :half] = res_ref[:, :half] + g[:, :half] * lo_acc
    o_ref[:, half:] = res_ref[:, half:] + g[:, half:] * hi_acc


def _moe_combine(hmid, s_down, x_res, gate, seq_len, mine, wts):
    M, F = hmid.shape
    N = s_down.shape[1]
    n_k = mine.shape[0]
    tm = _pick_tile(seq_len, MOE_COMBINE_ROWS)
    row_tile = pl.BlockSpec((tm, N), lambda i: (i, 0))
    return pl.pallas_call(
        _moe_combine_kernel,
        grid=(M // tm,),
        in_specs=[pl.BlockSpec((tm, F), lambda i: (i, 0)),
                  pl.BlockSpec((F, N), lambda i: (0, 0)),
                  row_tile,
                  pl.BlockSpec((1, 1, N), lambda i: ((i * tm) // seq_len, 0, 0)),
                  pl.BlockSpec((n_k, tm, N // 2), lambda i: (0, i, 0)),
                  pl.BlockSpec((tm, n_k), lambda i: (i, 0))],
        out_specs=row_tile,
        out_shape=jax.ShapeDtypeStruct((M, N), F32),
        compiler_params=pltpu.CompilerParams(dimension_semantics=("parallel",),
                                             vmem_limit_bytes=V7X_VMEM_LIMIT_BYTES),
        name="moe_combine_residual",
    )(hmid, s_down, x_res, gate, mine, wts)


def _split(z, widths):
    cuts = [int(i) for i in np.cumsum(widths)[:-1]]
    return jnp.split(z, cuts, axis=-1)


def _rms(z, g, eps=NORM_EPS):
    return z * lax.rsqrt(jnp.mean(z * z, axis=-1, keepdims=True) + eps) * g


def _heads(z, n_heads):
    B, S, C = z.shape
    return z.reshape(B, S, n_heads, C // n_heads).transpose(0, 2, 1, 3)


def _merge_heads(z):
    B, H, S, d = z.shape
    return z.transpose(0, 2, 1, 3).reshape(B, S, H * d)


def _rope(z, pos):
    half = ROPE_DIM // 2
    inv_freq = ROPE_THETA ** (-jnp.arange(half, dtype=F32) / half)
    ang = pos.astype(F32)[:, None] * inv_freq[None, :]
    cos = jnp.cos(ang)
    sin = jnp.sin(ang)
    z1, z2, zr = z[..., :half], z[..., half:ROPE_DIM], z[..., ROPE_DIM:]
    return jnp.concatenate([z1 * cos - z2 * sin, z2 * cos + z1 * sin, zr], axis=-1)


def _token_shift(z):
    return jnp.pad(z, ((0, 0), (1, 0), (0, 0)))[:, :-1]


def _rwkv7(p, v_first, v_mix, mu, w0, w_up, a0, a_up, g_up, k_k, k_a, r_k, ln_w, ln_b):
    B, S, _ = p.shape
    H, N = RWKV_HEADS, RWKV_HEAD_DIM
    xs = p + (_token_shift(p) - p) * mu
    r, k, v, xw, xa, xg = _split(xs, (RWKV_DIM, RWKV_DIM, RWKV_DIM, RWKV_DECAY_LORA, RWKV_ICLR_LORA, RWKV_GATE_LORA))
    w = -jax.nn.softplus(-(w0 + jnp.tanh(xw) @ w_up)) - 0.5
    decay = jnp.exp(-jnp.exp(w))
    a = jax.nn.sigmoid(a0 + xa @ a_up)
    g = jax.nn.sigmoid(xg) @ g_up
    kk = (k * k_k).reshape(B, S, H, N)
    kk = kk / jnp.maximum(jnp.sqrt(jnp.sum(kk * kk, axis=-1, keepdims=True)), 1e-12)
    k = k * (1.0 + (a - 1.0) * k_a)
    if v_mix is None:
        v_first = v
    else:
        v0, v_down, v_up = v_mix
        v = v + (v_first - v) * jax.nn.sigmoid(v0 + (v @ v_down) @ v_up)

    def hd(z):
        return z.reshape(B, S, H, N)

    y = _rwkv_scan(hd(r), hd(decay), hd(k), hd(v), kk, kk * hd(a))
    mean = jnp.mean(y, axis=-1, keepdims=True)
    var = jnp.mean(jnp.square(y - mean), axis=-1, keepdims=True)
    y = ((y - mean) * lax.rsqrt(var + RWKV_GN_EPS)).reshape(B, S, RWKV_DIM) * ln_w + ln_b
    bonus = jnp.sum((r * k * r_k).reshape(B, S, H, N), axis=-1, keepdims=True) * v.reshape(B, S, H, N)
    y = (y + bonus.reshape(B, S, RWKV_DIM)) * g
    return y, v_first


def _mlstm_chunkwise(q, k, v, i_pre, logf):
    B, H, S, dk = q.shape
    dv = v.shape[-1]
    L = MLSTM_CHUNK
    NC = S // L
    q = q.reshape(B, H, NC, L, dk)
    k = k.reshape(B, H, NC, L, dk)
    v = v.reshape(B, H, NC, L, dv)
    ic = i_pre.reshape(B, H, NC, L)
    bcum = jnp.cumsum(logf.reshape(B, H, NC, L), axis=-1)
    gtot = bcum[..., -1]
    a_log = gtot[..., None] - bcum + ic
    a_max = jnp.max(a_log, axis=-1)

    def step(carry, inp):
        C, n, m = carry
        g_c, a_c, amax_c, k_c, v_c = inp
        m_new = jnp.maximum(g_c + m, amax_c)
        wgt = jnp.exp(a_c - m_new[..., None])
        dec = jnp.exp(g_c + m - m_new)
        C_new = dec[..., None, None] * C + jnp.einsum('bhl,bhlk,bhlv->bhkv', wgt, k_c, v_c)
        n_new = dec[..., None] * n + jnp.einsum('bhl,bhlk->bhk', wgt, k_c)
        return (C_new, n_new, m_new), (C, n, m)

    init = (jnp.zeros((B, H, dk, dv), F32), jnp.zeros((B, H, dk), F32), jnp.zeros((B, H), F32))
    xs = (jnp.moveaxis(gtot, 2, 0), jnp.moveaxis(a_log, 2, 0), jnp.moveaxis(a_max, 2, 0),
          jnp.moveaxis(k, 2, 0), jnp.moveaxis(v, 2, 0))
    _, (C_prev, n_prev, m_prev) = lax.scan(step, init, xs)
    C_prev = jnp.moveaxis(C_prev, 0, 2)
    n_prev = jnp.moveaxis(n_prev, 0, 2)
    m_prev = jnp.moveaxis(m_prev, 0, 2)

    causal = jnp.tril(jnp.ones((L, L), dtype=bool))
    logD = jnp.where(causal, bcum[..., :, None] - bcum[..., None, :] + ic[..., None, :], -jnp.inf)
    m_inter = bcum + m_prev[..., None]
    m_t = jnp.maximum(m_inter, jnp.max(logD, axis=-1))
    Dm = jnp.exp(logD - m_t[..., None])
    sqk = jnp.einsum('bhctd,bhcsd->bhcts', q, k) * Dm
    inter = jnp.exp(m_inter - m_t)
    num = jnp.einsum('bhcts,bhcsv->bhctv', sqk, v) + inter[..., None] * jnp.einsum('bhctk,bhckv->bhctv', q, C_prev)
    den = jnp.sum(sqk, axis=-1) + inter * jnp.einsum('bhctk,bhck->bhct', q, n_prev)
    h = num / jnp.maximum(jnp.abs(den), jnp.exp(-m_t))[..., None]
    return h.reshape(B, H, S, dv)


def _mlstm(p, p_gates, conv_w, conv_b, i_b, f_b, norm_g):
    H = MLSTM_HEADS
    qk, v, o = _split(p, (2 * MLSTM_QK_WIDTH, MLSTM_DIM, MLSTM_DIM))
    ig, fg = _split(p_gates, (H, H))
    S = qk.shape[1]
    qkp = jnp.pad(qk, ((0, 0), (MLSTM_CONV - 1, 0), (0, 0)))
    conv = conv_b
    for j in range(MLSTM_CONV):
        conv = conv + qkp[:, j:j + S] * conv_w[j]
    qk = jax.nn.silu(conv)
    q, k = jnp.split(qk, 2, axis=-1)
    q = _heads(q, H) * (MLSTM_QK_DIM ** -0.5)
    k = _heads(k, H)
    v = _heads(v, H)
    i_pre = jnp.moveaxis(ig + i_b, -1, 1)
    logf = jax.nn.log_sigmoid(jnp.moveaxis(fg + f_b, -1, 1))
    h = _mlstm_chunkwise(q, k, v, i_pre, logf)
    h = _rms(h, norm_g.reshape(H, 1, MLSTM_V_DIM))
    return jax.nn.sigmoid(o) * _merge_heads(h)


def _moba(q, k, v, k_mean):
    B, H, S, d = q.shape
    BLK = MOBA_BLOCK
    nb = S // BLK
    pos = jnp.arange(S)
    qblk = pos // BLK
    gate = jnp.einsum('bhsd,bnhd->bhsn', q.astype(F32), k_mean, precision=HIGHEST)
    gate = jnp.where(jnp.arange(nb)[None, :] < qblk[:, None], gate, NEG)
    n_top = min(MOBA_TOPK, nb)
    _, sel = lax.top_k(gate, n_top)
    slot_ok = jnp.arange(n_top)[None, :] < qblk[:, None]
    lanes = jnp.arange(LANES)
    hit = (sel[..., None] == lanes) & slot_ok[:, :, None]
    blk_mask = jnp.any(hit, axis=3) | (lanes[None, :] == qblk[:, None])
    return _attention(q, k, v, MOBA_BLOCK, d ** -0.5, block_mask=blk_mask.astype(BF16), block=BLK, out_dtype=BF16)


def _nsa(qb, k_slc, v_slc, k_win, v_win, kc_in, vc_in, gl, pos, k_norm_cmp, cmp_pe, cmp_w1, cmp_b1, cmp_w2):
    B, H, S, d = qb.shape
    scale = d ** -0.5

    ncmp = (S - NSA_CMP_BLOCK) // NSA_CMP_STRIDE + 1
    cstart = NSA_CMP_STRIDE * jnp.arange(ncmp)
    cend = cstart + NSA_CMP_BLOCK - 1
    cidx = cstart[:, None] + jnp.arange(NSA_CMP_BLOCK)[None, :]

    def compress(z, i):
        blocks = (z[:, cidx] + cmp_pe[i]).reshape(B, ncmp, NSA_CMP_BLOCK * d)
        return jax.nn.gelu(blocks @ cmp_w1[i] + cmp_b1[i]) @ cmp_w2[i]

    k_cmp = _rope(_rms(compress(kc_in, 0), k_norm_cmp), cend)
    v_cmp = compress(vc_in, 1)
    SB = NSA_SLC_BLOCK
    nsb = S // SB
    assert nsb <= LANES
    ncp = -(-ncmp // LANES) * LANES
    cs_np = NSA_CMP_STRIDE * np.arange(ncp)
    ss_np = SB * np.arange(LANES)
    overlap = ((cs_np[:, None] <= ss_np[None, :] + SB - 1) & (cs_np[:, None] + NSA_CMP_BLOCK - 1 >= ss_np[None, :])
               & (np.arange(ncp)[:, None] < ncmp) & (np.arange(LANES)[None, :] < nsb))
    cpad = ((0, 0), (0, ncp - ncmp), (0, 0))
    o_cmp, imp = _nsa_cmp_attention(qb, jnp.pad(k_cmp, cpad).astype(BF16), jnp.pad(v_cmp, cpad).astype(BF16),
                                    jnp.asarray(overlap, F32), ncmp, scale)
    imp = imp[:, :, :nsb]
    cur = pos // SB
    blk = jnp.arange(nsb)[None, :]
    forced = (blk == 0) | (blk == cur[:, None]) | (blk == cur[:, None] - 1)
    imp = jnp.where(forced, BIG, jnp.where(blk <= cur[:, None], imp, NEG))
    n_sel = min(NSA_SLC_TOPK, nsb)
    _, sel = lax.top_k(imp, n_sel)
    blk_mask = jnp.any(sel[..., None] == jnp.arange(LANES), axis=2)
    o_slc = _attention(qb, k_slc, v_slc, NSA_Q_TILE, scale, block_mask=blk_mask[:, None].astype(BF16), block=SB)
    o_win = _attention(qb, k_win, v_win, NSA_Q_TILE, scale, window=NSA_WINDOW)

    gate_logits, lane_block = gl

    def rows(z):
        return z.reshape(B * S, H * d)

    return _nsa_gate(gate_logits, lane_block, rows(o_cmp), rows(o_slc), rows(o_win), H).reshape(B, S, H * d)


def _moe(xb, logits, x_packed, x_res, gate, seq_len, bias, w_gate, w_up, w_down, layer, s_gate, s_up, s_down):
    T, D = xb.shape
    A = T * TOP_K
    eidx, wts, rank, counts = _route(logits, bias)

    experts = jnp.arange(N_EXPERTS, dtype=jnp.int32)
    padded = (counts + MOE_BLOCK - 1) // MOE_BLOCK * MOE_BLOCK
    pstart = jnp.cumsum(padded) - padded
    slot = rank + jnp.sum(jnp.where(eidx[None] == experts[:, None, None], pstart[:, None, None], 0), axis=0)

    n_blk = -(-(A + N_EXPERTS * (MOE_BLOCK - 1)) // MOE_BLOCK)
    R = n_blk * MOE_BLOCK
    blk_e = jnp.minimum(jnp.sum((pstart + padded)[None, :] <= (jnp.arange(n_blk) * MOE_BLOCK)[:, None], axis=1),
                        N_EXPERTS - 1).astype(jnp.int32)
    n_used = (jnp.sum(padded) // MOE_BLOCK).astype(jnp.int32).reshape(1)
    tok_bits = max(int(T - 1).bit_length(), 1)
    tok_ids = jnp.arange(T, dtype=jnp.int32)[None, :]
    real_keys = (eidx << (tok_bits + 1)) | tok_ids
    n_fill = MOE_BLOCK - 1
    fill_j = jnp.arange(n_fill, dtype=jnp.int32)[None, :]
    unused = jnp.int32(N_EXPERTS << (tok_bits + 1))
    fill_keys = jnp.where(fill_j < (padded - counts)[:, None], (experts[:, None] << (tok_bits + 1)) | (1 << tok_bits), unused)
    tail = jnp.full((R - A - N_EXPERTS * n_fill,), unused, jnp.int32)
    keys = lax.sort(jnp.concatenate([real_keys.reshape(A), fill_keys.reshape(-1), tail]))
    is_fill = ((keys >> tok_bits) & 1 == 1) | (keys >= unused)
    tok_buf = jnp.where(is_fill, jnp.arange(R, dtype=jnp.int32) % T, keys & ((1 << tok_bits) - 1))

    packed = _moe_grouped(x_packed[tok_buf], blk_e, n_used, w_gate, w_up, w_down, layer)
    mine = packed[slot.reshape(A)].reshape(TOP_K, T, D // 2)

    gu = _matmul(xb, jnp.concatenate([s_gate, s_up], axis=1).astype(BF16))
    F = s_gate.shape[1]
    hmid = (jax.nn.silu(gu[:, :F]) * gu[:, F:]).astype(BF16)
    return _moe_combine(hmid, s_down.astype(BF16), x_res, gate, seq_len, mine, wts.T)


def _projection_layout():
    o_b = RWKV_IN_DIM
    o_c = o_b + MLSTM_IN_DIM
    o_d = o_c + MOBA_IN_DIM
    o_g = o_d + NSA_IN_DIM
    mlstm_main = MLSTM_IN_DIM - 2 * MLSTM_HEADS
    nsa_main = NSA_IN_DIM - 3 * NSA_HEADS
    spans = (('rwkv', 0, o_b), ('mlstm', o_b, o_b + mlstm_main), ('moba', o_c, o_d), ('nsa', o_d, o_d + nsa_main),
             ('mlstm_gates', o_b + mlstm_main, o_c), ('nsa_gates', o_d + nsa_main, o_g))
    pieces, cur = [], 0
    for name, lo, hi in spans:
        pieces.append((name, lo, hi, cur))
        cur += -(-(hi - lo) // LANES) * LANES
    return tuple(pieces), cur, o_g


def _relayout_kernel(w_ref, mix_ref, gates_ref, *, pieces, gate_lo):
    for _, lo, hi, off in pieces:
        width = hi - lo
        mix_ref[:, off:off + width] = w_ref[0, :, lo:hi].astype(BF16)
        pad = -width % LANES
        if pad:
            mix_ref[:, off + width:off + width + pad] = jnp.zeros((mix_ref.shape[0], pad), BF16)
    gates_ref[...] = w_ref[0, :, gate_lo:].astype(BF16)


def _projection_weights(w_in, layer, rows=128):
    _, D, n_in = w_in.shape
    pieces, mix_width, gate_lo = _projection_layout()
    w_mix, w_gates = pl.pallas_call(
        functools.partial(_relayout_kernel, pieces=pieces, gate_lo=gate_lo),
        grid=(D // rows,),
        in_specs=[pl.BlockSpec((1, rows, n_in), lambda i: (layer, i, 0))],
        out_specs=[pl.BlockSpec((rows, mix_width), lambda i: (i, 0)),
                   pl.BlockSpec((rows, n_in - gate_lo), lambda i: (i, 0))],
        out_shape=[jax.ShapeDtypeStruct((D, mix_width), BF16), jax.ShapeDtypeStruct((D, n_in - gate_lo), BF16)],
        compiler_params=pltpu.CompilerParams(dimension_semantics=("parallel",),
                                             vmem_limit_bytes=V7X_VMEM_LIMIT_BYTES),
        name="projection_weight_relayout",
    )(w_in)
    return w_mix, w_gates, {name: off for name, _, _, off in pieces}


def kernel(x, c, ada_w, ada_b, norm_mix_g, norm_ffn_g, w_in, rwkv_mu, rwkv_w0, rwkv_w_up, rwkv_a0, rwkv_a_up, rwkv_g_up, rwkv_k_k, rwkv_k_a, rwkv_r_k, rwkv_ln_w, rwkv_ln_b, rwkv_v0, rwkv_v_down, rwkv_v_up, mlstm_conv_w, mlstm_conv_b, mlstm_i_b, mlstm_f_b, mlstm_norm_g, moba_q_norm, moba_k_norm, nsa_q_norm, nsa_k_norm, nsa_cmp_pe, nsa_cmp_w1, nsa_cmp_b1, nsa_cmp_w2, w_branch, w_out, moe_router, moe_bias, moe_w_gate, moe_w_up, moe_w_down, shared_w_gate, shared_w_up, shared_w_down):
    B, S, D = x.shape
    T = B * S
    depth = ada_w.shape[0]
    pos = jnp.arange(S)
    cond = jax.nn.silu(c)
    v_first = None
    x2d = x.reshape(T, D)
    for l in range(depth):
        mod = jnp.dot(cond, ada_w[l], precision=HIGHEST) + ada_b[l]
        sh_mix, sc_mix, gate_mix, sh_ffn, sc_ffn, gate_ffn = [z[:, None, :] for z in jnp.split(mod, 6, axis=-1)]

        hb = _normmod(x2d, norm_mix_g[l], sc_mix, sh_mix, S)
        w_mix, w_gates, offs = _projection_weights(w_in, l)

        def proj(name, width, dtype=F32, n_pieces=1):
            padded = -(-width // LANES) * LANES * n_pieces
            return _matmul(hb, w_mix, out_dtype=dtype, cols=(offs[name], padded)).reshape(B, S, padded)

        small = proj('mlstm_gates', 2 * MLSTM_HEADS, n_pieces=2)
        nsa_gate_off = offs['nsa_gates'] - offs['mlstm_gates']
        v_mix = None if l == 0 else (rwkv_v0[l - 1], rwkv_v_down[l - 1], rwkv_v_up[l - 1])
        y_a, v_first = _rwkv7(proj('rwkv', RWKV_IN_DIM)[:, :, :RWKV_IN_DIM], v_first, v_mix, rwkv_mu[l], rwkv_w0[l],
                              rwkv_w_up[l], rwkv_a0[l], rwkv_a_up[l], rwkv_g_up[l], rwkv_k_k[l], rwkv_k_a[l],
                              rwkv_r_k[l], rwkv_ln_w[l], rwkv_ln_b[l])
        y_b = _mlstm(proj('mlstm', MLSTM_IN_DIM - 2 * MLSTM_HEADS), small[:, :, :2 * MLSTM_HEADS],
                     mlstm_conv_w[l], mlstm_conv_b[l], mlstm_i_b[l], mlstm_f_b[l], mlstm_norm_g[l])
        p_moba = proj('moba', MOBA_IN_DIM, BF16).reshape(T, MOBA_IN_DIM)
        hm = MOBA_HEADS
        q_c, k_c, v_c, k_mean = _head_prep(p_moba, jnp.stack([moba_q_norm[l], moba_k_norm[l]]),
                                           ((hm, 0, 0), (hm, hm, 1), (hm, 2 * hm, None)), S, with_mean=True)
        y_c = _moba(q_c, k_c, v_c, k_mean)
        nsa_main = NSA_IN_DIM - 3 * NSA_HEADS
        p_nsa = proj('nsa', nsa_main, BF16)
        hn = NSA_HEADS
        q_d, ks_d, vs_d, kw_d, vw_d = _head_prep(
            p_nsa.reshape(T, nsa_main), jnp.stack([nsa_q_norm[l], nsa_k_norm[l, 1], nsa_k_norm[l, 2]]),
            ((hn, 0, 0), (1, hn + 2, 1), (1, hn + 3, None), (1, hn + 4, 2), (1, hn + 5, None)), S)
        cmp_in = p_nsa[:, :, NSA_DIM:NSA_DIM + 2 * ATTN_HEAD_DIM].astype(F32)
        y_d = _nsa(q_d, ks_d, vs_d, kw_d, vw_d, cmp_in[:, :, :ATTN_HEAD_DIM], cmp_in[:, :, ATTN_HEAD_DIM:],
                   (small.reshape(T, -1), nsa_gate_off // LANES), pos, nsa_k_norm[l, 0],
                   nsa_cmp_pe[l], nsa_cmp_w1[l], nsa_cmp_b1[l], nsa_cmp_w2[l])
        ys = [y.reshape(T, BRANCH_DIM).astype(BF16) for y in (y_a, y_b, y_c, y_d)]
        merged = _merge_branches(hb, w_gates, ys, w_branch[l].astype(BF16))
        x2d = _matmul_residual(merged, w_out[l].astype(BF16), x2d, gate_mix, S)

        hb, logits, hb_packed = _normmod(x2d, norm_ffn_g[l], sc_ffn, sh_ffn, S, router=moe_router[l])
        x2d = _moe(hb, logits, hb_packed, x2d, gate_ffn, S, moe_bias[l], moe_w_gate, moe_w_up, moe_w_down, l,
                   shared_w_gate[l], shared_w_up[l], shared_w_down[l])
    return x2d.reshape(B, S, D)
```

```python
import functools

import numpy as np
import jax
import jax.numpy as jnp
from jax import lax
from jax.experimental import pallas as pl
from jax.experimental.pallas import tpu as pltpu

F32 = jnp.float32
BF16 = jnp.bfloat16
HIGHEST = lax.Precision.HIGHEST

D_MODEL = 2048
N_MIXERS = 4
BRANCH_DIM = 512
NORM_EPS = 1e-6
NEG = -1e30
BIG = 1e9

RWKV_HEADS = 8
RWKV_HEAD_DIM = 64
RWKV_DIM = RWKV_HEADS * RWKV_HEAD_DIM
RWKV_DECAY_LORA = 96
RWKV_ICLR_LORA = 96
RWKV_GATE_LORA = 256
RWKV_GN_EPS = 64e-5
RWKV_IN_DIM = 3 * RWKV_DIM + RWKV_DECAY_LORA + RWKV_ICLR_LORA + RWKV_GATE_LORA

MLSTM_HEADS = 4
MLSTM_QK_DIM = 128
MLSTM_V_DIM = 128
MLSTM_QK_WIDTH = MLSTM_HEADS * MLSTM_QK_DIM
MLSTM_DIM = MLSTM_HEADS * MLSTM_V_DIM
MLSTM_CONV = 4
MLSTM_CHUNK = 64
MLSTM_IN_DIM = 2 * MLSTM_QK_WIDTH + 2 * MLSTM_DIM + 2 * MLSTM_HEADS

ATTN_HEAD_DIM = 128
ROPE_DIM = ATTN_HEAD_DIM // 4
ROPE_THETA = 500000.0

MOBA_HEADS = 4
MOBA_DIM = MOBA_HEADS * ATTN_HEAD_DIM
MOBA_BLOCK = 256
MOBA_TOPK = 3
MOBA_IN_DIM = 3 * MOBA_DIM

NSA_HEADS = 4
NSA_DIM = NSA_HEADS * ATTN_HEAD_DIM
NSA_CMP_BLOCK = 32
NSA_CMP_STRIDE = 16
NSA_SLC_BLOCK = 64
NSA_SLC_TOPK = 16
NSA_WINDOW = 512
NSA_IN_DIM = NSA_DIM + 6 * ATTN_HEAD_DIM + 3 * NSA_HEADS


N_EXPERTS = 64
TOP_K = 8
N_GROUPS = 8
TOPK_GROUPS = 4
EXPERT_DIM = 512
ROUTED_SCALE = 2.5
MOE_BLOCK = 512

V7X_VMEM_LIMIT_BYTES = 48 * 1024 * 1024
LANES = 128
SUBLANES = 8


def _mm_kernel(a_ref, b_ref, o_ref):
    o_ref[...] = jnp.dot(a_ref[...], b_ref[...], preferred_element_type=F32).astype(o_ref.dtype)


def _pick_tile(n, pref):
    t = min(pref, n)
    while n % t:
        t //= 2
    return t


def _pick_lane_tile(n, pref):
    assert n % LANES == 0
    units = n // LANES
    best = 1
    for u in range(1, units + 1):
        if units % u == 0 and u * LANES <= pref:
            best = u
    return best * LANES


def _matmul(a, b, out_dtype=F32, tm=1024, tn=1024, cols=None):
    M, K = a.shape
    off, N = (0, b.shape[1]) if cols is None else cols
    tm = _pick_tile(M, tm)
    tn = _pick_lane_tile(int(np.gcd(N, off)) if off else N, tn)
    first = off // tn
    return pl.pallas_call(
        _mm_kernel,
        grid=(N // tn, M // tm),
        in_specs=[pl.BlockSpec((tm, K), lambda j, i: (i, 0)),
                  pl.BlockSpec((K, tn), lambda j, i: (0, first + j))],
        out_specs=pl.BlockSpec((tm, tn), lambda j, i: (i, j)),
        out_shape=jax.ShapeDtypeStruct((M, N), out_dtype),
        compiler_params=pltpu.CompilerParams(dimension_semantics=("parallel", "parallel"),
                                             vmem_limit_bytes=V7X_VMEM_LIMIT_BYTES),
        name="dense_matmul",
    )(a, b)


def _mm_res_kernel(a_ref, b_ref, res_ref, gate_ref, o_ref):
    y = jnp.dot(a_ref[...], b_ref[...], preferred_element_type=F32)
    o_ref[...] = res_ref[...] + gate_ref[0] * y


def _matmul_residual(a, b, res, gate, seq_len, tm=1024, tn=1024):
    M, K = a.shape
    _, N = b.shape
    tm = _pick_tile(seq_len, tm)
    tn = _pick_lane_tile(N, tn)
    row_tile = pl.BlockSpec((tm, tn), lambda j, i: (i, j))
    return pl.pallas_call(
        _mm_res_kernel,
        grid=(N // tn, M // tm),
        in_specs=[pl.BlockSpec((tm, K), lambda j, i: (i, 0)),
                  pl.BlockSpec((K, tn), lambda j, i: (0, j)),
                  row_tile,
                  pl.BlockSpec((1, 1, tn), lambda j, i: ((i * tm) // seq_len, 0, j))],
        out_specs=row_tile,
        out_shape=jax.ShapeDtypeStruct((M, N), F32),
        compiler_params=pltpu.CompilerParams(dimension_semantics=("parallel", "parallel"),
                                             vmem_limit_bytes=V7X_VMEM_LIMIT_BYTES),
        name="matmul_gated_residual",
    )(a, b, res, gate)


def _normmod_kernel(*refs, with_router):
    if with_router:
        x_ref, g_ref, sc_ref, sh_ref, rt_ref, o_ref, lg_ref, pk_ref = refs
    else:
        x_ref, g_ref, sc_ref, sh_ref, o_ref = refs
    x = x_ref[...]
    y = x * lax.rsqrt(jnp.mean(x * x, axis=-1, keepdims=True) + NORM_EPS) * g_ref[...]
    h = y * (1.0 + sc_ref[0]) + sh_ref[0]
    o_ref[...] = h.astype(o_ref.dtype)
    if with_router:
        lg_ref[...] = lax.dot_general(rt_ref[...], h, (((1,), (1,)), ((), ())),
                                      preferred_element_type=F32, precision=HIGHEST)
        pk_ref[...] = _pack_bf16_halves(h)


def _pack_bf16_halves(y):
    half = y.shape[1] // 2
    lo = pltpu.bitcast(y[:, :half].astype(BF16).astype(F32), jnp.uint32) >> 16
    hi = pltpu.bitcast(y[:, half:].astype(BF16).astype(F32), jnp.uint32) & jnp.uint32(0xFFFF0000)
    return hi | lo


def _unpack_bf16_halves(word):
    return pltpu.bitcast(word << 16, F32), pltpu.bitcast(word & jnp.uint32(0xFFFF0000), F32)


def _normmod(x2d, g, scale, shift, seq_len, router=None, tm=512):
    M, D = x2d.shape
    tm = _pick_tile(seq_len, tm)
    mod_spec = pl.BlockSpec((1, 1, D), lambda i: ((i * tm) // seq_len, 0, 0))
    row_spec = pl.BlockSpec((tm, D), lambda i: (i, 0))
    in_specs = [row_spec, pl.BlockSpec((1, D), lambda i: (0, 0)), mod_spec, mod_spec]
    args = [x2d, g.reshape(1, D), scale, shift]
    out_shape = [jax.ShapeDtypeStruct((M, D), BF16)]
    out_specs = [row_spec]
    if router is not None:
        n_exp = router.shape[1]
        in_specs.append(pl.BlockSpec((n_exp, D), lambda i: (0, 0)))
        args.append(router.T)
        out_shape.append(jax.ShapeDtypeStruct((n_exp, M), F32))
        out_specs.append(pl.BlockSpec((n_exp, tm), lambda i: (0, i)))
        out_shape.append(jax.ShapeDtypeStruct((M, D // 2), jnp.uint32))
        out_specs.append(pl.BlockSpec((tm, D // 2), lambda i: (i, 0)))
    out = pl.pallas_call(
        functools.partial(_normmod_kernel, with_router=router is not None),
        grid=(M // tm,),
        in_specs=in_specs,
        out_specs=out_specs,
        out_shape=out_shape,
        compiler_params=pltpu.CompilerParams(dimension_semantics=("parallel",),
                                             vmem_limit_bytes=V7X_VMEM_LIMIT_BYTES),
        name="rmsnorm_modulate",
    )(*args)
    return tuple(out) if router is not None else out[0]


def _merge_kernel(h_ref, wg0, wg1, wg2, wg3, y0, y1, y2, y3, wb_ref, o_ref):
    h = h_ref[...]
    acc = None
    for m, (wg_ref, y_ref) in enumerate(((wg0, y0), (wg1, y1), (wg2, y2), (wg3, y3))):
        gate = jax.nn.sigmoid(jnp.dot(h, wg_ref[...], preferred_element_type=F32))
        t = jnp.dot(y_ref[...], wb_ref[m], preferred_element_type=F32) * gate
        acc = t if acc is None else acc + t
    o_ref[...] = acc.astype(o_ref.dtype)


def _merge_branches(hb, w_gates, ys, w_branch, tm=1024, tn=512):
    M, K = hb.shape
    n_mix, kb, D = w_branch.shape
    tm = _pick_tile(M, tm)
    tn = _pick_lane_tile(D, tn)
    nj = D // tn
    gate_specs = [pl.BlockSpec((K, tn), functools.partial(lambda j, i, m: (0, m * nj + j), m=m)) for m in range(n_mix)]
    y_specs = [pl.BlockSpec((tm, kb), lambda j, i: (i, 0)) for _ in range(n_mix)]
    return pl.pallas_call(
        _merge_kernel,
        grid=(nj, M // tm),
        in_specs=[pl.BlockSpec((tm, K), lambda j, i: (i, 0))] + gate_specs + y_specs
                 + [pl.BlockSpec((n_mix, kb, tn), lambda j, i: (0, 0, j))],
        out_specs=pl.BlockSpec((tm, tn), lambda j, i: (i, j)),
        out_shape=jax.ShapeDtypeStruct((M, D), BF16),
        compiler_params=pltpu.CompilerParams(dimension_semantics=("parallel", "parallel"),
                                             vmem_limit_bytes=V7X_VMEM_LIMIT_BYTES),
        name="merge_gated_branches",
    )(hb, *([w_gates] * n_mix), *ys, w_branch)


HEAD_PREP_ROWS = 256


def _rope_tables(seq_len):
    half = ROPE_DIM // 2
    inv_freq = ROPE_THETA ** (-jnp.arange(half, dtype=F32) / half)
    ang = jnp.arange(seq_len, dtype=F32)[:, None] * inv_freq[None, :]
    cos, sin = jnp.cos(ang), jnp.sin(ang)
    rest = LANES - ROPE_DIM
    cosf = jnp.concatenate([cos, cos, jnp.ones((seq_len, rest), F32)], axis=1)
    sinf = jnp.concatenate([-sin, sin, jnp.zeros((seq_len, rest), F32)], axis=1)
    return cosf, sinf


def _head_prep_kernel(*refs, groups, with_mean):
    p_ref, g_ref, cos_ref, sin_ref = refs[:4]
    out_refs = refs[4:]
    half = ROPE_DIM // 2
    lane = lax.broadcasted_iota(jnp.int32, (p_ref.shape[0], LANES), 1)
    cosf = cos_ref[...]
    sinf = sin_ref[...]
    for gi, (n_heads, first_tile, gain_row) in enumerate(groups):
        for h in range(n_heads):
            c0 = (first_tile + h) * LANES
            z = p_ref[:, c0:c0 + LANES]
            if gain_row is not None:
                z = z.astype(F32)
                z = z * lax.rsqrt(jnp.mean(z * z, axis=-1, keepdims=True) + NORM_EPS) * g_ref[gain_row:gain_row + 1, :]
                partner = jnp.where(lane < half, pltpu.roll(z, LANES - half, axis=1), pltpu.roll(z, half, axis=1))
                z = z * cosf + partner * sinf
                if with_mean and gi == 1:
                    out_refs[-1][0, 0, h:h + 1, :] = jnp.mean(z, axis=0, keepdims=True)
            out_refs[gi][0, h] = z.astype(BF16)


def _head_prep(p, gains, groups, seq_len, with_mean=False):
    M, W = p.shape
    B = M // seq_len
    tm = _pick_tile(seq_len, HEAD_PREP_ROWS)
    per_seq = seq_len // tm
    cosf, sinf = _rope_tables(seq_len)
    rope_spec = pl.BlockSpec((tm, LANES), lambda i: (i % per_seq, 0))
    out_shape = [jax.ShapeDtypeStruct((B, n, seq_len, LANES), BF16) for n, _, _ in groups]
    out_specs = [pl.BlockSpec((1, n, tm, LANES), lambda i: (i // per_seq, 0, i % per_seq, 0)) for n, _, _ in groups]
    if with_mean:
        n = groups[1][0]
        out_shape.append(jax.ShapeDtypeStruct((B, per_seq, n, LANES), F32))
        out_specs.append(pl.BlockSpec((1, 1, n, LANES), lambda i: (i // per_seq, i % per_seq, 0, 0)))
    return pl.pallas_call(
        functools.partial(_head_prep_kernel, groups=tuple(groups), with_mean=with_mean),
        grid=(M // tm,),
        in_specs=[pl.BlockSpec((tm, W), lambda i: (i, 0)),
                  pl.BlockSpec(gains.shape, lambda i: (0, 0)),
                  rope_spec, rope_spec],
        out_specs=out_specs,
        out_shape=out_shape,
        compiler_params=pltpu.CompilerParams(dimension_semantics=("parallel",),
                                             vmem_limit_bytes=V7X_VMEM_LIMIT_BYTES),
        name="attention_head_prep",
    )(p, gains, cosf, sinf)


ATTN_KEY_CHUNK = 512
NSA_Q_TILE = 128
MASKED_SCORE = -1e30
RUNNING_MAX_FLOOR = -1e20


def _attn_kernel(*refs, tqt, n_rep, kc, window, use_bm, scale):
    if use_bm:
        q_ref, k_ref, v_ref, bm_ref, e_ref, o_ref = refs
    else:
        q_ref, k_ref, v_ref, o_ref = refs
    t0 = pl.program_id(2) * tqt
    d = q_ref.shape[-1]
    rows = n_rep * tqt
    q = q_ref[0].reshape(rows, d)
    dmat = lax.broadcasted_iota(jnp.int32, (tqt, kc), 1) - lax.broadcasted_iota(jnp.int32, (tqt, kc), 0)

    def body(c, carry):
        m, l, acc = carry
        k0 = pl.multiple_of(c * kc, kc)
        kch = k_ref[0, 0, pl.ds(k0, kc), :]
        vch = v_ref[0, 0, pl.ds(k0, kc), :]
        s = lax.dot_general(q, kch, (((1,), (1,)), ((), ())), preferred_element_type=F32) * scale
        off = t0 - k0
        ok = dmat <= off
        if window is not None:
            ok = ok & (dmat > off - window)
        if use_bm:
            ok = ok & (jnp.dot(bm_ref[0, 0], e_ref[c], preferred_element_type=F32) > 0.5)
        bias = jnp.where(ok, 0.0, MASKED_SCORE)
        s = (s.reshape(n_rep, tqt, kc) + bias[None]).reshape(rows, kc)
        m_new = jnp.maximum(m, jnp.max(s, axis=-1, keepdims=True))
        alpha = jnp.exp(m - m_new)
        p = jnp.exp(s - m_new)
        l = alpha * l + jnp.sum(p, axis=-1, keepdims=True)
        acc = alpha * acc + jnp.dot(p.astype(BF16), vch, preferred_element_type=F32)
        return m_new, l, acc

    lo = 0 if window is None else jnp.maximum(t0 - window + 1, 0) // kc
    hi = (t0 + tqt + kc - 1) // kc
    init = (jnp.full((rows, 1), RUNNING_MAX_FLOOR, F32), jnp.zeros((rows, 1), F32), jnp.zeros((rows, d), F32))
    _, l, acc = lax.fori_loop(lo, hi, body, init)
    o = (acc / l).astype(o_ref.dtype)
    for h in range(n_rep):
        o_ref[0, :, h * d:(h + 1) * d] = o[h * tqt:(h + 1) * tqt, :]


def _block_expander(seq_len, kc, block, n_lanes=LANES):
    key_blk = (np.arange(seq_len) // block).reshape(seq_len // kc, 1, kc)
    return jnp.asarray(key_blk == np.arange(n_lanes).reshape(1, n_lanes, 1), dtype=BF16)


def _attention(q, k, v, tqt, scale, block_mask=None, block=None, window=None, out_dtype=F32):
    B, HQ, S, d = q.shape
    G = k.shape[1]
    n_rep = HQ // G
    kc = _pick_tile(S, ATTN_KEY_CHUNK)
    tqt = _pick_tile(S, tqt)
    use_bm = block_mask is not None
    qspec = pl.BlockSpec((1, n_rep, tqt, d), lambda b, g, i: (b, g, i, 0))
    kvspec = pl.BlockSpec((1, 1, S, d), lambda b, g, i: (b, g, 0, 0))
    in_specs = [qspec, kvspec, kvspec]
    args = [q, k, v]
    if use_bm:
        in_specs += [pl.BlockSpec((1, 1, tqt, LANES), lambda b, g, i: (b, g, i, 0)),
                     pl.BlockSpec((S // kc, LANES, kc), lambda b, g, i: (0, 0, 0))]
        args += [block_mask, _block_expander(S, kc, block)]
    return pl.pallas_call(
        functools.partial(_attn_kernel, tqt=tqt, n_rep=n_rep, kc=kc, window=window, use_bm=use_bm, scale=scale),
        grid=(B, G, S // tqt),
        in_specs=in_specs,
        out_specs=pl.BlockSpec((1, tqt, n_rep * d), lambda b, g, i: (b, i, g)),
        out_shape=jax.ShapeDtypeStruct((B, S, HQ * d), out_dtype),
        compiler_params=pltpu.CompilerParams(dimension_semantics=("parallel", "parallel", "parallel"),
                                             vmem_limit_bytes=V7X_VMEM_LIMIT_BYTES),
        name="block_masked_attention",
    )(*args)


def _nsa_cmp_kernel(q_ref, kc_ref, vc_ref, ov_ref, o_ref, imp_ref, *, tqt, n_rep, ncmp, scale):
    t0 = pl.program_id(1) * tqt
    d = q_ref.shape[-1]
    ncp = kc_ref.shape[1]
    rows = n_rep * tqt
    q = q_ref[0].reshape(rows, d)
    s = lax.dot_general(q, kc_ref[0], (((1,), (1,)), ((), ())), preferred_element_type=F32) * scale
    n_io = lax.broadcasted_iota(jnp.int32, (tqt, ncp), 1)
    t_io = t0 + lax.broadcasted_iota(jnp.int32, (tqt, ncp), 0)
    valid = (n_io * NSA_CMP_STRIDE + (NSA_CMP_BLOCK - 1) <= t_io) & (n_io < ncmp)
    bias = jnp.where(valid, 0.0, MASKED_SCORE)
    s = s.reshape(n_rep, tqt, ncp) + bias[None]
    m = jnp.maximum(jnp.max(s, axis=-1, keepdims=True), RUNNING_MAX_FLOOR)
    e = jnp.exp(s - m)
    l = jnp.sum(e, axis=-1, keepdims=True)
    p = e / jnp.where(l > 0.0, l, 1.0)
    o = jnp.dot(p.reshape(rows, ncp).astype(BF16), vc_ref[0], preferred_element_type=F32)
    for h in range(n_rep):
        o_ref[0, :, h * d:(h + 1) * d] = o[h * tqt:(h + 1) * tqt, :]
    imp_ref[0] = jnp.dot(jnp.sum(p, axis=0), ov_ref[...], preferred_element_type=F32, precision=HIGHEST)


def _nsa_cmp_attention(q, k_cmp, v_cmp, overlap, ncmp, scale, tqt=128):
    B, H, S, d = q.shape
    ncp = k_cmp.shape[1]
    tqt = _pick_tile(S, tqt)
    qspec = pl.BlockSpec((1, H, tqt, d), lambda b, i: (b, 0, i, 0))
    cspec = pl.BlockSpec((1, ncp, d), lambda b, i: (b, 0, 0))
    return pl.pallas_call(
        functools.partial(_nsa_cmp_kernel, tqt=tqt, n_rep=H, ncmp=ncmp, scale=scale),
        grid=(B, S // tqt),
        in_specs=[qspec, cspec, cspec, pl.BlockSpec((ncp, LANES), lambda b, i: (0, 0))],
        out_specs=[pl.BlockSpec((1, tqt, H * d), lambda b, i: (b, i, 0)),
                   pl.BlockSpec((1, tqt, LANES), lambda b, i: (b, i, 0))],
        out_shape=[jax.ShapeDtypeStruct((B, S, H * d), F32), jax.ShapeDtypeStruct((B, S, LANES), F32)],
        compiler_params=pltpu.CompilerParams(dimension_semantics=("parallel", "parallel"),
                                             vmem_limit_bytes=V7X_VMEM_LIMIT_BYTES),
        name="nsa_compressed_attention",
    )(q, k_cmp, v_cmp, overlap)


def _nsa_gate_kernel(gl_ref, oc_ref, os_ref, ow_ref, o_ref, *, n_heads, d):
    g = jax.nn.sigmoid(gl_ref[...])
    for h in range(n_heads):
        cols = slice(h * d, (h + 1) * d)
        acc = None
        for j, b_ref in enumerate((oc_ref, os_ref, ow_ref)):
            t = g[:, 3 * h + j:3 * h + j + 1] * b_ref[:, cols]
            acc = t if acc is None else acc + t
        o_ref[:, cols] = acc.astype(o_ref.dtype)


def _nsa_gate(gate_logits, lane_block, o_cmp, o_slc, o_win, n_heads, tm=512):
    M, W = o_cmp.shape
    tm = _pick_tile(M, tm)
    row = pl.BlockSpec((tm, W), lambda i: (i, 0))
    return pl.pallas_call(
        functools.partial(_nsa_gate_kernel, n_heads=n_heads, d=W // n_heads),
        grid=(M // tm,),
        in_specs=[pl.BlockSpec((tm, LANES), lambda i: (i, lane_block)), row, row, row],
        out_specs=row,
        out_shape=jax.ShapeDtypeStruct((M, W), BF16),
        compiler_params=pltpu.CompilerParams(dimension_semantics=("parallel",),
                                             vmem_limit_bytes=V7X_VMEM_LIMIT_BYTES),
        name="nsa_branch_gate",
    )(gate_logits, o_cmp, o_slc, o_win)


RWKV_TIME_CHUNK = 32
RWKV_ROWS = RWKV_HEAD_DIM // 2
RWKV_ROW_GROUPS = RWKV_ROWS // SUBLANES
RWKV_KEY_TILES = RWKV_HEAD_DIM // SUBLANES


def _rwkv_scan_kernel(keys_ref, v_ref, y_ref, st_ref):
    @pl.when(pl.program_id(0) == 0)
    def _():
        st_ref[...] = jnp.zeros_like(st_ref)

    n_t = keys_ref.shape[0]
    sub_iota = lax.broadcasted_iota(jnp.int32, (SUBLANES, st_ref.shape[-1]), 0)

    def step(t, carry):
        r = keys_ref[t, 0]
        w = keys_ref[t, 1]
        k = keys_ref[t, 2]
        kk = keys_ref[t, 3]
        bb = keys_ref[t, 4]

        def group(g, c2):
            y_tile = jnp.zeros(sub_iota.shape, F32)
            for ii in range(SUBLANES):
                i = g * SUBLANES + ii
                s = st_ref[i]
                sa = jnp.sum(jnp.sum(s * kk, axis=0), axis=0, keepdims=True)
                vrow = v_ref[t, g, pl.ds(ii, 1), :]
                s = s * w - sa[None] * bb + vrow[None] * k
                st_ref[i] = s
                yrow = jnp.sum(jnp.sum(s * r, axis=0), axis=0, keepdims=True)
                y_tile = jnp.where(sub_iota == ii, jnp.broadcast_to(yrow, sub_iota.shape), y_tile)
            y_ref[t, g] = y_tile
            return c2

        lax.fori_loop(0, RWKV_ROW_GROUPS, group, 0, unroll=True)
        return carry

    lax.fori_loop(0, n_t, step, 0)


def _rwkv_scan(r, w, k, v, kk, bvec):
    B, S, H, N = r.shape
    chains = B * H
    lanes = 2 * chains

    keys = jnp.transpose(jnp.stack([r, w, k, kk, bvec]), (2, 0, 4, 1, 3)).reshape(S, 5, N, chains)
    keys = jnp.concatenate([keys, keys], axis=-1).reshape(S, 5, RWKV_KEY_TILES, SUBLANES, lanes)

    def val_layout(z):
        zt = jnp.transpose(z.reshape(B, S, H, 2, RWKV_ROWS), (1, 4, 3, 0, 2))
        return zt.reshape(S, RWKV_ROW_GROUPS, SUBLANES, lanes)

    tc = _pick_tile(S, RWKV_TIME_CHUNK)
    kspec = pl.BlockSpec((tc, 5, RWKV_KEY_TILES, SUBLANES, lanes), lambda c: (c, 0, 0, 0, 0))
    vspec = pl.BlockSpec((tc, RWKV_ROW_GROUPS, SUBLANES, lanes), lambda c: (c, 0, 0, 0))
    y = pl.pallas_call(
        _rwkv_scan_kernel,
        grid=(S // tc,),
        in_specs=[kspec, vspec],
        out_specs=vspec,
        out_shape=jax.ShapeDtypeStruct((S, RWKV_ROW_GROUPS, SUBLANES, lanes), F32),
        scratch_shapes=[pltpu.VMEM((RWKV_ROWS, RWKV_KEY_TILES, SUBLANES, lanes), F32)],
        compiler_params=pltpu.CompilerParams(dimension_semantics=("arbitrary",),
                                             vmem_limit_bytes=V7X_VMEM_LIMIT_BYTES),
        name="rwkv7_scan",
    )(keys, val_layout(v))
    y = y.reshape(S, RWKV_ROWS, 2, B, H)
    return jnp.transpose(y, (3, 0, 4, 2, 1)).reshape(B, S, H, N)


ROUTE_TOKENS = 512


def _first_index_of_max(vals, index, n, axes):
    m = vals
    for ax in axes:
        m = jnp.max(m, axis=ax, keepdims=True)
    idx = jnp.where(vals == m, index, float(n))
    for ax in axes:
        idx = jnp.min(idx, axis=ax, keepdims=True)
    return m, idx


def _route_kernel(lg_ref, bias_ref, upper_ref, eidx_ref, wts_ref, rank_ref, cnt_ref, carry_ref):
    @pl.when(pl.program_id(0) == 0)
    def _():
        carry_ref[...] = jnp.zeros_like(carry_ref)

    E, tn = lg_ref.shape
    G = N_GROUPS
    J = E // G
    s3 = jax.nn.sigmoid(lg_ref[...]).reshape(G, J, tn)
    b3 = s3 + bias_ref[...].reshape(G, J, 1)
    j_io = lax.broadcasted_iota(jnp.int32, (G, J, tn), 1).astype(F32)
    g_io = lax.broadcasted_iota(jnp.int32, (G, 1, tn), 0).astype(F32)
    e_io = lax.broadcasted_iota(jnp.int32, (G, J, tn), 0).astype(F32) * J + j_io

    m1, first = _first_index_of_max(b3, j_io, J, (1,))
    m2 = jnp.max(jnp.where(j_io == first, -jnp.inf, b3), axis=1, keepdims=True)
    cur = m1 + m2
    keep = jnp.zeros((G, 1, tn), F32)
    for _ in range(TOPK_GROUPS):
        _, idx = _first_index_of_max(cur, g_io, G, (0,))
        hit = g_io == idx
        keep = jnp.where(hit, 1.0, keep)
        cur = jnp.where(hit, -jnp.inf, cur)

    cur = jnp.where(keep > 0.5, b3, NEG)
    hot = jnp.zeros((G, J, tn), F32)
    picks, weights = [], []
    for _ in range(TOP_K):
        _, idx = _first_index_of_max(cur, e_io, E, (0, 1))
        hit = e_io == idx
        w = jnp.sum(jnp.sum(jnp.where(hit, s3, 0.0), axis=0, keepdims=True), axis=1, keepdims=True)
        cur = jnp.where(hit, -jnp.inf, cur)
        hot = jnp.where(hit, 1.0, hot)
        picks.append(idx)
        weights.append(w)
    wsum = weights[0]
    for w in weights[1:]:
        wsum = wsum + w

    hot2 = hot.reshape(E, tn)
    before = jnp.dot(hot2.astype(BF16), upper_ref[...], preferred_element_type=F32) + carry_ref[:, 0:1]
    before3 = before.reshape(G, J, tn)
    for k in range(TOP_K):
        eidx_ref[k:k + 1, :] = picks[k].reshape(1, tn).astype(jnp.int32)
        wts_ref[k:k + 1, :] = (weights[k] / wsum * ROUTED_SCALE).reshape(1, tn)
        r = jnp.sum(jnp.sum(jnp.where(e_io == picks[k], before3, 0.0), axis=0, keepdims=True), axis=1, keepdims=True)
        rank_ref[k:k + 1, :] = r.reshape(1, tn).astype(jnp.int32)
    carry_ref[...] = carry_ref[...] + jnp.sum(hot2, axis=1, keepdims=True)
    cnt_ref[...] = carry_ref[...]


def _route(logits_t, bias):
    E, T = logits_t.shape
    tn = _pick_tile(T, ROUTE_TOKENS)
    upper = jnp.asarray(np.triu(np.ones((tn, tn), np.float32), 1), BF16)
    tok_spec = pl.BlockSpec((TOP_K, tn), lambda i: (0, i))
    eidx, wts, rank, cnt = pl.pallas_call(
        _route_kernel,
        grid=(T // tn,),
        in_specs=[pl.BlockSpec((E, tn), lambda i: (0, i)),
                  pl.BlockSpec((E, 1), lambda i: (0, 0)),
                  pl.BlockSpec((tn, tn), lambda i: (0, 0))],
        out_specs=[tok_spec, tok_spec, tok_spec, pl.BlockSpec((E, LANES), lambda i: (0, 0))],
        out_shape=[jax.ShapeDtypeStruct((TOP_K, T), jnp.int32), jax.ShapeDtypeStruct((TOP_K, T), F32),
                   jax.ShapeDtypeStruct((TOP_K, T), jnp.int32), jax.ShapeDtypeStruct((E, LANES), F32)],
        scratch_shapes=[pltpu.VMEM((E, LANES), F32)],
        compiler_params=pltpu.CompilerParams(dimension_semantics=("arbitrary",),
                                             vmem_limit_bytes=V7X_VMEM_LIMIT_BYTES),
        name="moe_route",
    )(logits_t, bias.reshape(E, 1), upper)
    return eidx, wts, rank, cnt[:, 0].astype(jnp.int32)


def _moe_kernel(be_ref, nused_ref, x_ref, wg_ref, wu_ref, wd_ref, o_ref, wg_s, wu_s, wd_s):
    i = pl.program_id(0)
    active = i < nused_ref[0]

    @pl.when(active & ((i == 0) | (be_ref[i] != be_ref[jnp.maximum(i - 1, 0)])))
    def _():
        wg_s[...] = wg_ref[0].astype(BF16)
        wu_s[...] = wu_ref[0].astype(BF16)
        wd_s[...] = wd_ref[0].astype(BF16)

    @pl.when(active)
    def _():
        x_lo, x_hi = (z.astype(BF16) for z in _unpack_bf16_halves(x_ref[...]))
        half = x_lo.shape[1]
        g = (jnp.dot(x_lo, wg_s[:half, :], preferred_element_type=F32)
             + jnp.dot(x_hi, wg_s[half:, :], preferred_element_type=F32))
        u = (jnp.dot(x_lo, wu_s[:half, :], preferred_element_type=F32)
             + jnp.dot(x_hi, wu_s[half:, :], preferred_element_type=F32))
        hmid = (g * jax.nn.sigmoid(g) * u).astype(BF16)
        o_ref[...] = _pack_bf16_halves(jnp.dot(hmid, wd_s[...], preferred_element_type=F32))

    @pl.when(jnp.logical_not(active))
    def _():
        o_ref[...] = jnp.zeros_like(o_ref)


def _moe_grouped(xg, blk_e, n_used, w_gate, w_up, w_down, layer):
    R = xg.shape[0]
    D = 2 * xg.shape[1]
    n_blk = R // MOE_BLOCK
    n_layers, E, _, F = w_gate.shape
    w_gate, w_up, w_down = (w.reshape((n_layers * E,) + w.shape[2:]) for w in (w_gate, w_up, w_down))

    def xmap(i, be, nu):
        return (jnp.minimum(i, jnp.maximum(nu[0] - 1, 0)), 0)

    def wmap(i, be, nu):
        return (layer * E + be[i], 0, 0)

    grid_spec = pltpu.PrefetchScalarGridSpec(
        num_scalar_prefetch=2,
        grid=(n_blk,),
        in_specs=[pl.BlockSpec((MOE_BLOCK, D // 2), xmap),
                  pl.BlockSpec((1, D, F), wmap),
                  pl.BlockSpec((1, D, F), wmap),
                  pl.BlockSpec((1, F, D), wmap)],
        out_specs=pl.BlockSpec((MOE_BLOCK, D // 2), lambda i, be, nu: (i, 0)),
        scratch_shapes=[pltpu.VMEM((D, F), BF16), pltpu.VMEM((D, F), BF16), pltpu.VMEM((F, D), BF16)],
    )
    return pl.pallas_call(
        _moe_kernel,
        grid_spec=grid_spec,
        out_shape=jax.ShapeDtypeStruct((R, D // 2), jnp.uint32),
        compiler_params=pltpu.CompilerParams(dimension_semantics=("arbitrary",),
                                             vmem_limit_bytes=V7X_VMEM_LIMIT_BYTES),
        name="moe_grouped_swiglu",
    )(blk_e, n_used, xg, w_gate, w_up, w_down)


MOE_COMBINE_ROWS = 256


def _moe_combine_kernel(a_ref, b_ref, res_ref, gate_ref, mine_ref, w_ref, o_ref):
    half = o_ref.shape[1] // 2
    f = a_ref.shape[1] // 2
    gate_pre = a_ref[:, :f]
    hmid = (gate_pre * jax.nn.sigmoid(gate_pre) * a_ref[:, f:]).astype(BF16)
    y = jnp.dot(hmid, b_ref[...], preferred_element_type=F32)
    lo_acc = y[:, :half]
    hi_acc = y[:, half:]
    for k in range(mine_ref.shape[0]):
        lo, hi = _unpack_bf16_halves(mine_ref[k])
        wk = w_ref[:, k:k + 1]
        lo_acc = lo_acc + wk * lo
        hi_acc = hi_acc + wk * hi
    g = gate_ref[0]
    o_ref[:, :half] = res_ref[:, :half] + g[:, :half] * lo_acc
    o_ref[:, half:] = res_ref[:, half:] + g[:, half:] * hi_acc


def _moe_combine(gu, s_down, x_res, gate, seq_len, mine, wts):
    M, F2 = gu.shape
    F = F2 // 2
    N = s_down.shape[1]
    n_k = mine.shape[0]
    tm = _pick_tile(seq_len, MOE_COMBINE_ROWS)
    row_tile = pl.BlockSpec((tm, N), lambda i: (i, 0))
    return pl.pallas_call(
        _moe_combine_kernel,
        grid=(M // tm,),
        in_specs=[pl.BlockSpec((tm, F2), lambda i: (i, 0)),
                  pl.BlockSpec((F, N), lambda i: (0, 0)),
                  row_tile,
                  pl.BlockSpec((1, 1, N), lambda i: ((i * tm) // seq_len, 0, 0)),
                  pl.BlockSpec((n_k, tm, N // 2), lambda i: (0, i, 0)),
                  pl.BlockSpec((tm, n_k), lambda i: (i, 0))],
        out_specs=row_tile,
        out_shape=jax.ShapeDtypeStruct((M, N), F32),
        compiler_params=pltpu.CompilerParams(dimension_semantics=("parallel",),
                                             vmem_limit_bytes=V7X_VMEM_LIMIT_BYTES),
        name="moe_combine_residual",
    )(gu, s_down, x_res, gate, mine, wts)


def _split(z, widths):
    cuts = [int(i) for i in np.cumsum(widths)[:-1]]
    return jnp.split(z, cuts, axis=-1)


def _rms(z, g, eps=NORM_EPS):
    return z * lax.rsqrt(jnp.mean(z * z, axis=-1, keepdims=True) + eps) * g


def _heads(z, n_heads):
    B, S, C = z.shape
    return z.reshape(B, S, n_heads, C // n_heads).transpose(0, 2, 1, 3)


def _merge_heads(z):
    B, H, S, d = z.shape
    return z.transpose(0, 2, 1, 3).reshape(B, S, H * d)


def _rope(z, pos):
    half = ROPE_DIM // 2
    inv_freq = ROPE_THETA ** (-jnp.arange(half, dtype=F32) / half)
    ang = pos.astype(F32)[:, None] * inv_freq[None, :]
    cos = jnp.cos(ang)
    sin = jnp.sin(ang)
    z1, z2, zr = z[..., :half], z[..., half:ROPE_DIM], z[..., ROPE_DIM:]
    return jnp.concatenate([z1 * cos - z2 * sin, z2 * cos + z1 * sin, zr], axis=-1)


def _token_shift(z):
    return jnp.pad(z, ((0, 0), (1, 0), (0, 0)))[:, :-1]


def _rwkv7(p, v_first, v_mix, mu, w0, w_up, a0, a_up, g_up, k_k, k_a, r_k, ln_w, ln_b):
    B, S, _ = p.shape
    H, N = RWKV_HEADS, RWKV_HEAD_DIM
    xs = p + (_token_shift(p) - p) * mu
    r, k, v, xw, xa, xg = _split(xs, (RWKV_DIM, RWKV_DIM, RWKV_DIM, RWKV_DECAY_LORA, RWKV_ICLR_LORA, RWKV_GATE_LORA))
    w = -jax.nn.softplus(-(w0 + jnp.tanh(xw) @ w_up)) - 0.5
    decay = jnp.exp(-jnp.exp(w))
    a = jax.nn.sigmoid(a0 + xa @ a_up)
    g = jax.nn.sigmoid(xg) @ g_up
    kk = (k * k_k).reshape(B, S, H, N)
    kk = kk / jnp.maximum(jnp.sqrt(jnp.sum(kk * kk, axis=-1, keepdims=True)), 1e-12)
    k = k * (1.0 + (a - 1.0) * k_a)
    if v_mix is None:
        v_first = v
    else:
        v0, v_down, v_up = v_mix
        v = v + (v_first - v) * jax.nn.sigmoid(v0 + (v @ v_down) @ v_up)

    def hd(z):
        return z.reshape(B, S, H, N)

    y = _rwkv_scan(hd(r), hd(decay), hd(k), hd(v), kk, kk * hd(a))
    mean = jnp.mean(y, axis=-1, keepdims=True)
    var = jnp.mean(jnp.square(y - mean), axis=-1, keepdims=True)
    y = ((y - mean) * lax.rsqrt(var + RWKV_GN_EPS)).reshape(B, S, RWKV_DIM) * ln_w + ln_b
    bonus = jnp.sum((r * k * r_k).reshape(B, S, H, N), axis=-1, keepdims=True) * v.reshape(B, S, H, N)
    y = (y + bonus.reshape(B, S, RWKV_DIM)) * g
    return y, v_first


def _mlstm_chunkwise(q, k, v, i_pre, logf):
    B, H, S, dk = q.shape
    dv = v.shape[-1]
    L = MLSTM_CHUNK
    NC = S // L
    q = q.reshape(B, H, NC, L, dk)
    k = k.reshape(B, H, NC, L, dk)
    v = v.reshape(B, H, NC, L, dv)
    ic = i_pre.reshape(B, H, NC, L)
    bcum = jnp.cumsum(logf.reshape(B, H, NC, L), axis=-1)
    gtot = bcum[..., -1]
    a_log = gtot[..., None] - bcum + ic
    a_max = jnp.max(a_log, axis=-1)

    def step(carry, inp):
        C, n, m = carry
        g_c, a_c, amax_c, k_c, v_c = inp
        m_new = jnp.maximum(g_c + m, amax_c)
        wgt = jnp.exp(a_c - m_new[..., None])
        dec = jnp.exp(g_c + m - m_new)
        C_new = dec[..., None, None] * C + jnp.einsum('bhl,bhlk,bhlv->bhkv', wgt, k_c, v_c)
        n_new = dec[..., None] * n + jnp.einsum('bhl,bhlk->bhk', wgt, k_c)
        return (C_new, n_new, m_new), (C, n, m)

    init = (jnp.zeros((B, H, dk, dv), F32), jnp.zeros((B, H, dk), F32), jnp.zeros((B, H), F32))
    xs = (jnp.moveaxis(gtot, 2, 0), jnp.moveaxis(a_log, 2, 0), jnp.moveaxis(a_max, 2, 0),
          jnp.moveaxis(k, 2, 0), jnp.moveaxis(v, 2, 0))
    _, (C_prev, n_prev, m_prev) = lax.scan(step, init, xs)
    C_prev = jnp.moveaxis(C_prev, 0, 2)
    n_prev = jnp.moveaxis(n_prev, 0, 2)
    m_prev = jnp.moveaxis(m_prev, 0, 2)

    causal = jnp.tril(jnp.ones((L, L), dtype=bool))
    logD = jnp.where(causal, bcum[..., :, None] - bcum[..., None, :] + ic[..., None, :], -jnp.inf)
    m_inter = bcum + m_prev[..., None]
    m_t = jnp.maximum(m_inter, jnp.max(logD, axis=-1))
    Dm = jnp.exp(logD - m_t[..., None])
    sqk = jnp.einsum('bhctd,bhcsd->bhcts', q, k) * Dm
    inter = jnp.exp(m_inter - m_t)
    num = jnp.einsum('bhcts,bhcsv->bhctv', sqk, v) + inter[..., None] * jnp.einsum('bhctk,bhckv->bhctv', q, C_prev)
    den = jnp.sum(sqk, axis=-1) + inter * jnp.einsum('bhctk,bhck->bhct', q, n_prev)
    h = num / jnp.maximum(jnp.abs(den), jnp.exp(-m_t))[..., None]
    return h.reshape(B, H, S, dv)


def _mlstm(p, p_gates, conv_w, conv_b, i_b, f_b, norm_g):
    H = MLSTM_HEADS
    qk, v, o = _split(p, (2 * MLSTM_QK_WIDTH, MLSTM_DIM, MLSTM_DIM))
    ig, fg = _split(p_gates, (H, H))
    S = qk.shape[1]
    qkp = jnp.pad(qk, ((0, 0), (MLSTM_CONV - 1, 0), (0, 0)))
    conv = conv_b
    for j in range(MLSTM_CONV):
        conv = conv + qkp[:, j:j + S] * conv_w[j]
    qk = jax.nn.silu(conv)
    q, k = jnp.split(qk, 2, axis=-1)
    q = _heads(q, H) * (MLSTM_QK_DIM ** -0.5)
    k = _heads(k, H)
    v = _heads(v, H)
    i_pre = jnp.moveaxis(ig + i_b, -1, 1)
    logf = jax.nn.log_sigmoid(jnp.moveaxis(fg + f_b, -1, 1))
    h = _mlstm_chunkwise(q, k, v, i_pre, logf)
    h = _rms(h, norm_g.reshape(H, 1, MLSTM_V_DIM))
    return jax.nn.sigmoid(o) * _merge_heads(h)


def _moba(q, k, v, k_mean):
    B, H, S, d = q.shape
    BLK = MOBA_BLOCK
    nb = S // BLK
    pos = jnp.arange(S)
    qblk = pos // BLK
    gate = jnp.einsum('bhsd,bnhd->bhsn', q.astype(F32), k_mean, precision=HIGHEST)
    gate = jnp.where(jnp.arange(nb)[None, :] < qblk[:, None], gate, NEG)
    n_top = min(MOBA_TOPK, nb)
    _, sel = lax.top_k(gate, n_top)
    slot_ok = jnp.arange(n_top)[None, :] < qblk[:, None]
    lanes = jnp.arange(LANES)
    hit = (sel[..., None] == lanes) & slot_ok[:, :, None]
    blk_mask = jnp.any(hit, axis=3) | (lanes[None, :] == qblk[:, None])
    return _attention(q, k, v, MOBA_BLOCK, d ** -0.5, block_mask=blk_mask.astype(BF16), block=BLK, out_dtype=BF16)


def _nsa(qb, k_slc, v_slc, k_win, v_win, kc_in, vc_in, gl, pos, k_norm_cmp, cmp_pe, cmp_w1, cmp_b1, cmp_w2):
    B, H, S, d = qb.shape
    scale = d ** -0.5

    ncmp = (S - NSA_CMP_BLOCK) // NSA_CMP_STRIDE + 1
    cstart = NSA_CMP_STRIDE * jnp.arange(ncmp)
    cend = cstart + NSA_CMP_BLOCK - 1
    cidx = cstart[:, None] + jnp.arange(NSA_CMP_BLOCK)[None, :]

    def compress(z, i):
        blocks = (z[:, cidx] + cmp_pe[i]).reshape(B, ncmp, NSA_CMP_BLOCK * d)
        return jax.nn.gelu(blocks @ cmp_w1[i] + cmp_b1[i]) @ cmp_w2[i]

    k_cmp = _rope(_rms(compress(kc_in, 0), k_norm_cmp), cend)
    v_cmp = compress(vc_in, 1)
    SB = NSA_SLC_BLOCK
    nsb = S // SB
    assert nsb <= LANES
    ncp = -(-ncmp // LANES) * LANES
    cs_np = NSA_CMP_STRIDE * np.arange(ncp)
    ss_np = SB * np.arange(LANES)
    overlap = ((cs_np[:, None] <= ss_np[None, :] + SB - 1) & (cs_np[:, None] + NSA_CMP_BLOCK - 1 >= ss_np[None, :])
               & (np.arange(ncp)[:, None] < ncmp) & (np.arange(LANES)[None, :] < nsb))
    cpad = ((0, 0), (0, ncp - ncmp), (0, 0))
    o_cmp, imp = _nsa_cmp_attention(qb, jnp.pad(k_cmp, cpad).astype(BF16), jnp.pad(v_cmp, cpad).astype(BF16),
                                    jnp.asarray(overlap, F32), ncmp, scale)
    imp = imp[:, :, :nsb]
    cur = pos // SB
    blk = jnp.arange(nsb)[None, :]
    forced = (blk == 0) | (blk == cur[:, None]) | (blk == cur[:, None] - 1)
    imp = jnp.where(forced, BIG, jnp.where(blk <= cur[:, None], imp, NEG))
    n_sel = min(NSA_SLC_TOPK, nsb)
    _, sel = lax.top_k(imp, n_sel)
    blk_mask = jnp.any(sel[..., None] == jnp.arange(LANES), axis=2)
    o_slc = _attention(qb, k_slc, v_slc, NSA_Q_TILE, scale, block_mask=blk_mask[:, None].astype(BF16), block=SB)
    o_win = _attention(qb, k_win, v_win, NSA_Q_TILE, scale, window=NSA_WINDOW)

    gate_logits, lane_block = gl

    def rows(z):
        return z.reshape(B * S, H * d)

    return _nsa_gate(gate_logits, lane_block, rows(o_cmp), rows(o_slc), rows(o_win), H).reshape(B, S, H * d)


def _moe(xb, logits, x_packed, x_res, gate, seq_len, bias, w_gate, w_up, w_down, layer, s_gate, s_up, s_down):
    T, D = xb.shape
    A = T * TOP_K
    eidx, wts, rank, counts = _route(logits, bias)

    experts = jnp.arange(N_EXPERTS, dtype=jnp.int32)
    padded = (counts + MOE_BLOCK - 1) // MOE_BLOCK * MOE_BLOCK
    pstart = jnp.cumsum(padded) - padded
    slot = rank + jnp.sum(jnp.where(eidx[None] == experts[:, None, None], pstart[:, None, None], 0), axis=0)

    n_blk = -(-(A + N_EXPERTS * (MOE_BLOCK - 1)) // MOE_BLOCK)
    R = n_blk * MOE_BLOCK
    blk_e = jnp.minimum(jnp.sum((pstart + padded)[None, :] <= (jnp.arange(n_blk) * MOE_BLOCK)[:, None], axis=1),
                        N_EXPERTS - 1).astype(jnp.int32)
    n_used = (jnp.sum(padded) // MOE_BLOCK).astype(jnp.int32).reshape(1)
    tok_bits = max(int(T - 1).bit_length(), 1)
    tok_ids = jnp.arange(T, dtype=jnp.int32)[None, :]
    real_keys = (eidx << (tok_bits + 1)) | tok_ids
    n_fill = MOE_BLOCK - 1
    fill_j = jnp.arange(n_fill, dtype=jnp.int32)[None, :]
    unused = jnp.int32(N_EXPERTS << (tok_bits + 1))
    fill_keys = jnp.where(fill_j < (padded - counts)[:, None], (experts[:, None] << (tok_bits + 1)) | (1 << tok_bits), unused)
    tail = jnp.full((R - A - N_EXPERTS * n_fill,), unused, jnp.int32)
    keys = lax.sort(jnp.concatenate([real_keys.reshape(A), fill_keys.reshape(-1), tail]))
    is_fill = ((keys >> tok_bits) & 1 == 1) | (keys >= unused)
    tok_buf = jnp.where(is_fill, jnp.arange(R, dtype=jnp.int32) % T, keys & ((1 << tok_bits) - 1))

    packed = _moe_grouped(x_packed[tok_buf], blk_e, n_used, w_gate, w_up, w_down, layer)
    mine = packed[slot.reshape(A)].reshape(TOP_K, T, D // 2)

    gu = _matmul(xb, jnp.concatenate([s_gate, s_up], axis=1).astype(BF16))
    return _moe_combine(gu, s_down.astype(BF16), x_res, gate, seq_len, mine, wts.T)


def _projection_layout():
    o_b = RWKV_IN_DIM
    o_c = o_b + MLSTM_IN_DIM
    o_d = o_c + MOBA_IN_DIM
    o_g = o_d + NSA_IN_DIM
    mlstm_main = MLSTM_IN_DIM - 2 * MLSTM_HEADS
    nsa_main = NSA_IN_DIM - 3 * NSA_HEADS
    spans = (('rwkv', 0, o_b), ('mlstm', o_b, o_b + mlstm_main), ('moba', o_c, o_d), ('nsa', o_d, o_d + nsa_main),
             ('mlstm_gates', o_b + mlstm_main, o_c), ('nsa_gates', o_d + nsa_main, o_g))
    pieces, cur = [], 0
    for name, lo, hi in spans:
        pieces.append((name, lo, hi, cur))
        cur += -(-(hi - lo) // LANES) * LANES
    return tuple(pieces), cur, o_g


def _relayout_kernel(w_ref, mix_ref, gates_ref, *, pieces, gate_lo):
    for _, lo, hi, off in pieces:
        width = hi - lo
        mix_ref[:, off:off + width] = w_ref[0, :, lo:hi].astype(BF16)
        pad = -width % LANES
        if pad:
            mix_ref[:, off + width:off + width + pad] = jnp.zeros((mix_ref.shape[0], pad), BF16)
    gates_ref[...] = w_ref[0, :, gate_lo:].astype(BF16)


def _projection_weights(w_in, layer, rows=128):
    _, D, n_in = w_in.shape
    pieces, mix_width, gate_lo = _projection_layout()
    w_mix, w_gates = pl.pallas_call(
        functools.partial(_relayout_kernel, pieces=pieces, gate_lo=gate_lo),
        grid=(D // rows,),
        in_specs=[pl.BlockSpec((1, rows, n_in), lambda i: (layer, i, 0))],
        out_specs=[pl.BlockSpec((rows, mix_width), lambda i: (i, 0)),
                   pl.BlockSpec((rows, n_in - gate_lo), lambda i: (i, 0))],
        out_shape=[jax.ShapeDtypeStruct((D, mix_width), BF16), jax.ShapeDtypeStruct((D, n_in - gate_lo), BF16)],
        compiler_params=pltpu.CompilerParams(dimension_semantics=("parallel",),
                                             vmem_limit_bytes=V7X_VMEM_LIMIT_BYTES),
        name="projection_weight_relayout",
    )(w_in)
    return w_mix, w_gates, {name: off for name, _, _, off in pieces}


def kernel(x, c, ada_w, ada_b, norm_mix_g, norm_ffn_g, w_in, rwkv_mu, rwkv_w0, rwkv_w_up, rwkv_a0, rwkv_a_up, rwkv_g_up, rwkv_k_k, rwkv_k_a, rwkv_r_k, rwkv_ln_w, rwkv_ln_b, rwkv_v0, rwkv_v_down, rwkv_v_up, mlstm_conv_w, mlstm_conv_b, mlstm_i_b, mlstm_f_b, mlstm_norm_g, moba_q_norm, moba_k_norm, nsa_q_norm, nsa_k_norm, nsa_cmp_pe, nsa_cmp_w1, nsa_cmp_b1, nsa_cmp_w2, w_branch, w_out, moe_router, moe_bias, moe_w_gate, moe_w_up, moe_w_down, shared_w_gate, shared_w_up, shared_w_down):
    B, S, D = x.shape
    T = B * S
    depth = ada_w.shape[0]
    pos = jnp.arange(S)
    cond = jax.nn.silu(c)
    v_first = None
    x2d = x.reshape(T, D)
    for l in range(depth):
        mod = jnp.dot(cond, ada_w[l], precision=HIGHEST) + ada_b[l]
        sh_mix, sc_mix, gate_mix, sh_ffn, sc_ffn, gate_ffn = [z[:, None, :] for z in jnp.split(mod, 6, axis=-1)]

        hb = _normmod(x2d, norm_mix_g[l], sc_mix, sh_mix, S)
        w_mix, w_gates, offs = _projection_weights(w_in, l)

        def proj(name, width, dtype=F32, n_pieces=1):
            padded = -(-width // LANES) * LANES * n_pieces
            return _matmul(hb, w_mix, out_dtype=dtype, cols=(offs[name], padded)).reshape(B, S, padded)

        small = proj('mlstm_gates', 2 * MLSTM_HEADS, n_pieces=2)
        nsa_gate_off = offs['nsa_gates'] - offs['mlstm_gates']
        v_mix = None if l == 0 else (rwkv_v0[l - 1], rwkv_v_down[l - 1], rwkv_v_up[l - 1])
        y_a, v_first = _rwkv7(proj('rwkv', RWKV_IN_DIM)[:, :, :RWKV_IN_DIM], v_first, v_mix, rwkv_mu[l], rwkv_w0[l],
                              rwkv_w_up[l], rwkv_a0[l], rwkv_a_up[l], rwkv_g_up[l], rwkv_k_k[l], rwkv_k_a[l],
                              rwkv_r_k[l], rwkv_ln_w[l], rwkv_ln_b[l])
        y_b = _mlstm(proj('mlstm', MLSTM_IN_DIM - 2 * MLSTM_HEADS), small[:, :, :2 * MLSTM_HEADS],
                     mlstm_conv_w[l], mlstm_conv_b[l], mlstm_i_b[l], mlstm_f_b[l], mlstm_norm_g[l])
        p_moba = proj('moba', MOBA_IN_DIM, BF16).reshape(T, MOBA_IN_DIM)
        hm = MOBA_HEADS
        q_c, k_c, v_c, k_mean = _head_prep(p_moba, jnp.stack([moba_q_norm[l], moba_k_norm[l]]),
                                           ((hm, 0, 0), (hm, hm, 1), (hm, 2 * hm, None)), S, with_mean=True)
        y_c = _moba(q_c, k_c, v_c, k_mean)
        nsa_main = NSA_IN_DIM - 3 * NSA_HEADS
        p_nsa = proj('nsa', nsa_main, BF16)
        hn = NSA_HEADS
        q_d, ks_d, vs_d, kw_d, vw_d = _head_prep(
            p_nsa.reshape(T, nsa_main), jnp.stack([nsa_q_norm[l], nsa_k_norm[l, 1], nsa_k_norm[l, 2]]),
            ((hn, 0, 0), (1, hn + 2, 1), (1, hn + 3, None), (1, hn + 4, 2), (1, hn + 5, None)), S)
        cmp_in = p_nsa[:, :, NSA_DIM:NSA_DIM + 2 * ATTN_HEAD_DIM].astype(F32)
        y_d = _nsa(q_d, ks_d, vs_d, kw_d, vw_d, cmp_in[:, :, :ATTN_HEAD_DIM], cmp_in[:, :, ATTN_HEAD_DIM:],
                   (small.reshape(T, -1), nsa_gate_off // LANES), pos, nsa_k_norm[l, 0],
                   nsa_cmp_pe[l], nsa_cmp_w1[l], nsa_cmp_b1[l], nsa_cmp_w2[l])
        ys = [y.reshape(T, BRANCH_DIM).astype(BF16) for y in (y_a, y_b, y_c, y_d)]
        merged = _merge_branches(hb, w_gates, ys, w_branch[l].astype(BF16))
        x2d = _matmul_residual(merged, w_out[l].astype(BF16), x2d, gate_mix, S)

        hb, logits, hb_packed = _normmod(x2d, norm_ffn_g[l], sc_ffn, sh_ffn, S, router=moe_router[l])
        x2d = _moe(hb, logits, hb_packed, x2d, gate_ffn, S, moe_bias[l], moe_w_gate, moe_w_up, moe_w_down, l,
                   shared_w_gate[l], shared_w_up[l], shared_w_down[l])
    return x2d.reshape(B, S, D)
```
